```python
import functools
import math
import jax
import jax.numpy as jnp
from jax import lax
import numpy as np


D_MODEL = 1024
BATCH = 1
SEQ = 16384
DEPTH = 1
DEC_BATCH = 128
DEC_SEQ = 8
PAST_LEN = 8192
PAGE_SIZE = 128

D_MIX = D_MODEL
D_ATT = D_MIX // 2
D_SSM = D_MIX - D_ATT
HEAD_DIM = 64
N_HEADS = D_ATT // HEAD_DIM
ROT_DIM = HEAD_DIM // 4
ROPE_THETA = 500000.0
DILATED_GROUPS = ((128, 1), (512, 4), (2048, 16))
WIN_MAX = 2048
Q_BLOCK = 128
SSM_GROUP = 16
N_SSM_GROUPS = D_SSM // SSM_GROUP
SSM_STATE = 64
D_FF = 2816
N_ADA = 9
EPS = 1e-6
DT_MIN = 1e-3
DT_MAX = 1e-1

kernel_name = "hymba_dilated_s5_macaron_step"


def rmsnorm(x, g):
    xf = x.astype(jnp.float32)
    y = xf * lax.rsqrt(jnp.mean(xf * xf, axis=-1, keepdims=True) + EPS)
    return (y * g.astype(jnp.float32)).astype(x.dtype)


def ada_terms(c, w_ada, b_ada):
    h = jax.nn.silu(c) @ w_ada + b_ada
    return jnp.split(h[:, None, :], N_ADA, axis=-1)


def modulate(x, shift, scale):
    return x * (1.0 + scale) + shift


def swiglu(x, w_gate, w_up, w_down):
    return (jax.nn.silu(x @ w_gate) * (x @ w_up)) @ w_down


def rope_partial(x, pos):
    half = ROT_DIM // 2
    inv = ROPE_THETA ** (-jnp.arange(half, dtype=jnp.float32) / half)
    ang = pos.astype(jnp.float32)[:, None] * inv[None, :]
    cos = jnp.cos(ang)[None, :, None, :]
    sin = jnp.sin(ang)[None, :, None, :]
    xr = x[..., :ROT_DIM].astype(jnp.float32)
    x1, x2 = xr[..., :half], xr[..., half:]
    rot = jnp.concatenate([x1 * cos - x2 * sin, x2 * cos + x1 * sin], axis=-1)
    return jnp.concatenate([rot.astype(x.dtype), x[..., ROT_DIM:]], axis=-1)


def dilated_attention(q, k_ext, v_ext, q_idx):
    qf = q.astype(jnp.float32) * (HEAD_DIM ** -0.5)
    outs, lses = [], []
    for window, dil in DILATED_GROUPS:
        offs = jnp.arange(window // dil + 1) * dil
        idx = q_idx[:, None] - offs[None, :]
        valid = idx >= 0
        idx = jnp.maximum(idx, 0)
        kg = k_ext[:, idx].astype(jnp.float32)
        vg = v_ext[:, idx].astype(jnp.float32)
        s = jnp.einsum("bqhd,bqjhd->bqhj", qf, kg)
        s = jnp.where(valid[None, :, None, :], s, -jnp.inf)
        lse = jax.nn.logsumexp(s, axis=-1)
        p = jnp.exp(s - lse[..., None])
        outs.append(jnp.einsum("bqhj,bqjhd->bqhd", p, vg))
        lses.append(lse)
    w = jax.nn.softmax(jnp.stack(lses, axis=0), axis=0)
    o = jnp.sum(w[..., None] * jnp.stack(outs, axis=0), axis=0)
    return o.astype(q.dtype)


def prompt_attend(q, k, v):
    b, s, h, hd = q.shape
    nb = s // Q_BLOCK
    qb = q.reshape(b, nb, Q_BLOCK, h, hd).transpose(1, 0, 2, 3, 4)
    starts = jnp.arange(nb) * Q_BLOCK

    def blk(args):
        qi, st = args
        return dilated_attention(qi, k, v, st + jnp.arange(Q_BLOCK))

    o = lax.map(blk, (qb, starts)).transpose(1, 0, 2, 3, 4).reshape(b, s, h, hd)
    keep = min(WIN_MAX, s)
    return o, k[:, s - keep:], v[:, s - keep:]


def buffered_attend(q, k, v, k_buf, v_buf):
    w_buf = k_buf.shape[1]
    k_ext = jnp.concatenate([k_buf.astype(k.dtype), k], axis=1)
    v_ext = jnp.concatenate([v_buf.astype(v.dtype), v], axis=1)
    o = dilated_attention(q, k_ext, v_ext, w_buf + jnp.arange(q.shape[1]))
    return o, k_ext[:, -w_buf:], v_ext[:, -w_buf:]


def s5_mixer(u, h0_re, h0_im, a_re, a_im, log_dt, b_re, b_im, c_re, c_im, d_skip):
    b, s, _ = u.shape
    uf = u.astype(jnp.float32).reshape(b, s, N_SSM_GROUPS, SSM_GROUP)
    lam = lax.complex(a_re.astype(jnp.float32), a_im.astype(jnp.float32))
    dt = jnp.exp(log_dt.astype(jnp.float32))[:, None]
    lam_bar = jnp.exp(lam * dt)
    bmat = lax.complex(b_re.astype(jnp.float32), b_im.astype(jnp.float32))
    b_bar = ((lam_bar - 1.0) / lam)[..., None] * bmat
    cmat = lax.complex(c_re.astype(jnp.float32), c_im.astype(jnp.float32))
    bu = jnp.einsum("gpc,bsgc->bsgp", b_bar, uf.astype(jnp.complex64))
    h0 = lax.complex(h0_re.astype(jnp.float32), h0_im.astype(jnp.float32))
    bu = bu.at[:, 0].add(lam_bar[None] * h0)
    a = jnp.broadcast_to(lam_bar, bu.shape)

    def combine(left, right):
        return (left[0] * right[0], right[0] * left[1] + right[1])

    _, h = lax.associative_scan(combine, (a, bu), axis=1)
    y = jnp.einsum("gcp,bsgp->bsgc", cmat, h).real \
        + d_skip.astype(jnp.float32).reshape(N_SSM_GROUPS, SSM_GROUP) * uf
    h_last = h[:, -1]
    return y.reshape(b, s, D_SSM).astype(u.dtype), jnp.real(h_last), jnp.imag(h_last)


def decoder_layer(x, c, pos, attend, h0_re, h0_im, p):
    (w_ada, b_ada, g_ffn1, w1_gate, w1_up, w1_down, g_mix, w_in, g_q, g_k,
     a_re, a_im, log_dt, b_re, b_im, c_re, c_im, d_skip, w_glu, w_out,
     g_ffn2, w2_gate, w2_up, w2_down) = p
    sh1, sc1, gt1, sh2, sc2, gt2, sh3, sc3, gt3 = ada_terms(c, w_ada, b_ada)
    b, s, _ = x.shape
    x = x + 0.5 * gt1 * swiglu(modulate(rmsnorm(x, g_ffn1), sh1, sc1), w1_gate, w1_up, w1_down)
    h = modulate(rmsnorm(x, g_mix), sh2, sc2)
    proj = h @ w_in
    q, k, v, u = jnp.split(proj, [D_ATT, 2 * D_ATT, 3 * D_ATT], axis=-1)
    q = rope_partial(rmsnorm(q.reshape(b, s, N_HEADS, HEAD_DIM), g_q), pos)
    k = rope_partial(rmsnorm(k.reshape(b, s, N_HEADS, HEAD_DIM), g_k), pos)
    v = v.reshape(b, s, N_HEADS, HEAD_DIM)
    o_att, k_state, v_state = attend(q, k, v)
    y_ssm, h_re, h_im = s5_mixer(u, h0_re, h0_im, a_re, a_im, log_dt, b_re, b_im, c_re, c_im, d_skip)
    g = jax.nn.gelu(y_ssm)
    ga, gb = jnp.split(g @ w_glu, 2, axis=-1)
    o_ssm = ga * jax.nn.sigmoid(gb)
    mix = jnp.concatenate([o_att.reshape(b, s, D_ATT), o_ssm], axis=-1) @ w_out
    x = x + gt2 * mix
    x = x + 0.5 * gt3 * swiglu(modulate(rmsnorm(x, g_ffn2), sh3, sc3), w2_gate, w2_up, w2_down)
    return x, k_state, v_state, h_re, h_im


def setup_inputs(seed: int = 0) -> dict:
    key = jax.random.key(seed)
    ks = jax.random.split(key, 40)
    f32 = jnp.float32
    nrm = lambda k, shape, sc: jax.random.normal(k, shape, f32) * sc
    w_buf = min(WIN_MAX, PAST_LEN)
    L = DEPTH
    n_idx = jnp.arange(SSM_STATE, dtype=f32)
    a_im = jnp.broadcast_to(math.pi * n_idx, (L, N_SSM_GROUPS, SSM_STATE)) \
        + nrm(ks[20], (L, N_SSM_GROUPS, SSM_STATE), 0.01)
    return {
        "x_prompt": nrm(ks[0], (BATCH, SEQ, D_MODEL), 1.0),
        "x_sample": nrm(ks[1], (DEC_BATCH, DEC_SEQ, D_MODEL), 1.0),
        "c_prompt": nrm(ks[2], (BATCH, D_MODEL), 1.0),
        "c_sample": nrm(ks[3], (DEC_BATCH, D_MODEL), 1.0),
        "cache_k_win": nrm(ks[4], (L, DEC_BATCH, w_buf, N_HEADS, HEAD_DIM), 1.0),
        "cache_v_win": nrm(ks[5], (L, DEC_BATCH, w_buf, N_HEADS, HEAD_DIM), 1.0),
        "state_ssm_re": nrm(ks[6], (L, DEC_BATCH, N_SSM_GROUPS, SSM_STATE), 0.1),
        "state_ssm_im": nrm(ks[7], (L, DEC_BATCH, N_SSM_GROUPS, SSM_STATE), 0.1),
        "w_ada": nrm(ks[8], (L, D_MODEL, N_ADA * D_MODEL), 0.5 * D_MODEL ** -0.5),
        "b_ada": nrm(ks[9], (L, N_ADA * D_MODEL), 0.02),
        "g_ffn1": 1.0 + nrm(ks[10], (L, D_MODEL), 0.02),
        "w1_gate": nrm(ks[11], (L, D_MODEL, D_FF), D_MODEL ** -0.5),
        "w1_up": nrm(ks[12], (L, D_MODEL, D_FF), D_MODEL ** -0.5),
        "w1_down": nrm(ks[13], (L, D_FF, D_MODEL), D_FF ** -0.5),
        "g_mix": 1.0 + nrm(ks[14], (L, D_MODEL), 0.02),
        "w_in": nrm(ks[15], (L, D_MODEL, 3 * D_ATT + D_SSM), D_MODEL ** -0.5),
        "g_q": 1.0 + nrm(ks[16], (L, HEAD_DIM), 0.02),
        "g_k": 1.0 + nrm(ks[17], (L, HEAD_DIM), 0.02),
        "ssm_a_re": -0.5 + nrm(ks[18], (L, N_SSM_GROUPS, SSM_STATE), 0.01),
        "ssm_a_im": a_im,
        "ssm_log_dt": jax.random.uniform(ks[19], (L, N_SSM_GROUPS), f32,
                                         math.log(DT_MIN), math.log(DT_MAX)),
        "ssm_b_re": nrm(ks[21], (L, N_SSM_GROUPS, SSM_STATE, SSM_GROUP), (2.0 * SSM_GROUP) ** -0.5),
        "ssm_b_im": nrm(ks[22], (L, N_SSM_GROUPS, SSM_STATE, SSM_GROUP), (2.0 * SSM_GROUP) ** -0.5),
        "ssm_c_re": nrm(ks[23], (L, N_SSM_GROUPS, SSM_GROUP, SSM_STATE), (2.0 * SSM_STATE) ** -0.5),
        "ssm_c_im": nrm(ks[24], (L, N_SSM_GROUPS, SSM_GROUP, SSM_STATE), (2.0 * SSM_STATE) ** -0.5),
        "ssm_d": nrm(ks[25], (L, D_SSM), 1.0),
        "w_glu": nrm(ks[26], (L, D_SSM, 2 * D_SSM), D_SSM ** -0.5),
        "w_out": nrm(ks[27], (L, D_MIX, D_MODEL), D_MIX ** -0.5),
        "g_ffn2": 1.0 + nrm(ks[28], (L, D_MODEL), 0.02),
        "w2_gate": nrm(ks[29], (L, D_MODEL, D_FF), D_MODEL ** -0.5),
        "w2_up": nrm(ks[30], (L, D_MODEL, D_FF), D_MODEL ** -0.5),
        "w2_down": nrm(ks[31], (L, D_FF, D_MODEL), D_FF ** -0.5),
    }


def reference(x_prompt, x_sample, c_prompt, c_sample, cache_k_win, cache_v_win,
              state_ssm_re, state_ssm_im, w_ada, b_ada, g_ffn1, w1_gate, w1_up, w1_down,
              g_mix, w_in, g_q, g_k, ssm_a_re, ssm_a_im, ssm_log_dt, ssm_b_re, ssm_b_im,
              ssm_c_re, ssm_c_im, ssm_d, w_glu, w_out, g_ffn2, w2_gate, w2_up, w2_down):
    pos_p = jnp.arange(x_prompt.shape[1])
    pos_s = PAST_LEN + jnp.arange(x_sample.shape[1])
    h0_p = jnp.zeros((x_prompt.shape[0], N_SSM_GROUPS, SSM_STATE), jnp.float32)
    y_p, y_s = x_prompt, x_sample
    kp_l, vp_l, hrp_l, hip_l, ks_l, vs_l, hrs_l, his_l = [], [], [], [], [], [], [], []
    for l in range(DEPTH):
        p = (w_ada[l], b_ada[l], g_ffn1[l], w1_gate[l], w1_up[l], w1_down[l], g_mix[l], w_in[l],
             g_q[l], g_k[l], ssm_a_re[l], ssm_a_im[l], ssm_log_dt[l], ssm_b_re[l], ssm_b_im[l],
             ssm_c_re[l], ssm_c_im[l], ssm_d[l], w_glu[l], w_out[l], g_ffn2[l], w2_gate[l],
             w2_up[l], w2_down[l])
        y_p, kp, vp, hrp, hip = decoder_layer(y_p, c_prompt, pos_p, prompt_attend, h0_p, h0_p, p)
        sample_attend = functools.partial(buffered_attend, k_buf=cache_k_win[l], v_buf=cache_v_win[l])
        y_s, ksm, vsm, hrs, his = decoder_layer(y_s, c_sample, pos_s, sample_attend,
                                                state_ssm_re[l], state_ssm_im[l], p)
        kp_l.append(kp); vp_l.append(vp); hrp_l.append(hrp); hip_l.append(hip)
        ks_l.append(ksm); vs_l.append(vsm); hrs_l.append(hrs); his_l.append(his)
    return (y_p, y_s,
            jnp.stack(kp_l, 0), jnp.stack(vp_l, 0), jnp.stack(hrp_l, 0), jnp.stack(hip_l, 0),
            jnp.stack(ks_l, 0), jnp.stack(vs_l, 0), jnp.stack(hrs_l, 0), jnp.stack(his_l, 0))
```

```python
import functools
import math

import numpy as np
import jax
import jax.numpy as jnp
from jax import lax
from jax.experimental import pallas as pl
from jax.experimental.pallas import tpu as pltpu

F32 = jnp.float32
BF16 = jnp.bfloat16

D_MODEL = 1024
D_ATT = 512
D_SSM = 512
HEAD_DIM = 64
N_HEADS = 8
ROT_DIM = 16
ROPE_THETA = 500000.0
DILATIONS = (1, 4, 16)
WIN_STEPS = 128
WIN_MAX = 2048
PAST_LEN = 8192
SSM_GROUP = 16
N_SSM_GROUPS = 32
SSM_STATE = 64
D_FF = 2816
N_ADA = 9
EPS = 1e-6

LANES = 128
CHUNK = 8
SUPER = D_SSM // LANES
SG_STATE = (LANES // SSM_GROUP) * SSM_STATE
CW = CHUNK * LANES
NEG = -1e30
VMEM_LIMIT = 56 * 1024 * 1024

_NT = (((1,), (1,)), ((), ()))


def _params(sem, vmem=VMEM_LIMIT):
    return pltpu.CompilerParams(dimension_semantics=sem, vmem_limit_bytes=vmem)


def _const_spec(shape):
    nd = len(shape)
    return pl.BlockSpec(shape, lambda *_: (0,) * nd, pipeline_mode=pl.Buffered(1))


def _dot(a, b):
    return jnp.dot(a, b, preferred_element_type=F32)


def _rms(x, g):
    ms = jnp.mean(x * x, axis=-1, keepdims=True)
    return x * lax.rsqrt(ms + EPS) * g


def _swiglu(h, wg_ref, wu_ref, wd_ref):
    a = _dot(h, wg_ref[...])
    b = _dot(h, wu_ref[...])
    t = (a * (1.0 / (1.0 + jnp.exp(-a))) * b).astype(BF16)
    return _dot(t, wd_ref[...])


def _ada_kernel(c_ref, w_ref, b_ref, o_ref):
    c = c_ref[...]
    s = (c * (1.0 / (1.0 + jnp.exp(-c)))).astype(BF16)
    o_ref[...] = _dot(s, w_ref[...].astype(BF16)) + b_ref[...]


def _ada(c, w_ada, b_ada):
    m = c.shape[0]
    n = w_ada.shape[1]
    tn = n // N_ADA
    return pl.pallas_call(
        _ada_kernel,
        out_shape=jax.ShapeDtypeStruct((m, n), F32),
        grid=(n // tn,),
        in_specs=[pl.BlockSpec((m, D_MODEL), lambda j: (0, 0)),
                  pl.BlockSpec((D_MODEL, tn), lambda j: (0, j)),
                  pl.BlockSpec((1, tn), lambda j: (0, j))],
        out_specs=pl.BlockSpec((m, tn), lambda j: (0, j)),
        compiler_params=_params(("arbitrary",)),
        name="ada",
    )(c, w_ada, b_ada.reshape(1, n))


def _stage_a_kernel(x_ref, ada_ref, cos_ref, sina_ref, sinb_ref, gf_ref, gm_ref, gq_ref, gk_ref,
                    seg_ref, wg_ref, wu_ref, wd_ref, win_ref,
                    x1_ref, q_ref, k_ref, v_ref, u_ref):
    kb, r, _ = x_ref.shape
    n = kb * r
    x = x_ref[...]
    ada = ada_ref[...]
    sh1, sc1, gt1 = ada[:, 0:1], ada[:, 1:2], ada[:, 2:3]
    sh2, sc2 = ada[:, 3:4], ada[:, 4:5]

    h = _rms(x, gf_ref[...]) * (1.0 + sc1) + sh1
    f = _swiglu(h.reshape(n, D_MODEL).astype(BF16), wg_ref, wu_ref, wd_ref)
    x1 = x + 0.5 * gt1 * f.reshape(kb, r, D_MODEL)
    x1_ref[...] = x1

    h = _rms(x1, gm_ref[...]) * (1.0 + sc2) + sh2
    proj = _dot(h.reshape(n, D_MODEL).astype(BF16), win_ref[...])

    cos = jnp.concatenate([cos_ref[...]] * (D_ATT // LANES), axis=-1)
    sina = jnp.concatenate([sina_ref[...]] * (D_ATT // LANES), axis=-1)
    sinb = jnp.concatenate([sinb_ref[...]] * (D_ATT // LANES), axis=-1)

    def head_norm_rope(z, g):
        ms = _dot((z * z).astype(BF16), seg_ref[...])
        zn = z * lax.rsqrt(ms + EPS) * g
        up = pltpu.roll(zn, D_ATT - ROT_DIM // 2, 1).reshape(kb, r, D_ATT)
        dn = pltpu.roll(zn, ROT_DIM // 2, 1).reshape(kb, r, D_ATT)
        return zn.reshape(kb, r, D_ATT) * cos + up * sina + dn * sinb

    q_ref[...] = head_norm_rope(proj[:, 0:D_ATT], gq_ref[...])
    k_ref[...] = head_norm_rope(proj[:, D_ATT:2 * D_ATT], gk_ref[...])
    v_ref[...] = proj[:, 2 * D_ATT:3 * D_ATT].reshape(kb, r, D_ATT)
    u = proj[:, 3 * D_ATT:]
    for sg in range(SUPER):
        u_ref[sg] = u[:, sg * LANES:(sg + 1) * LANES]


def _rope_tables(pos):
    half = ROT_DIM // 2
    j = np.arange(LANES) % HEAD_DIM
    lo = jnp.asarray(j < half)
    hi = jnp.asarray((j >= half) & (j < ROT_DIM))
    inv = ROPE_THETA ** (-jnp.asarray(j % half, F32) / half)
    ang = pos.astype(F32)[:, None] * inv[None, :]
    c = jnp.cos(ang)
    s = jnp.sin(ang)
    cos = jnp.where(lo | hi, c, 1.0)
    sina = jnp.where(lo, -s, 0.0)
    sinb = jnp.where(hi, s, 0.0)
    return cos, sina, sinb


def _stage_a(x3, ada3, tables, per_seq, wts, tm):
    b3, r3, _ = x3.shape
    if per_seq:
        kb, r = min(tm // r3, b3), r3
        grid = (b3 // kb,)
        tok = lambda i: (i, 0, 0)
        tab = lambda i: (0, 0, 0)
        ada_spec = pl.BlockSpec((kb, N_ADA, D_MODEL), lambda i: (i, 0, 0))
        u_map = lambda i: (0, i, 0)
    else:
        kb, r = 1, min(tm, r3)
        grid = (r3 // r,)
        tok = lambda i: (0, i, 0)
        tab = lambda i: (0, i, 0)
        ada_spec = pl.BlockSpec((1, N_ADA, D_MODEL), lambda i: (0, 0, 0))
        u_map = lambda i: (0, i, 0)
    n = kb * r
    ntok = b3 * r3
    cos, sina, sinb = tables
    gf, gm, gq, gk, seg, wg, wu, wd, win = wts
    tab_spec = pl.BlockSpec((1, r, LANES), tab)
    att_spec = pl.BlockSpec((kb, r, D_ATT), tok)
    att_shape = jax.ShapeDtypeStruct((b3, r3, D_ATT), F32)
    return pl.pallas_call(
        _stage_a_kernel,
        out_shape=(jax.ShapeDtypeStruct(x3.shape, F32), att_shape, att_shape, att_shape,
                   jax.ShapeDtypeStruct((SUPER, ntok, LANES), F32)),
        grid=grid,
        in_specs=[pl.BlockSpec((kb, r, D_MODEL), tok), ada_spec, tab_spec, tab_spec, tab_spec,
                  _const_spec(gf.shape), _const_spec(gm.shape), _const_spec(gq.shape),
                  _const_spec(gk.shape), _const_spec(seg.shape), _const_spec(wg.shape),
                  _const_spec(wu.shape), _const_spec(wd.shape), _const_spec(win.shape)],
        out_specs=(pl.BlockSpec((kb, r, D_MODEL), tok), att_spec, att_spec, att_spec,
                   pl.BlockSpec((SUPER, n, LANES), u_map)),
        compiler_params=_params(("arbitrary",)),
        name="stage_a",
    )(x3, ada3, cos, sina, sinb, gf, gm, gq, gk, seg, wg, wu, wd, win)


def _dil_attn_kernel(q_ref, kp_ref, kc_ref, vp_ref, vc_ref, o_ref, lse_ref):
    m = pl.program_id(1)
    tq = q_ref.shape[0]
    q = (q_ref[...] * (HEAD_DIM ** -0.5)).astype(BF16)
    kk = jnp.concatenate([kp_ref[...], kc_ref[...]], axis=0).astype(BF16)
    vv = jnp.concatenate([vp_ref[...], vc_ref[...]], axis=0).astype(BF16)
    row = lax.broadcasted_iota(jnp.int32, (tq, 2 * tq), 0)
    col = lax.broadcasted_iota(jnp.int32, (tq, 2 * tq), 1)
    first = jnp.where(m > 0, 0, tq)
    bias = jnp.where(col >= jnp.maximum(row, first), 0.0, NEG)
    bias = jnp.where(col <= row + WIN_STEPS, bias, NEG)
    lane = lax.broadcasted_iota(jnp.int32, (tq, LANES), 1)
    low = lane < HEAD_DIM
    for hp in range(D_ATT // LANES):
        sl = slice(hp * LANES, (hp + 1) * LANES)
        qp, kp, vp = q[:, sl], kk[:, sl], vv[:, sl]
        outs, lses = [], []
        for e in range(2):
            qe = jnp.where(low if e == 0 else jnp.logical_not(low), qp, jnp.zeros_like(qp))
            s = lax.dot_general(qe, kp, _NT, preferred_element_type=F32) + bias
            mx = jnp.max(s, axis=1, keepdims=True)
            p = jnp.exp(s - mx)
            l = jnp.sum(p, axis=1, keepdims=True)
            o = _dot(p.astype(BF16), vp)
            outs.append(o * (1.0 / l))
            lses.append(mx + jnp.log(l))
        o_ref[:, sl] = jnp.where(low, outs[0], outs[1])
        lse_ref[:, sl] = jnp.where(low, lses[0], lses[1])


def _dil_attn(q, k, v, d):
    s = q.shape[0]
    sd = s // d
    tq = WIN_STEPS
    qv, kv, vv = (a.reshape(sd, d * D_ATT) for a in (q, k, v))
    cur = pl.BlockSpec((tq, D_ATT), lambda r, m: (m, r))
    prev = pl.BlockSpec((tq, D_ATT), lambda r, m: (jnp.maximum(m - 1, 0), r))
    shape = jax.ShapeDtypeStruct((sd, d * D_ATT), F32)
    o, lse = pl.pallas_call(
        _dil_attn_kernel,
        out_shape=(shape, shape),
        grid=(d, sd // tq),
        in_specs=[cur, prev, cur, prev, cur],
        out_specs=(cur, cur),
        compiler_params=_params(("arbitrary", "arbitrary")),
        name=f"dil_attn_{d}",
    )(qv, kv, kv, vv, vv)
    return o.reshape(1, s, D_ATT), lse.reshape(1, s, D_ATT)


def _sample_bias():
    w = WIN_MAX
    out = []
    i = (np.arange(N_HEADS * CHUNK) % CHUNK)[:, None]
    for d in DILATIONS:
        span = WIN_STEPS * d
        c = np.arange(w - span, w)[None, :]
        dist = w + i - c
        ok_buf = (dist % d == 0) & (dist <= span)
        cn = np.arange(LANES)[None, :] - (LANES - CHUNK)
        dn = i - cn
        ok_new = (cn >= 0) & (dn >= 0) & (dn % d == 0) & (dn <= span)
        ok = np.concatenate([ok_buf, ok_new], axis=1)
        out.append(jnp.asarray(np.where(ok, 0.0, NEG), F32))
    return out


def _sample_attn_kernel(q_ref, kn_ref, vn_ref, kc_ref, vc_ref, b1_ref, b2_ref, b3_ref,
                        o_ref, ko_ref, vo_ref):
    w = kc_ref.shape[2]
    nq = q_ref.shape[1]
    kc = kc_ref[0]
    vc = vc_ref[0]
    lane = lax.broadcasted_iota(jnp.int32, (D_ATT, LANES), 1)
    pad = jnp.zeros((LANES - nq, D_ATT), F32)

    def shifted(buf, new, out_ref):
        newt = jnp.concatenate([pad, new], axis=0).T
        rolled = pltpu.roll(buf, w - nq, 1)
        out_ref[0, :, 0:w - LANES] = rolled[:, 0:w - LANES]
        out_ref[0, :, w - LANES:] = jnp.where(lane >= LANES - nq, newt, rolled[:, w - LANES:])
        return newt

    knb = shifted(kc, kn_ref[0], ko_ref).astype(BF16)
    vnb = shifted(vc, vn_ref[0], vo_ref).astype(BF16)

    rows = N_HEADS * nq
    q = q_ref[0] * (HEAD_DIM ** -0.5)
    qt = jnp.concatenate([q] * N_HEADS, axis=0)
    rhead = lax.broadcasted_iota(jnp.int32, (rows, D_ATT), 0) // nq
    lhead = lax.broadcasted_iota(jnp.int32, (rows, D_ATT), 1) // HEAD_DIM
    qe = jnp.where(rhead == lhead, qt, 0.0).astype(BF16)
    kcb = kc.astype(BF16)
    vcb = vc.astype(BF16)
    s_buf = _dot(qe, kcb)
    s_new = _dot(qe, knb)

    outs, lses = [], []
    for d, b_ref in zip(DILATIONS, (b1_ref, b2_ref, b3_ref)):
        span = WIN_STEPS * d
        s = jnp.concatenate([s_buf[:, w - span:], s_new], axis=1) + b_ref[...]
        mx = jnp.max(s, axis=1, keepdims=True)
        p = jnp.exp(s - mx)
        l = jnp.sum(p, axis=1, keepdims=True)
        vcat = jnp.concatenate([vcb[:, w - span:], vnb], axis=1)
        o = lax.dot_general(p.astype(BF16), vcat, _NT, preferred_element_type=F32)
        outs.append(o * (1.0 / l))
        lses.append(mx + jnp.log(l))
    lmax = jnp.maximum(jnp.maximum(lses[0], lses[1]), lses[2])
    es = [jnp.exp(l - lmax) for l in lses]
    o = (es[0] * outs[0] + es[1] * outs[1] + es[2] * outs[2]) * (1.0 / (es[0] + es[1] + es[2]))
    o = jnp.where(rhead == lhead, o, 0.0)
    acc = o[0:nq]
    for h in range(1, N_HEADS):
        acc = acc + o[h * nq:(h + 1) * nq]
    o_ref[0] = acc


def _sample_attn(q, kn, vn, kc, vc):
    b, nq, _ = q.shape
    w = kc.shape[2]
    assert w == WIN_MAX and nq == CHUNK
    b1, b2, b3 = _sample_bias()
    new = pl.BlockSpec((1, nq, D_ATT), lambda i: (i, 0, 0))
    buf = pl.BlockSpec((1, D_ATT, w), lambda i: (i, 0, 0))
    return pl.pallas_call(
        _sample_attn_kernel,
        out_shape=(jax.ShapeDtypeStruct(q.shape, F32), jax.ShapeDtypeStruct(kc.shape, F32),
                   jax.ShapeDtypeStruct(vc.shape, F32)),
        grid=(b,),
        in_specs=[new, new, new, buf, buf, _const_spec(b1.shape), _const_spec(b2.shape),
                  _const_spec(b3.shape)],
        out_specs=(new, buf, buf),
        compiler_params=_params(("arbitrary",)),
        name="sample_attn",
    )(q, kn, vn, kc, vc, b1, b2, b3)


def _split(a):
    hi = a.astype(BF16)
    return hi, (a - hi.astype(F32)).astype(BF16)


def _s5_prep_kernel(ar_ref, ai_ref, ldt_ref, bre_ref, bim_ref, cre_ref, cim_ref,
                    p_ref, qt_ref, t_ref, a_ref, pf_scr, q2_scr):
    ar, ai = ar_ref[0], ai_ref[0]
    dt = jnp.exp(ldt_ref[0])
    mag = jnp.exp(ar * dt)
    lr, li = mag * jnp.cos(ai * dt), mag * jnp.sin(ai * dt)
    den = 1.0 / (ar * ar + ai * ai)
    kr = ((lr - 1.0) * ar + li * ai) * den
    ki = (li * ar - (lr - 1.0) * ai) * den
    bre, bim = bre_ref[0], bim_ref[0]
    bbr, bbi = kr * bre - ki * bim, kr * bim + ki * bre
    cre, cim = cre_ref[0], cim_ref[0]
    pr, pi = [jnp.ones_like(lr)], [jnp.zeros_like(lr)]
    for _ in range(CHUNK):
        pr.append(pr[-1] * lr - pi[-1] * li)
        pi.append(pr[-2] * li + pi[-1] * lr)
    dd = 1.0 / (pr[CHUNK] * pr[CHUNK] + pi[CHUNK] * pi[CHUNK])
    dr, di = pr[CHUNK] * dd, -pi[CHUNK] * dd
    for s in range(CHUNK):
        n = CHUNK - 1 - s
        rows = slice(s * LANES, (s + 1) * LANES)
        pf_scr[rows, 0:SG_STATE] = bbr * pr[n] - bbi * pi[n]
        pf_scr[rows, SG_STATE:] = bbr * pi[n] + bbi * pr[n]
        n = s + 1
        re, im = cre * pr[n] - cim * pi[n], cre * pi[n] + cim * pr[n]
        qt_ref[0, rows, 0:SG_STATE] = re.astype(BF16)
        qt_ref[0, rows, SG_STATE:] = (-im).astype(BF16)
        q2_scr[rows, 0:SG_STATE] = dr * re - di * im
        q2_scr[rows, SG_STATE:] = -(dr * im + di * re)
    pf = pf_scr[...]
    p_ref[0] = pf.astype(BF16)
    a_ref[0] = jnp.concatenate([pr[CHUNK], pi[CHUNK]], axis=-1)
    ph, plo = _split(pf)
    qh, qlo = _split(q2_scr[...])
    t = (lax.dot_general(ph, qh, _NT, preferred_element_type=F32)
         + lax.dot_general(ph, qlo, _NT, preferred_element_type=F32)
         + lax.dot_general(plo, qh, _NT, preferred_element_type=F32))
    rb = lax.broadcasted_iota(jnp.int32, (CW, CW), 0) // LANES
    cb = lax.broadcasted_iota(jnp.int32, (CW, CW), 1) // LANES
    t_ref[0] = jnp.where(rb <= cb, t, 0.0).astype(BF16)


def _block_diag(a):
    g = LANES // SSM_GROUP
    _, r, c = a.shape
    a = a.reshape(SUPER, g, r, c)
    eye = jnp.eye(g, dtype=a.dtype)
    return jnp.einsum("sgrc,gh->sgrhc", a, eye).reshape(SUPER, g * r, g * c)


def _s5_prep(a_re, a_im, log_dt, b_re, b_im, c_re, c_im):
    row = lambda a: a.reshape(SUPER, 1, SG_STATE)
    ldt = jnp.broadcast_to(log_dt[:, None], (N_SSM_GROUPS, SSM_STATE))
    bre = _block_diag(jnp.swapaxes(b_re, 1, 2))
    bim = _block_diag(jnp.swapaxes(b_im, 1, 2))
    cre = _block_diag(c_re)
    cim = _block_diag(c_im)
    vec = pl.BlockSpec((1, 1, SG_STATE), lambda g: (g, 0, 0))
    mat = pl.BlockSpec((1, LANES, SG_STATE), lambda g: (g, 0, 0))
    big = pl.BlockSpec((1, CW, CW), lambda g: (g, 0, 0))
    big_shape = jax.ShapeDtypeStruct((SUPER, CW, CW), BF16)
    return pl.pallas_call(
        _s5_prep_kernel,
        out_shape=(big_shape, big_shape, big_shape, jax.ShapeDtypeStruct((SUPER, 1, CW), F32)),
        grid=(SUPER,),
        in_specs=[vec, vec, vec, mat, mat, mat, mat],
        out_specs=(big, big, big, pl.BlockSpec((1, 1, CW), lambda g: (g, 0, 0))),
        scratch_shapes=[pltpu.VMEM((CW, CW), F32), pltpu.VMEM((CW, CW), F32)],
        compiler_params=_params(("arbitrary",)),
        name="s5_prep",
    )(row(a_re), row(a_im), row(ldt), bre, bim, cre, cim)


def _gelu_tanh(y):
    return 0.5 * y * (1.0 + jnp.tanh(math.sqrt(2.0 / math.pi) * (y + 0.044715 * (y * y * y))))


def _s5_out(u, ub, hin, qt_ref, t_ref, d_ref):
    y = (_dot(ub, t_ref[0])
         + lax.dot_general(hin.astype(BF16), qt_ref[0], _NT, preferred_element_type=F32)
         + d_ref[0] * u)
    return _gelu_tanh(y)


def _s5_scan_kernel(u_ref, p_ref, qt_ref, t_ref, a_ref, d_ref, y_ref, hout_ref,
                    h_scr, gs_scr, hin_scr):
    tc = u_ref.shape[1]

    @pl.when(pl.program_id(1) == 0)
    def _():
        h_scr[...] = jnp.zeros_like(h_scr)

    u = u_ref[0]
    ub = u.astype(BF16)
    gs_scr[...] = _dot(ub, p_ref[0])
    a = a_ref[0]
    ar, ai = a[:, 0:SG_STATE], a[:, SG_STATE:]

    def step(k, carry):
        hr, hi = carry
        hin_scr[pl.ds(k, 1), 0:SG_STATE] = hr
        hin_scr[pl.ds(k, 1), SG_STATE:] = hi
        g = gs_scr[pl.ds(k, 1), :]
        return (ar * hr - ai * hi + g[:, 0:SG_STATE], ar * hi + ai * hr + g[:, SG_STATE:])

    h0 = h_scr[...]
    hr, hi = lax.fori_loop(0, tc, step, (h0[:, 0:SG_STATE], h0[:, SG_STATE:]))
    h_new = jnp.concatenate([hr, hi], axis=-1)
    h_scr[...] = h_new
    hout_ref[0] = h_new
    y_ref[0] = _s5_out(u, ub, hin_scr[...], qt_ref, t_ref, d_ref)


def _s5_scan(u, mats, dvec, tc):
    p, qt, t, a = mats
    nch = u.shape[1]
    tc = min(tc, nch)
    tile = pl.BlockSpec((1, tc, CW), lambda g, i: (g, i, 0))
    big = pl.BlockSpec((1, CW, CW), lambda g, i: (g, 0, 0))
    vec = pl.BlockSpec((1, 1, CW), lambda g, i: (g, 0, 0))
    return pl.pallas_call(
        _s5_scan_kernel,
        out_shape=(jax.ShapeDtypeStruct(u.shape, F32), jax.ShapeDtypeStruct((SUPER, 1, CW), F32)),
        grid=(SUPER, nch // tc),
        in_specs=[tile, big, big, big, vec, vec],
        out_specs=(tile, vec),
        scratch_shapes=[pltpu.VMEM((1, CW), F32), pltpu.VMEM((tc, CW), F32),
                        pltpu.VMEM((tc, CW), F32)],
        compiler_params=_params(("arbitrary", "arbitrary")),
        name="s5_scan",
    )(u, p, qt, t, a, dvec)


def _s5_step_kernel(u_ref, h0_ref, p_ref, qt_ref, t_ref, a_ref, d_ref, y_ref, hout_ref):
    u = u_ref[0]
    ub = u.astype(BF16)
    h0 = h0_ref[0]
    hr, hi = h0[:, 0:SG_STATE], h0[:, SG_STATE:]
    a = a_ref[0]
    ar, ai = a[:, 0:SG_STATE], a[:, SG_STATE:]
    gs = _dot(ub, p_ref[0])
    hout_ref[0] = jnp.concatenate([ar * hr - ai * hi + gs[:, 0:SG_STATE],
                                   ar * hi + ai * hr + gs[:, SG_STATE:]], axis=-1)
    y_ref[0] = _s5_out(u, ub, h0, qt_ref, t_ref, d_ref)


def _s5_step(u, h0, mats, dvec):
    p, qt, t, a = mats
    b = u.shape[1]
    tile = pl.BlockSpec((1, b, CW), lambda g: (g, 0, 0))
    big = pl.BlockSpec((1, CW, CW), lambda g: (g, 0, 0))
    vec = pl.BlockSpec((1, 1, CW), lambda g: (g, 0, 0))
    shape = jax.ShapeDtypeStruct(u.shape, F32)
    return pl.pallas_call(
        _s5_step_kernel,
        out_shape=(shape, shape),
        grid=(SUPER,),
        in_specs=[tile, tile, big, big, big, vec, vec],
        out_specs=(tile, tile),
        compiler_params=_params(("arbitrary",)),
        name="s5_step",
    )(u, h0, p, qt, t, a, dvec)


def _stage_b_kernel(n_att, x_ref, ada_ref, *refs):
    att_refs = refs[:2 * n_att - 1] if n_att == 1 else refs[:2 * n_att]
    rest = refs[len(att_refs):]
    gs_ref, g2_ref, wglu_ref, wout_ref, wg_ref, wu_ref, wd_ref, y_ref = rest
    kb, r, _ = x_ref.shape
    n = kb * r
    x = x_ref[...]
    ada = ada_ref[...]
    gt2 = ada[:, 5:6]
    sh3, sc3, gt3 = ada[:, 6:7], ada[:, 7:8], ada[:, 8:9]

    if n_att == 1:
        o_att = att_refs[0][...].reshape(n, D_ATT)
    else:
        os_ = [att_refs[2 * g][...].reshape(n, D_ATT) for g in range(n_att)]
        ls = [att_refs[2 * g + 1][...].reshape(n, D_ATT) for g in range(n_att)]
        lmax = functools.reduce(jnp.maximum, ls)
        es = [jnp.exp(l - lmax) for l in ls]
        num = functools.reduce(lambda p, q: p + q, [e * o for e, o in zip(es, os_)])
        o_att = num * (1.0 / functools.reduce(lambda p, q: p + q, es))

    g = jnp.concatenate([gs_ref[sg] for sg in range(SUPER)], axis=-1)
    gl = _dot(g.astype(BF16), wglu_ref[...])
    o_ssm = gl[:, 0:D_SSM] * (1.0 / (1.0 + jnp.exp(-gl[:, D_SSM:])))
    mix = _dot(jnp.concatenate([o_att, o_ssm], axis=-1).astype(BF16), wout_ref[...])
    x2 = x + gt2 * mix.reshape(kb, r, D_MODEL)

    h = _rms(x2, g2_ref[...]) * (1.0 + sc3) + sh3
    f = _swiglu(h.reshape(n, D_MODEL).astype(BF16), wg_ref, wu_ref, wd_ref)
    y_ref[...] = x2 + 0.5 * gt3 * f.reshape(kb, r, D_MODEL)


def _stage_b(x3, ada3, atts, gs, per_seq, wts, tm):
    b3, r3, _ = x3.shape
    if per_seq:
        kb, r = min(tm // r3, b3), r3
        grid = (b3 // kb,)
        tok = lambda i: (i, 0, 0)
        ada_spec = pl.BlockSpec((kb, N_ADA, D_MODEL), lambda i: (i, 0, 0))
    else:
        kb, r = 1, min(tm, r3)
        grid = (r3 // r,)
        tok = lambda i: (0, i, 0)
        ada_spec = pl.BlockSpec((1, N_ADA, D_MODEL), lambda i: (0, 0, 0))
    n = kb * r
    n_att = (len(atts) + 1) // 2
    g2, wglu, wout, wg, wu, wd = wts
    att_spec = pl.BlockSpec((kb, r, D_ATT), tok)
    return pl.pallas_call(
        functools.partial(_stage_b_kernel, n_att),
        out_shape=jax.ShapeDtypeStruct(x3.shape, F32),
        grid=grid,
        in_specs=[pl.BlockSpec((kb, r, D_MODEL), tok), ada_spec] + [att_spec] * len(atts)
                 + [pl.BlockSpec((SUPER, n, LANES), lambda i: (0, i, 0)),
                    _const_spec(g2.shape), _const_spec(wglu.shape), _const_spec(wout.shape),
                    _const_spec(wg.shape), _const_spec(wu.shape), _const_spec(wd.shape)],
        out_specs=pl.BlockSpec((kb, r, D_MODEL), tok),
        compiler_params=_params(("arbitrary",)),
        name="stage_b",
    )(x3, ada3, *atts, gs, g2, wglu, wout, wg, wu, wd)


def _layer(x_p, x_s, c_p, c_s, cache_k, cache_v, st_re, st_im, p):
    (w_ada, b_ada, g_ffn1, w1_gate, w1_up, w1_down, g_mix, w_in, g_q, g_k,
     a_re, a_im, log_dt, b_re, b_im, c_re, c_im, d_skip, w_glu, w_out,
     g_ffn2, w2_gate, w2_up, w2_down) = p
    bp, s, _ = x_p.shape
    bs, ns, _ = x_s.shape
    assert bp == 1 and ns == CHUNK and s % (WIN_STEPS * DILATIONS[-1]) == 0
    w_buf = cache_k.shape[2]
    tm = 256

    rows = bp + bs
    pad = (-rows) % 8
    c_all = jnp.concatenate([c_p, c_s, jnp.zeros((pad, D_MODEL), F32)], axis=0)
    ada = _ada(c_all, w_ada, b_ada)
    ada_p = ada[0:bp].reshape(bp, N_ADA, D_MODEL)
    ada_s = ada[bp:rows].reshape(bs, N_ADA, D_MODEL)

    vec = lambda g: g.reshape(1, 1, -1)
    head_gain = lambda g: jnp.tile(g, N_HEADS).reshape(1, D_ATT)
    seg = jnp.asarray(np.kron(np.eye(N_HEADS), np.full((HEAD_DIM, HEAD_DIM), 1.0 / HEAD_DIM)), BF16)
    bf = lambda w: w.astype(BF16)
    wts_a = (vec(g_ffn1), vec(g_mix), head_gain(g_q), head_gain(g_k), seg,
             bf(w1_gate), bf(w1_up), bf(w1_down), bf(w_in))
    wts_b = (vec(g_ffn2), bf(w_glu), bf(w_out), bf(w2_gate), bf(w2_up), bf(w2_down))

    tab_p = tuple(t[None] for t in _rope_tables(jnp.arange(s)))
    tab_s = tuple(t[None] for t in _rope_tables(PAST_LEN + jnp.arange(ns)))

    x1_p, q_p, k_p, v_p, u_p = _stage_a(x_p, ada_p, tab_p, False, wts_a, tm)
    x1_s, q_s, k_s, v_s, u_s = _stage_a(x_s, ada_s, tab_s, True, wts_a, tm)

    atts_p = []
    for d in DILATIONS:
        atts_p.extend(_dil_attn(q_p[0], k_p[0], v_p[0], d))
    o_s, kwin_s, vwin_s = _sample_attn(q_s, k_s, v_s, cache_k, cache_v)

    mats = _s5_prep(a_re, a_im, log_dt, b_re, b_im, c_re, c_im)
    dvec = jnp.tile(d_skip.reshape(SUPER, 1, LANES), (1, 1, CHUNK))
    gs_p, hfin_p = _s5_scan(u_p.reshape(SUPER, s // CHUNK, CW), mats, dvec, 512)
    h0 = jnp.stack([jnp.concatenate([st_re[:, g * SG_STATE:(g + 1) * SG_STATE],
                                     st_im[:, g * SG_STATE:(g + 1) * SG_STATE]], axis=-1)
                    for g in range(SUPER)], axis=0)
    gs_s, hfin_s = _s5_step(u_s.reshape(SUPER, bs, CW), h0, mats, dvec)

    y_p = _stage_b(x1_p, ada_p, atts_p, gs_p.reshape(SUPER, s, LANES), False, wts_b, tm)
    y_s = _stage_b(x1_s, ada_s, [o_s], gs_s.reshape(SUPER, bs * ns, LANES), True, wts_b, tm)

    def unpack_state(h):
        b = h.shape[1]
        re = jnp.concatenate([h[g, :, 0:SG_STATE] for g in range(SUPER)], axis=-1)
        im = jnp.concatenate([h[g, :, SG_STATE:] for g in range(SUPER)], axis=-1)
        return (re.reshape(b, N_SSM_GROUPS, SSM_STATE), im.reshape(b, N_SSM_GROUPS, SSM_STATE))

    keep = min(WIN_MAX, s)
    kwin_p = k_p[:, s - keep:].reshape(bp, keep, N_HEADS, HEAD_DIM)
    vwin_p = v_p[:, s - keep:].reshape(bp, keep, N_HEADS, HEAD_DIM)
    hre_p, him_p = unpack_state(hfin_p)
    hre_s, him_s = unpack_state(hfin_s)
    unflip = lambda a: jnp.transpose(a.reshape(bs, N_HEADS, HEAD_DIM, w_buf), (0, 3, 1, 2))
    return (y_p, y_s, kwin_p, vwin_p, hre_p, him_p, unflip(kwin_s), unflip(vwin_s), hre_s, him_s)


def kernel(x_prompt, x_sample, c_prompt, c_sample, cache_k_win, cache_v_win, state_ssm_re, state_ssm_im, w_ada, b_ada, g_ffn1, w1_gate, w1_up, w1_down, g_mix, w_in, g_q, g_k, ssm_a_re, ssm_a_im, ssm_log_dt, ssm_b_re, ssm_b_im, ssm_c_re, ssm_c_im, ssm_d, w_glu, w_out, g_ffn2, w2_gate, w2_up, w2_down):
    depth = w_ada.shape[0]
    assert depth == 1
    bs = x_sample.shape[0]
    w_buf = cache_k_win.shape[2]
    p = tuple(a[0] for a in (w_ada, b_ada, g_ffn1, w1_gate, w1_up, w1_down, g_mix, w_in, g_q, g_k,
                             ssm_a_re, ssm_a_im, ssm_log_dt, ssm_b_re, ssm_b_im, ssm_c_re, ssm_c_im,
                             ssm_d, w_glu, w_out, g_ffn2, w2_gate, w2_up, w2_down))
    flip = lambda a: jnp.transpose(a[0], (0, 2, 3, 1)).reshape(bs, D_ATT, w_buf)
    outs = _layer(x_prompt, x_sample, c_prompt, c_sample, flip(cache_k_win), flip(cache_v_win),
                  state_ssm_re[0].reshape(bs, N_SSM_GROUPS * SSM_STATE),
                  state_ssm_im[0].reshape(bs, N_SSM_GROUPS * SSM_STATE), p)
    return tuple(o[None] if i >= 2 else o for i, o in enumerate(outs))
```

```python
import functools
import math

import numpy as np
import jax
import jax.numpy as jnp
from jax import lax
from jax.experimental import pallas as pl
from jax.experimental.pallas import tpu as pltpu

F32 = jnp.float32
BF16 = jnp.bfloat16

D_MODEL = 1024
D_ATT = 512
D_SSM = 512
HEAD_DIM = 64
N_HEADS = 8
ROT_DIM = 16
ROPE_THETA = 500000.0
DILATIONS = (1, 4, 16)
WIN_STEPS = 128
WIN_MAX = 2048
PAST_LEN = 8192
SSM_GROUP = 16
N_SSM_GROUPS = 32
SSM_STATE = 64
D_FF = 2816
N_ADA = 9
EPS = 1e-6

LANES = 128
CHUNK = 8
SUPER = D_SSM // LANES
SG_STATE = (LANES // SSM_GROUP) * SSM_STATE
CW = CHUNK * LANES
NEG = -1e30
VMEM_LIMIT = 56 * 1024 * 1024

_NT = (((1,), (1,)), ((), ()))


def _params(sem, vmem=VMEM_LIMIT):
    return pltpu.CompilerParams(dimension_semantics=sem, vmem_limit_bytes=vmem)


def _const_spec(shape):
    nd = len(shape)
    return pl.BlockSpec(shape, lambda *_: (0,) * nd, pipeline_mode=pl.Buffered(1))


def _dot(a, b):
    return jnp.dot(a, b, preferred_element_type=F32)


def _rms(x, g):
    ms = jnp.mean(x * x, axis=-1, keepdims=True)
    return x * lax.rsqrt(ms + EPS) * g


def _swiglu(h, wg_ref, wu_ref, wd_ref):
    a = _dot(h, wg_ref[...])
    b = _dot(h, wu_ref[...])
    t = (a * (1.0 / (1.0 + jnp.exp(-a))) * b).astype(BF16)
    return _dot(t, wd_ref[...])


def _ada_kernel(c_ref, w_ref, b_ref, o_ref):
    c = c_ref[...]
    s = (c * (1.0 / (1.0 + jnp.exp(-c)))).astype(BF16)
    o_ref[...] = _dot(s, w_ref[...].astype(BF16)) + b_ref[...]


def _ada(c, w_ada, b_ada):
    m = c.shape[0]
    n = w_ada.shape[1]
    tn = n // N_ADA
    return pl.pallas_call(
        _ada_kernel,
        out_shape=jax.ShapeDtypeStruct((m, n), F32),
        grid=(n // tn,),
        in_specs=[pl.BlockSpec((m, D_MODEL), lambda j: (0, 0)),
                  pl.BlockSpec((D_MODEL, tn), lambda j: (0, j)),
                  pl.BlockSpec((1, tn), lambda j: (0, j))],
        out_specs=pl.BlockSpec((m, tn), lambda j: (0, j)),
        compiler_params=_params(("arbitrary",)),
        name="ada",
    )(c, w_ada, b_ada.reshape(1, n))


def _stage_a_kernel(pair_major, x_ref, ada_ref, cos_ref, sina_ref, sinb_ref, gf_ref, gm_ref, gq_ref,
                    gk_ref, seg_ref, wg_ref, wu_ref, wd_ref, win_ref,
                    x1_ref, q_ref, k_ref, v_ref, u_ref, u_scr):
    kb, r, _ = x_ref.shape
    n = kb * r
    x = x_ref[...]
    ada = ada_ref[...]
    sh1, sc1, gt1 = ada[:, 0:1], ada[:, 1:2], ada[:, 2:3]
    sh2, sc2 = ada[:, 3:4], ada[:, 4:5]

    h = _rms(x, gf_ref[...]) * (1.0 + sc1) + sh1
    f = _swiglu(h.reshape(n, D_MODEL).astype(BF16), wg_ref, wu_ref, wd_ref)
    x1 = x + 0.5 * gt1 * f.reshape(kb, r, D_MODEL)
    x1_ref[...] = x1

    h = _rms(x1, gm_ref[...]) * (1.0 + sc2) + sh2
    proj = _dot(h.reshape(n, D_MODEL).astype(BF16), win_ref[...])

    cos = jnp.concatenate([cos_ref[...]] * (D_ATT // LANES), axis=-1)
    sina = jnp.concatenate([sina_ref[...]] * (D_ATT // LANES), axis=-1)
    sinb = jnp.concatenate([sinb_ref[...]] * (D_ATT // LANES), axis=-1)

    def head_norm_rope(z, g):
        ms = _dot((z * z).astype(BF16), seg_ref[...])
        zn = z * lax.rsqrt(ms + EPS) * g
        up = pltpu.roll(zn, D_ATT - ROT_DIM // 2, 1).reshape(kb, r, D_ATT)
        dn = pltpu.roll(zn, ROT_DIM // 2, 1).reshape(kb, r, D_ATT)
        return zn.reshape(kb, r, D_ATT) * cos + up * sina + dn * sinb

    q = head_norm_rope(proj[:, 0:D_ATT], gq_ref[...])
    k = head_norm_rope(proj[:, D_ATT:2 * D_ATT], gk_ref[...])
    v = proj[:, 2 * D_ATT:3 * D_ATT]
    if pair_major:
        for z, z_ref in ((q.reshape(n, D_ATT), q_ref), (k.reshape(n, D_ATT), k_ref), (v, v_ref)):
            for hp in range(D_ATT // LANES):
                z_ref[hp] = z[:, hp * LANES:(hp + 1) * LANES]
    else:
        q_ref[...] = q
        k_ref[...] = k
        v_ref[...] = v.reshape(kb, r, D_ATT)
    u = proj[:, 3 * D_ATT:]
    for sg in range(SUPER):
        u_scr[sg] = u[:, sg * LANES:(sg + 1) * LANES]
    for sg in range(SUPER):
        for s in range(CHUNK):
            u_ref[sg, :, s * LANES:(s + 1) * LANES] = u_scr[sg, pl.ds(s, n // CHUNK, stride=CHUNK), :]


def _rope_tables(pos):
    half = ROT_DIM // 2
    j = np.arange(LANES) % HEAD_DIM
    lo = jnp.asarray(j < half)
    hi = jnp.asarray((j >= half) & (j < ROT_DIM))
    inv = ROPE_THETA ** (-jnp.asarray(j % half, F32) / half)
    ang = pos.astype(F32)[:, None] * inv[None, :]
    c = jnp.cos(ang)
    s = jnp.sin(ang)
    cos = jnp.where(lo | hi, c, 1.0)
    sina = jnp.where(lo, -s, 0.0)
    sinb = jnp.where(hi, s, 0.0)
    return cos, sina, sinb


def _stage_a(x3, ada3, tables, per_seq, wts, tm):
    b3, r3, _ = x3.shape
    if per_seq:
        kb, r = min(tm // r3, b3), r3
        grid = (b3 // kb,)
        tok = lambda i: (i, 0, 0)
        tab = lambda i: (0, 0, 0)
        ada_spec = pl.BlockSpec((kb, N_ADA, D_MODEL), lambda i: (i, 0, 0))
    else:
        kb, r = 1, min(tm, r3)
        grid = (r3 // r,)
        tok = lambda i: (0, i, 0)
        tab = lambda i: (0, i, 0)
        ada_spec = pl.BlockSpec((1, N_ADA, D_MODEL), lambda i: (0, 0, 0))
    n = kb * r
    ntok = b3 * r3
    cos, sina, sinb = tables
    gf, gm, gq, gk, seg, wg, wu, wd, win = wts
    tab_spec = pl.BlockSpec((1, r, LANES), tab)
    if per_seq:
        att_spec = pl.BlockSpec((kb, r, D_ATT), tok)
        att_shape = jax.ShapeDtypeStruct((b3, r3, D_ATT), F32)
    else:
        att_spec = pl.BlockSpec((D_ATT // LANES, n, LANES), lambda i: (0, i, 0))
        att_shape = jax.ShapeDtypeStruct((D_ATT // LANES, ntok, LANES), F32)
    return pl.pallas_call(
        functools.partial(_stage_a_kernel, not per_seq),
        out_shape=(jax.ShapeDtypeStruct(x3.shape, F32), att_shape, att_shape, att_shape,
                   jax.ShapeDtypeStruct((SUPER, ntok // CHUNK, CW), F32)),
        grid=grid,
        in_specs=[pl.BlockSpec((kb, r, D_MODEL), tok), ada_spec, tab_spec, tab_spec, tab_spec,
                  _const_spec(gf.shape), _const_spec(gm.shape), _const_spec(gq.shape),
                  _const_spec(gk.shape), _const_spec(seg.shape), _const_spec(wg.shape),
                  _const_spec(wu.shape), _const_spec(wd.shape), _const_spec(win.shape)],
        out_specs=(pl.BlockSpec((kb, r, D_MODEL), tok), att_spec, att_spec, att_spec,
                   pl.BlockSpec((SUPER, n // CHUNK, CW), lambda i: (0, i, 0))),
        scratch_shapes=[pltpu.VMEM((SUPER, n, LANES), F32)],
        compiler_params=_params(("arbitrary",)),
        name="stage_a",
    )(x3, ada3, cos, sina, sinb, gf, gm, gq, gk, seg, wg, wu, wd, win)


UNITS_PER_ITER = 4


def _rows(start, size, stride):
    return pl.ds(start, size) if stride == 1 else pl.ds(start, size, stride=stride)


def _dil_attn_kernel(q_ref, k_ref, v_ref, o_ref, kbuf, vbuf, acc, mrow, lrow):
    i = pl.program_id(1)
    sb = q_ref.shape[1]
    tq = WIN_STEPS

    @pl.when(i == 0)
    def _():
        kbuf[0:sb, :] = jnp.zeros((sb, LANES), F32)
        vbuf[0:sb, :] = jnp.zeros((sb, LANES), F32)

    @pl.when(i > 0)
    def _():
        kbuf[0:sb, :] = kbuf[sb:2 * sb, :]
        vbuf[0:sb, :] = vbuf[sb:2 * sb, :]

    kbuf[sb:2 * sb, :] = k_ref[0]
    vbuf[sb:2 * sb, :] = v_ref[0]

    row = lax.broadcasted_iota(jnp.int32, (tq, 2 * tq), 0)
    col = lax.broadcasted_iota(jnp.int32, (tq, 2 * tq), 1)
    band = jnp.where(col >= row, 0.0, NEG)
    band = jnp.where(col <= row + tq, band, NEG)
    band0 = jnp.where(col >= tq, band, NEG)
    band_first = jnp.where(i == 0, band0, band)
    low = lax.broadcasted_iota(jnp.int32, (tq, LANES), 1) < HEAD_DIM
    high = jnp.logical_not(low)

    def unit(d, qs, bias, mode):
        qp = (q_ref[0, _rows(qs, tq, d), :] * (HEAD_DIM ** -0.5)).astype(BF16)
        kp = kbuf[_rows(sb + qs - d * tq, 2 * tq, d), :].astype(BF16)
        vp = vbuf[_rows(sb + qs - d * tq, 2 * tq, d), :].astype(BF16)
        parts = []
        for sel in (low, high):
            qe = jnp.where(sel, qp, jnp.zeros_like(qp))
            s = lax.dot_general(qe, kp, _NT, preferred_element_type=F32) + bias
            mx = jnp.max(s, axis=1, keepdims=True)
            p = jnp.exp(s - mx)
            parts.append((_dot(p.astype(BF16), vp), mx, jnp.sum(p, axis=1, keepdims=True)))
        o = jnp.where(low, parts[0][0], parts[1][0])
        mx = jnp.where(low, parts[0][1], parts[1][1])
        l = jnp.where(low, parts[0][2], parts[1][2])
        rows = _rows(qs, tq, d)
        if mode != "init":
            m_old = mrow[rows, :]
            m_new = jnp.maximum(m_old, mx)
            a_old = jnp.exp(m_old - m_new)
            a_new = jnp.exp(mx - m_new)
            o = acc[rows, :] * a_old + o * a_new
            l = lrow[rows, :] * a_old + l * a_new
            mx = m_new
        if mode == "final":
            o_ref[0, rows, :] = o * (1.0 / l)
        else:
            acc[rows, :] = o
            mrow[rows, :] = mx
            lrow[rows, :] = l

    upi = UNITS_PER_ITER
    d16, d4 = DILATIONS[2], DILATIONS[1]

    def body16(g, c):
        for u in range(upi):
            unit(d16, g * upi + u, band_first, "init")
        return c

    lax.fori_loop(0, d16 // upi, body16, 0)

    def body4(r, c):
        for mb in range(sb // d4 // tq):
            unit(d4, r + d4 * tq * mb, band_first if mb == 0 else band, "merge")
        return c

    lax.fori_loop(0, d4, body4, 0)

    def body1(g, c):
        for u in range(upi):
            bias = jnp.where(jnp.logical_and(i == 0, g == 0), band0, band) if u == 0 else band
            unit(1, (g * upi + u) * tq, bias, "final")
        return c

    lax.fori_loop(0, sb // tq // upi, body1, 0)


def _dil_attn(q, k, v):
    npair, s, _ = q.shape
    sb = WIN_STEPS * DILATIONS[-1]
    assert s % sb == 0 and DILATIONS[0] == 1
    blk = pl.BlockSpec((1, sb, LANES), lambda hp, i: (hp, i, 0))
    return pl.pallas_call(
        _dil_attn_kernel,
        out_shape=jax.ShapeDtypeStruct(q.shape, F32),
        grid=(npair, s // sb),
        in_specs=[blk, blk, blk],
        out_specs=blk,
        scratch_shapes=[pltpu.VMEM((2 * sb, LANES), F32), pltpu.VMEM((2 * sb, LANES), F32),
                        pltpu.VMEM((sb, LANES), F32), pltpu.VMEM((sb, LANES), F32),
                        pltpu.VMEM((sb, LANES), F32)],
        compiler_params=_params(("arbitrary", "arbitrary")),
        name="dil_attn",
    )(q, k, v)


def _sample_bias():
    w = WIN_MAX
    out = []
    i = (np.arange(N_HEADS * CHUNK) % CHUNK)[:, None]
    for d in DILATIONS:
        span = WIN_STEPS * d
        c = np.arange(w - span, w)[None, :]
        dist = w + i - c
        ok_buf = (dist % d == 0) & (dist <= span)
        cn = np.arange(LANES)[None, :] - (LANES - CHUNK)
        dn = i - cn
        ok_new = (cn >= 0) & (dn >= 0) & (dn % d == 0) & (dn <= span)
        ok = np.concatenate([ok_buf, ok_new], axis=1)
        out.append(jnp.asarray(np.where(ok, 0.0, NEG), F32))
    return out


def _sample_attn_kernel(q_ref, kn_ref, vn_ref, kc_ref, vc_ref, b1_ref, b2_ref, b3_ref,
                        o_ref, ko_ref, vo_ref):
    w = kc_ref.shape[2]
    nq = q_ref.shape[1]
    kc = kc_ref[0]
    vc = vc_ref[0]
    lane = lax.broadcasted_iota(jnp.int32, (D_ATT, LANES), 1)
    pad = jnp.zeros((LANES - nq, D_ATT), F32)

    def shifted(buf, new, out_ref):
        newt = jnp.concatenate([pad, new], axis=0).T
        rolled = pltpu.roll(buf, w - nq, 1)
        out_ref[0, :, 0:w - LANES] = rolled[:, 0:w - LANES]
        out_ref[0, :, w - LANES:] = jnp.where(lane >= LANES - nq, newt, rolled[:, w - LANES:])
        return newt

    knb = shifted(kc, kn_ref[0], ko_ref).astype(BF16)
    vnb = shifted(vc, vn_ref[0], vo_ref).astype(BF16)

    rows = N_HEADS * nq
    q = q_ref[0] * (HEAD_DIM ** -0.5)
    qt = jnp.concatenate([q] * N_HEADS, axis=0)
    rhead = lax.broadcasted_iota(jnp.int32, (rows, D_ATT), 0) // nq
    lhead = lax.broadcasted_iota(jnp.int32, (rows, D_ATT), 1) // HEAD_DIM
    qe = jnp.where(rhead == lhead, qt, 0.0).astype(BF16)
    kcb = kc.astype(BF16)
    vcb = vc.astype(BF16)
    s_buf = _dot(qe, kcb)
    s_new = _dot(qe, knb)

    outs, lses = [], []
    for d, b_ref in zip(DILATIONS, (b1_ref, b2_ref, b3_ref)):
        span = WIN_STEPS * d
        s = jnp.concatenate([s_buf[:, w - span:], s_new], axis=1) + b_ref[...]
        mx = jnp.max(s, axis=1, keepdims=True)
        p = jnp.exp(s - mx)
        l = jnp.sum(p, axis=1, keepdims=True)
        vcat = jnp.concatenate([vcb[:, w - span:], vnb], axis=1)
        o = lax.dot_general(p.astype(BF16), vcat, _NT, preferred_element_type=F32)
        outs.append(o * (1.0 / l))
        lses.append(mx + jnp.log(l))
    lmax = jnp.maximum(jnp.maximum(lses[0], lses[1]), lses[2])
    es = [jnp.exp(l - lmax) for l in lses]
    o = (es[0] * outs[0] + es[1] * outs[1] + es[2] * outs[2]) * (1.0 / (es[0] + es[1] + es[2]))
    o = jnp.where(rhead == lhead, o, 0.0)
    acc = o[0:nq]
    for h in range(1, N_HEADS):
        acc = acc + o[h * nq:(h + 1) * nq]
    o_ref[0] = acc


def _sample_attn(q, kn, vn, kc, vc):
    b, nq, _ = q.shape
    w = kc.shape[2]
    assert w == WIN_MAX and nq == CHUNK
    b1, b2, b3 = _sample_bias()
    new = pl.BlockSpec((1, nq, D_ATT), lambda i: (i, 0, 0))
    buf = pl.BlockSpec((1, D_ATT, w), lambda i: (i, 0, 0))
    return pl.pallas_call(
        _sample_attn_kernel,
        out_shape=(jax.ShapeDtypeStruct(q.shape, F32), jax.ShapeDtypeStruct(kc.shape, F32),
                   jax.ShapeDtypeStruct(vc.shape, F32)),
        grid=(b,),
        in_specs=[new, new, new, buf, buf, _const_spec(b1.shape), _const_spec(b2.shape),
                  _const_spec(b3.shape)],
        out_specs=(new, buf, buf),
        compiler_params=_params(("arbitrary",)),
        name="sample_attn",
    )(q, kn, vn, kc, vc, b1, b2, b3)


def _split(a):
    hi = a.astype(BF16)
    return hi, (a - hi.astype(F32)).astype(BF16)


def _s5_prep_kernel(ar_ref, ai_ref, ldt_ref, bre_ref, bim_ref, cre_ref, cim_ref,
                    p_ref, qt_ref, t_ref, a_ref, pf_scr, q2_scr):
    ar, ai = ar_ref[0], ai_ref[0]
    dt = jnp.exp(ldt_ref[0])
    mag = jnp.exp(ar * dt)
    lr, li = mag * jnp.cos(ai * dt), mag * jnp.sin(ai * dt)
    den = 1.0 / (ar * ar + ai * ai)
    kr = ((lr - 1.0) * ar + li * ai) * den
    ki = (li * ar - (lr - 1.0) * ai) * den
    bre, bim = bre_ref[0], bim_ref[0]
    bbr, bbi = kr * bre - ki * bim, kr * bim + ki * bre
    cre, cim = cre_ref[0], cim_ref[0]
    pr, pi = [jnp.ones_like(lr)], [jnp.zeros_like(lr)]
    for _ in range(CHUNK):
        pr.append(pr[-1] * lr - pi[-1] * li)
        pi.append(pr[-2] * li + pi[-1] * lr)
    dd = 1.0 / (pr[CHUNK] * pr[CHUNK] + pi[CHUNK] * pi[CHUNK])
    dr, di = pr[CHUNK] * dd, -pi[CHUNK] * dd
    for s in range(CHUNK):
        n = CHUNK - 1 - s
        rows = slice(s * LANES, (s + 1) * LANES)
        pf_scr[rows, 0:SG_STATE] = bbr * pr[n] - bbi * pi[n]
        pf_scr[rows, SG_STATE:] = bbr * pi[n] + bbi * pr[n]
        n = s + 1
        re, im = cre * pr[n] - cim * pi[n], cre * pi[n] + cim * pr[n]
        qt_ref[0, rows, 0:SG_STATE] = re.astype(BF16)
        qt_ref[0, rows, SG_STATE:] = (-im).astype(BF16)
        q2_scr[rows, 0:SG_STATE] = dr * re - di * im
        q2_scr[rows, SG_STATE:] = -(dr * im + di * re)
    pf = pf_scr[...]
    p_ref[0] = pf.astype(BF16)
    a_ref[0] = jnp.concatenate([pr[CHUNK], pi[CHUNK]], axis=-1)
    ph, plo = _split(pf)
    qh, qlo = _split(q2_scr[...])
    t = (lax.dot_general(ph, qh, _NT, preferred_element_type=F32)
         + lax.dot_general(ph, qlo, _NT, preferred_element_type=F32)
         + lax.dot_general(plo, qh, _NT, preferred_element_type=F32))
    rb = lax.broadcasted_iota(jnp.int32, (CW, CW), 0) // LANES
    cb = lax.broadcasted_iota(jnp.int32, (CW, CW), 1) // LANES
    t_ref[0] = jnp.where(rb <= cb, t, 0.0).astype(BF16)


def _block_diag(a):
    g = LANES // SSM_GROUP
    _, r, c = a.shape
    a = a.reshape(SUPER, g, r, c)
    eye = jnp.eye(g, dtype=a.dtype)
    return jnp.einsum("sgrc,gh->sgrhc", a, eye).reshape(SUPER, g * r, g * c)


def _s5_prep(a_re, a_im, log_dt, b_re, b_im, c_re, c_im):
    row = lambda a: a.reshape(SUPER, 1, SG_STATE)
    ldt = jnp.broadcast_to(log_dt[:, None], (N_SSM_GROUPS, SSM_STATE))
    bre = _block_diag(jnp.swapaxes(b_re, 1, 2))
    bim = _block_diag(jnp.swapaxes(b_im, 1, 2))
    cre = _block_diag(c_re)
    cim = _block_diag(c_im)
    vec = pl.BlockSpec((1, 1, SG_STATE), lambda g: (g, 0, 0))
    mat = pl.BlockSpec((1, LANES, SG_STATE), lambda g: (g, 0, 0))
    big = pl.BlockSpec((1, CW, CW), lambda g: (g, 0, 0))
    big_shape = jax.ShapeDtypeStruct((SUPER, CW, CW), BF16)
    return pl.pallas_call(
        _s5_prep_kernel,
        out_shape=(big_shape, big_shape, big_shape, jax.ShapeDtypeStruct((SUPER, 1, CW), F32)),
        grid=(SUPER,),
        in_specs=[vec, vec, vec, mat, mat, mat, mat],
        out_specs=(big, big, big, pl.BlockSpec((1, 1, CW), lambda g: (g, 0, 0))),
        scratch_shapes=[pltpu.VMEM((CW, CW), F32), pltpu.VMEM((CW, CW), F32)],
        compiler_params=_params(("arbitrary",)),
        name="s5_prep",
    )(row(a_re), row(a_im), row(ldt), bre, bim, cre, cim)


def _gelu_tanh(y):
    return 0.5 * y * (1.0 + jnp.tanh(math.sqrt(2.0 / math.pi) * (y + 0.044715 * (y * y * y))))


def _s5_out(u, ub, hin, qt_ref, t_ref, d_ref):
    y = (_dot(ub, t_ref[0])
         + lax.dot_general(hin.astype(BF16), qt_ref[0], _NT, preferred_element_type=F32)
         + d_ref[0] * u)
    return _gelu_tanh(y)


def _s5_scan_kernel(u_ref, p_ref, qt_ref, t_ref, a_ref, d_ref, y_ref, hout_ref,
                    h_scr, gs_scr, hin_scr):
    tc = u_ref.shape[1]

    @pl.when(pl.program_id(1) == 0)
    def _():
        h_scr[...] = jnp.zeros_like(h_scr)

    u = u_ref[0]
    ub = u.astype(BF16)
    gs_scr[...] = _dot(ub, p_ref[0])
    a = a_ref[0]
    ar, ai = a[:, 0:SG_STATE], a[:, SG_STATE:]

    def step(k, carry):
        hr, hi = carry
        hin_scr[pl.ds(k, 1), 0:SG_STATE] = hr
        hin_scr[pl.ds(k, 1), SG_STATE:] = hi
        g = gs_scr[pl.ds(k, 1), :]
        return (ar * hr - ai * hi + g[:, 0:SG_STATE], ar * hi + ai * hr + g[:, SG_STATE:])

    h0 = h_scr[...]
    hr, hi = lax.fori_loop(0, tc, step, (h0[:, 0:SG_STATE], h0[:, SG_STATE:]))
    h_new = jnp.concatenate([hr, hi], axis=-1)
    h_scr[...] = h_new
    hout_ref[0] = h_new
    y_ref[0] = _s5_out(u, ub, hin_scr[...], qt_ref, t_ref, d_ref)


def _s5_scan(u, mats, dvec, tc):
    p, qt, t, a = mats
    nch = u.shape[1]
    tc = min(tc, nch)
    tile = pl.BlockSpec((1, tc, CW), lambda g, i: (g, i, 0))
    big = pl.BlockSpec((1, CW, CW), lambda g, i: (g, 0, 0))
    vec = pl.BlockSpec((1, 1, CW), lambda g, i: (g, 0, 0))
    return pl.pallas_call(
        _s5_scan_kernel,
        out_shape=(jax.ShapeDtypeStruct(u.shape, F32), jax.ShapeDtypeStruct((SUPER, 1, CW), F32)),
        grid=(SUPER, nch // tc),
        in_specs=[tile, big, big, big, vec, vec],
        out_specs=(tile, vec),
        scratch_shapes=[pltpu.VMEM((1, CW), F32), pltpu.VMEM((tc, CW), F32),
                        pltpu.VMEM((tc, CW), F32)],
        compiler_params=_params(("arbitrary", "arbitrary")),
        name="s5_scan",
    )(u, p, qt, t, a, dvec)


def _s5_step_kernel(u_ref, h0_ref, p_ref, qt_ref, t_ref, a_ref, d_ref, y_ref, hout_ref):
    u = u_ref[0]
    ub = u.astype(BF16)
    h0 = h0_ref[0]
    hr, hi = h0[:, 0:SG_STATE], h0[:, SG_STATE:]
    a = a_ref[0]
    ar, ai = a[:, 0:SG_STATE], a[:, SG_STATE:]
    gs = _dot(ub, p_ref[0])
    hout_ref[0] = jnp.concatenate([ar * hr - ai * hi + gs[:, 0:SG_STATE],
                                   ar * hi + ai * hr + gs[:, SG_STATE:]], axis=-1)
    y_ref[0] = _s5_out(u, ub, h0, qt_ref, t_ref, d_ref)


def _s5_step(u, h0, mats, dvec):
    p, qt, t, a = mats
    b = u.shape[1]
    tile = pl.BlockSpec((1, b, CW), lambda g: (g, 0, 0))
    big = pl.BlockSpec((1, CW, CW), lambda g: (g, 0, 0))
    vec = pl.BlockSpec((1, 1, CW), lambda g: (g, 0, 0))
    shape = jax.ShapeDtypeStruct(u.shape, F32)
    return pl.pallas_call(
        _s5_step_kernel,
        out_shape=(shape, shape),
        grid=(SUPER,),
        in_specs=[tile, tile, big, big, big, vec, vec],
        out_specs=(tile, tile),
        compiler_params=_params(("arbitrary",)),
        name="s5_step",
    )(u, h0, p, qt, t, a, dvec)


def _stage_b_kernel(pair_major, x_ref, ada_ref, att_ref, gs_ref, g2_ref, wglu_ref, wout_ref,
                    wg_ref, wu_ref, wd_ref, y_ref, g_scr):
    kb, r, _ = x_ref.shape
    n = kb * r
    x = x_ref[...]
    ada = ada_ref[...]
    gt2 = ada[:, 5:6]
    sh3, sc3, gt3 = ada[:, 6:7], ada[:, 7:8], ada[:, 8:9]

    if pair_major:
        o_att = jnp.concatenate([att_ref[hp] for hp in range(D_ATT // LANES)], axis=-1)
    else:
        o_att = att_ref[...].reshape(n, D_ATT)

    for sg in range(SUPER):
        for s in range(CHUNK):
            g_scr[sg, pl.ds(s, n // CHUNK, stride=CHUNK), :] = gs_ref[sg, :, s * LANES:(s + 1) * LANES]
    g = jnp.concatenate([g_scr[sg] for sg in range(SUPER)], axis=-1)
    gl = _dot(g.astype(BF16), wglu_ref[...])
    o_ssm = gl[:, 0:D_SSM] * (1.0 / (1.0 + jnp.exp(-gl[:, D_SSM:])))
    mix = _dot(jnp.concatenate([o_att, o_ssm], axis=-1).astype(BF16), wout_ref[...])
    x2 = x + gt2 * mix.reshape(kb, r, D_MODEL)

    h = _rms(x2, g2_ref[...]) * (1.0 + sc3) + sh3
    f = _swiglu(h.reshape(n, D_MODEL).astype(BF16), wg_ref, wu_ref, wd_ref)
    y_ref[...] = x2 + 0.5 * gt3 * f.reshape(kb, r, D_MODEL)


def _stage_b(x3, ada3, att, gs, per_seq, wts, tm):
    b3, r3, _ = x3.shape
    if per_seq:
        kb, r = min(tm // r3, b3), r3
        grid = (b3 // kb,)
        tok = lambda i: (i, 0, 0)
        ada_spec = pl.BlockSpec((kb, N_ADA, D_MODEL), lambda i: (i, 0, 0))
    else:
        kb, r = 1, min(tm, r3)
        grid = (r3 // r,)
        tok = lambda i: (0, i, 0)
        ada_spec = pl.BlockSpec((1, N_ADA, D_MODEL), lambda i: (0, 0, 0))
    n = kb * r
    g2, wglu, wout, wg, wu, wd = wts
    if per_seq:
        att_spec = pl.BlockSpec((kb, r, D_ATT), tok)
    else:
        att_spec = pl.BlockSpec((D_ATT // LANES, n, LANES), lambda i: (0, i, 0))
    return pl.pallas_call(
        functools.partial(_stage_b_kernel, not per_seq),
        out_shape=jax.ShapeDtypeStruct(x3.shape, F32),
        grid=grid,
        in_specs=[pl.BlockSpec((kb, r, D_MODEL), tok), ada_spec, att_spec,
                  pl.BlockSpec((SUPER, n // CHUNK, CW), lambda i: (0, i, 0)),
                  _const_spec(g2.shape), _const_spec(wglu.shape), _const_spec(wout.shape),
                  _const_spec(wg.shape), _const_spec(wu.shape), _const_spec(wd.shape)],
        out_specs=pl.BlockSpec((kb, r, D_MODEL), tok),
        scratch_shapes=[pltpu.VMEM((SUPER, n, LANES), F32)],
        compiler_params=_params(("arbitrary",)),
        name="stage_b",
    )(x3, ada3, att, gs, g2, wglu, wout, wg, wu, wd)


def _layer(x_p, x_s, c_p, c_s, cache_k, cache_v, st_re, st_im, p):
    (w_ada, b_ada, g_ffn1, w1_gate, w1_up, w1_down, g_mix, w_in, g_q, g_k,
     a_re, a_im, log_dt, b_re, b_im, c_re, c_im, d_skip, w_glu, w_out,
     g_ffn2, w2_gate, w2_up, w2_down) = p
    bp, s, _ = x_p.shape
    bs, ns, _ = x_s.shape
    assert bp == 1 and ns == CHUNK and s % (WIN_STEPS * DILATIONS[-1]) == 0
    w_buf = cache_k.shape[2]
    tm = 256

    rows = bp + bs
    pad = (-rows) % 8
    c_all = jnp.concatenate([c_p, c_s, jnp.zeros((pad, D_MODEL), F32)], axis=0)
    ada = _ada(c_all, w_ada, b_ada)
    ada_p = ada[0:bp].reshape(bp, N_ADA, D_MODEL)
    ada_s = ada[bp:rows].reshape(bs, N_ADA, D_MODEL)

    vec = lambda g: g.reshape(1, 1, -1)
    head_gain = lambda g: jnp.tile(g, N_HEADS).reshape(1, D_ATT)
    seg = jnp.asarray(np.kron(np.eye(N_HEADS), np.full((HEAD_DIM, HEAD_DIM), 1.0 / HEAD_DIM)), BF16)
    bf = lambda w: w.astype(BF16)
    wts_a = (vec(g_ffn1), vec(g_mix), head_gain(g_q), head_gain(g_k), seg,
             bf(w1_gate), bf(w1_up), bf(w1_down), bf(w_in))
    wts_b = (vec(g_ffn2), bf(w_glu), bf(w_out), bf(w2_gate), bf(w2_up), bf(w2_down))

    tab_p = tuple(t[None] for t in _rope_tables(jnp.arange(s)))
    tab_s = tuple(t[None] for t in _rope_tables(PAST_LEN + jnp.arange(ns)))

    x1_p, q_p, k_p, v_p, u_p = _stage_a(x_p, ada_p, tab_p, False, wts_a, tm)
    x1_s, q_s, k_s, v_s, u_s = _stage_a(x_s, ada_s, tab_s, True, wts_a, tm)

    o_p = _dil_attn(q_p, k_p, v_p)
    o_s, kwin_s, vwin_s = _sample_attn(q_s, k_s, v_s, cache_k, cache_v)

    mats = _s5_prep(a_re, a_im, log_dt, b_re, b_im, c_re, c_im)
    dvec = jnp.tile(d_skip.reshape(SUPER, 1, LANES), (1, 1, CHUNK))
    gs_p, hfin_p = _s5_scan(u_p, mats, dvec, 512)
    h0 = jnp.stack([jnp.concatenate([st_re[:, g * SG_STATE:(g + 1) * SG_STATE],
                                     st_im[:, g * SG_STATE:(g + 1) * SG_STATE]], axis=-1)
                    for g in range(SUPER)], axis=0)
    gs_s, hfin_s = _s5_step(u_s, h0, mats, dvec)

    y_p = _stage_b(x1_p, ada_p, o_p, gs_p, False, wts_b, tm)
    y_s = _stage_b(x1_s, ada_s, o_s, gs_s, True, wts_b, tm)

    def unpack_state(h):
        b = h.shape[1]
        re = jnp.concatenate([h[g, :, 0:SG_STATE] for g in range(SUPER)], axis=-1)
        im = jnp.concatenate([h[g, :, SG_STATE:] for g in range(SUPER)], axis=-1)
        return (re.reshape(b, N_SSM_GROUPS, SSM_STATE), im.reshape(b, N_SSM_GROUPS, SSM_STATE))

    keep = min(WIN_MAX, s)
    tail = lambda a: jnp.transpose(a[:, s - keep:], (1, 0, 2)).reshape(bp, keep, N_HEADS, HEAD_DIM)
    kwin_p, vwin_p = tail(k_p), tail(v_p)
    hre_p, him_p = unpack_state(hfin_p)
    hre_s, him_s = unpack_state(hfin_s)
    unflip = lambda a: jnp.transpose(a.reshape(bs, N_HEADS, HEAD_DIM, w_buf), (0, 3, 1, 2))
    return (y_p, y_s, kwin_p, vwin_p, hre_p, him_p, unflip(kwin_s), unflip(vwin_s), hre_s, him_s)


def kernel(x_prompt, x_sample, c_prompt, c_sample, cache_k_win, cache_v_win, state_ssm_re, state_ssm_im, w_ada, b_ada, g_ffn1, w1_gate, w1_up, w1_down, g_mix, w_in, g_q, g_k, ssm_a_re, ssm_a_im, ssm_log_dt, ssm_b_re, ssm_b_im, ssm_c_re, ssm_c_im, ssm_d, w_glu, w_out, g_ffn2, w2_gate, w2_up, w2_down):
    depth = w_ada.shape[0]
    assert depth == 1
    bs = x_sample.shape[0]
    w_buf = cache_k_win.shape[2]
    p = tuple(a[0] for a in (w_ada, b_ada, g_ffn1, w1_gate, w1_up, w1_down, g_mix, w_in, g_q, g_k,
                             ssm_a_re, ssm_a_im, ssm_log_dt, ssm_b_re, ssm_b_im, ssm_c_re, ssm_c_im,
                             ssm_d, w_glu, w_out, g_ffn2, w2_gate, w2_up, w2_down))
    flip = lambda a: jnp.transpose(a[0], (0, 2, 3, 1)).reshape(bs, D_ATT, w_buf)
    outs = _layer(x_prompt, x_sample, c_prompt, c_sample, flip(cache_k_win), flip(cache_v_win),
                  state_ssm_re[0].reshape(bs, N_SSM_GROUPS * SSM_STATE),
                  state_ssm_im[0].reshape(bs, N_SSM_GROUPS * SSM_STATE), p)
    return tuple(o[None] if i >= 2 else o for i, o in enumerate(outs))
```

```python
import functools
import math

import numpy as np
import jax
import jax.numpy as jnp
from jax import lax
from jax.experimental import pallas as pl
from jax.experimental.pallas import tpu as pltpu

F32 = jnp.float32
BF16 = jnp.bfloat16

D_MODEL = 1024
D_ATT = 512
D_SSM = 512
HEAD_DIM = 64
N_HEADS = 8
ROT_DIM = 16
ROPE_THETA = 500000.0
DILATIONS = (1, 4, 16)
WIN_STEPS = 128
WIN_MAX = 2048
PAST_LEN = 8192
SSM_GROUP = 16
N_SSM_GROUPS = 32
SSM_STATE = 64
D_FF = 2816
N_ADA = 9
EPS = 1e-6

LANES = 128
CHUNK = 8
SUPER = D_SSM // LANES
SG_STATE = (LANES // SSM_GROUP) * SSM_STATE
CW = CHUNK * LANES
NEG = -1e30
VMEM_LIMIT = 56 * 1024 * 1024

_NT = (((1,), (1,)), ((), ()))


def _params(sem, vmem=VMEM_LIMIT):
    return pltpu.CompilerParams(dimension_semantics=sem, vmem_limit_bytes=vmem)


def _const_spec(shape):
    nd = len(shape)
    return pl.BlockSpec(shape, lambda *_: (0,) * nd, pipeline_mode=pl.Buffered(1))


def _dot(a, b):
    return jnp.dot(a, b, preferred_element_type=F32)


def _rms(x, g):
    ms = jnp.mean(x * x, axis=-1, keepdims=True)
    return x * lax.rsqrt(ms + EPS) * g


def _swiglu(h, wg_ref, wu_ref, wd_ref):
    a = _dot(h, wg_ref[...])
    b = _dot(h, wu_ref[...])
    t = (a * (1.0 / (1.0 + jnp.exp(-a))) * b).astype(BF16)
    return _dot(t, wd_ref[...])


def _ada_kernel(c_ref, w_ref, b_ref, o_ref):
    c = c_ref[...]
    s = (c * (1.0 / (1.0 + jnp.exp(-c)))).astype(BF16)
    o_ref[...] = _dot(s, w_ref[...].astype(BF16)) + b_ref[...]


def _ada(c, w_ada, b_ada):
    m = c.shape[0]
    n = w_ada.shape[1]
    tn = n // N_ADA
    return pl.pallas_call(
        _ada_kernel,
        out_shape=jax.ShapeDtypeStruct((m, n), F32),
        grid=(n // tn,),
        in_specs=[pl.BlockSpec((m, D_MODEL), lambda j: (0, 0)),
                  pl.BlockSpec((D_MODEL, tn), lambda j: (0, j)),
                  pl.BlockSpec((1, tn), lambda j: (0, j))],
        out_specs=pl.BlockSpec((m, tn), lambda j: (0, j)),
        compiler_params=_params(("arbitrary",)),
        name="ada",
    )(c, w_ada, b_ada.reshape(1, n))


def _stage_a_kernel(pair_major, x_ref, ada_ref, cos_ref, sina_ref, sinb_ref, gf_ref, gm_ref, gq_ref,
                    gk_ref, seg_ref, wg_ref, wu_ref, wd_ref, win_ref,
                    x1_ref, q_ref, k_ref, v_ref, u_ref, u_scr):
    kb, r, _ = x_ref.shape
    n = kb * r
    x = x_ref[...]
    ada = ada_ref[...]
    sh1, sc1, gt1 = ada[:, 0:1], ada[:, 1:2], ada[:, 2:3]
    sh2, sc2 = ada[:, 3:4], ada[:, 4:5]

    h = _rms(x, gf_ref[...]) * (1.0 + sc1) + sh1
    f = _swiglu(h.reshape(n, D_MODEL).astype(BF16), wg_ref, wu_ref, wd_ref)
    x1 = x + 0.5 * gt1 * f.reshape(kb, r, D_MODEL)
    x1_ref[...] = x1

    h = _rms(x1, gm_ref[...]) * (1.0 + sc2) + sh2
    proj = _dot(h.reshape(n, D_MODEL).astype(BF16), win_ref[...])

    cos = jnp.concatenate([cos_ref[...]] * (D_ATT // LANES), axis=-1)
    sina = jnp.concatenate([sina_ref[...]] * (D_ATT // LANES), axis=-1)
    sinb = jnp.concatenate([sinb_ref[...]] * (D_ATT // LANES), axis=-1)

    def head_norm_rope(z, g):
        ms = _dot((z * z).astype(BF16), seg_ref[...])
        zn = z * lax.rsqrt(ms + EPS) * g
        up = pltpu.roll(zn, D_ATT - ROT_DIM // 2, 1).reshape(kb, r, D_ATT)
        dn = pltpu.roll(zn, ROT_DIM // 2, 1).reshape(kb, r, D_ATT)
        return zn.reshape(kb, r, D_ATT) * cos + up * sina + dn * sinb

    q = head_norm_rope(proj[:, 0:D_ATT], gq_ref[...])
    k = head_norm_rope(proj[:, D_ATT:2 * D_ATT], gk_ref[...])
    v = proj[:, 2 * D_ATT:3 * D_ATT]
    if pair_major:
        for z, z_ref in ((q.reshape(n, D_ATT), q_ref), (k.reshape(n, D_ATT), k_ref), (v, v_ref)):
            for hp in range(D_ATT // LANES):
                z_ref[hp] = z[:, hp * LANES:(hp + 1) * LANES]
    else:
        q_ref[...] = q
        k_ref[...] = k
        v_ref[...] = v.reshape(kb, r, D_ATT)
    u = proj[:, 3 * D_ATT:]
    for sg in range(SUPER):
        u_scr[sg] = u[:, sg * LANES:(sg + 1) * LANES]
    for sg in range(SUPER):
        for s in range(CHUNK):
            u_ref[sg, :, s * LANES:(s + 1) * LANES] = u_scr[sg, pl.ds(s, n // CHUNK, stride=CHUNK), :]


def _rope_tables(pos):
    half = ROT_DIM // 2
    j = np.arange(LANES) % HEAD_DIM
    lo = jnp.asarray(j < half)
    hi = jnp.asarray((j >= half) & (j < ROT_DIM))
    inv = ROPE_THETA ** (-jnp.asarray(j % half, F32) / half)
    ang = pos.astype(F32)[:, None] * inv[None, :]
    c = jnp.cos(ang)
    s = jnp.sin(ang)
    cos = jnp.where(lo | hi, c, 1.0)
    sina = jnp.where(lo, -s, 0.0)
    sinb = jnp.where(hi, s, 0.0)
    return cos, sina, sinb


def _stage_a(x3, ada3, tables, per_seq, wts, tm):
    b3, r3, _ = x3.shape
    if per_seq:
        kb, r = min(tm // r3, b3), r3
        grid = (b3 // kb,)
        tok = lambda i: (i, 0, 0)
        tab = lambda i: (0, 0, 0)
        ada_spec = pl.BlockSpec((kb, N_ADA, D_MODEL), lambda i: (i, 0, 0))
    else:
        kb, r = 1, min(tm, r3)
        grid = (r3 // r,)
        tok = lambda i: (0, i, 0)
        tab = lambda i: (0, i, 0)
        ada_spec = pl.BlockSpec((1, N_ADA, D_MODEL), lambda i: (0, 0, 0))
    n = kb * r
    ntok = b3 * r3
    cos, sina, sinb = tables
    gf, gm, gq, gk, seg, wg, wu, wd, win = wts
    tab_spec = pl.BlockSpec((1, r, LANES), tab)
    if per_seq:
        att_spec = pl.BlockSpec((kb, r, D_ATT), tok)
        att_shape = jax.ShapeDtypeStruct((b3, r3, D_ATT), F32)
    else:
        att_spec = pl.BlockSpec((D_ATT // LANES, n, LANES), lambda i: (0, i, 0))
        att_shape = jax.ShapeDtypeStruct((D_ATT // LANES, ntok, LANES), F32)
    return pl.pallas_call(
        functools.partial(_stage_a_kernel, not per_seq),
        out_shape=(jax.ShapeDtypeStruct(x3.shape, F32), att_shape, att_shape, att_shape,
                   jax.ShapeDtypeStruct((SUPER, ntok // CHUNK, CW), F32)),
        grid=grid,
        in_specs=[pl.BlockSpec((kb, r, D_MODEL), tok), ada_spec, tab_spec, tab_spec, tab_spec,
                  _const_spec(gf.shape), _const_spec(gm.shape), _const_spec(gq.shape),
                  _const_spec(gk.shape), _const_spec(seg.shape), _const_spec(wg.shape),
                  _const_spec(wu.shape), _const_spec(wd.shape), _const_spec(win.shape)],
        out_specs=(pl.BlockSpec((kb, r, D_MODEL), tok), att_spec, att_spec, att_spec,
                   pl.BlockSpec((SUPER, n // CHUNK, CW), lambda i: (0, i, 0))),
        scratch_shapes=[pltpu.VMEM((SUPER, n, LANES), F32)],
        compiler_params=_params(("arbitrary",)),
        name="stage_a",
    )(x3, ada3, cos, sina, sinb, gf, gm, gq, gk, seg, wg, wu, wd, win)


UNITS_PER_ITER = 4


def _rows(start, size, stride):
    return pl.ds(start, size) if stride == 1 else pl.ds(start, size, stride=stride)


DIL_PARTS = 4


def _dil_attn_part(i, part, q_ref, k_ref, v_ref, o_ref, kbuf, vbuf, acc, mrow, lrow):
    sb = q_ref.shape[1]
    tq = WIN_STEPS

    @pl.when(jnp.logical_and(part == 0, i == 0))
    def _():
        kbuf[0:sb, :] = jnp.zeros((sb, LANES), F32)
        vbuf[0:sb, :] = jnp.zeros((sb, LANES), F32)

    @pl.when(jnp.logical_and(part == 0, i > 0))
    def _():
        kbuf[0:sb, :] = kbuf[sb:2 * sb, :]
        vbuf[0:sb, :] = vbuf[sb:2 * sb, :]

    @pl.when(part == 0)
    def _():
        kbuf[sb:2 * sb, :] = k_ref[0]
        vbuf[sb:2 * sb, :] = v_ref[0]

    row = lax.broadcasted_iota(jnp.int32, (tq, 2 * tq), 0)
    col = lax.broadcasted_iota(jnp.int32, (tq, 2 * tq), 1)
    band = jnp.where(col >= row, 0.0, NEG)
    band = jnp.where(col <= row + tq, band, NEG)
    band0 = jnp.where(col >= tq, band, NEG)
    band_first = jnp.where(i == 0, band0, band)
    low = lax.broadcasted_iota(jnp.int32, (tq, LANES), 1) < HEAD_DIM
    high = jnp.logical_not(low)

    def unit(d, qs, bias, mode):
        qp = (q_ref[0, _rows(qs, tq, d), :] * (HEAD_DIM ** -0.5)).astype(BF16)
        kp = kbuf[_rows(sb + qs - d * tq, 2 * tq, d), :].astype(BF16)
        vp = vbuf[_rows(sb + qs - d * tq, 2 * tq, d), :].astype(BF16)
        parts = []
        for sel in (low, high):
            qe = jnp.where(sel, qp, jnp.zeros_like(qp))
            s = lax.dot_general(qe, kp, _NT, preferred_element_type=F32) + bias
            mx = jnp.max(s, axis=1, keepdims=True)
            p = jnp.exp(s - mx)
            parts.append((_dot(p.astype(BF16), vp), mx, jnp.sum(p, axis=1, keepdims=True)))
        o = jnp.where(low, parts[0][0], parts[1][0])
        mx = jnp.where(low, parts[0][1], parts[1][1])
        l = jnp.where(low, parts[0][2], parts[1][2])
        rows = _rows(qs, tq, d)
        if mode != "init":
            m_old = mrow[rows, :]
            m_new = jnp.maximum(m_old, mx)
            a_old = jnp.exp(m_old - m_new)
            a_new = jnp.exp(mx - m_new)
            o = acc[rows, :] * a_old + o * a_new
            l = lrow[rows, :] * a_old + l * a_new
            mx = m_new
        if mode == "final":
            o_ref[0, rows, :] = o * (1.0 / l)
        else:
            acc[rows, :] = o
            mrow[rows, :] = mx
            lrow[rows, :] = l

    upi = UNITS_PER_ITER
    d16, d4 = DILATIONS[2], DILATIONS[1]

    def body16(g, c):
        for u in range(upi):
            unit(d16, g * upi + u, band_first, "init")
        return c

    def body4(r, c):
        for mb in range(sb // d4 // tq):
            unit(d4, r + d4 * tq * mb, band_first if mb == 0 else band, "merge")
        return c

    def body1(g, c):
        for u in range(upi):
            bias = jnp.where(jnp.logical_and(i == 0, g == 0), band0, band) if u == 0 else band
            unit(1, (g * upi + u) * tq, bias, "final")
        return c

    counts = (d16 // upi, d4, sb // tq // upi)
    share = sum(counts) // DIL_PARTS
    assert share * DIL_PARTS == sum(counts)
    first = 0
    for body, cnt in zip((body16, body4, body1), counts):
        lo = jnp.clip(part * share - first, 0, cnt)
        hi = jnp.clip((part + 1) * share - first, 0, cnt)
        lax.fori_loop(lo, hi, body, 0)
        first += cnt


def _sample_bias():
    w = WIN_MAX
    out = []
    i = (np.arange(N_HEADS * CHUNK) % CHUNK)[:, None]
    for d in DILATIONS:
        span = WIN_STEPS * d
        c = np.arange(w - span, w)[None, :]
        dist = w + i - c
        ok_buf = (dist % d == 0) & (dist <= span)
        cn = np.arange(LANES)[None, :] - (LANES - CHUNK)
        dn = i - cn
        ok_new = (cn >= 0) & (dn >= 0) & (dn % d == 0) & (dn <= span)
        ok = np.concatenate([ok_buf, ok_new], axis=1)
        out.append(jnp.asarray(np.where(ok, 0.0, NEG), F32))
    return out


def _sample_attn_kernel(q_ref, kn_ref, vn_ref, kc_ref, vc_ref, b1_ref, b2_ref, b3_ref,
                        o_ref, ko_ref, vo_ref):
    w = kc_ref.shape[2]
    nq = q_ref.shape[1]
    kc = kc_ref[0]
    vc = vc_ref[0]
    lane = lax.broadcasted_iota(jnp.int32, (D_ATT, LANES), 1)
    pad = jnp.zeros((LANES - nq, D_ATT), F32)

    def shifted(buf, new, out_ref):
        newt = jnp.concatenate([pad, new], axis=0).T
        rolled = pltpu.roll(buf, w - nq, 1)
        out_ref[0, :, 0:w - LANES] = rolled[:, 0:w - LANES]
        out_ref[0, :, w - LANES:] = jnp.where(lane >= LANES - nq, newt, rolled[:, w - LANES:])
        return newt

    knb = shifted(kc, kn_ref[0], ko_ref).astype(BF16)
    vnb = shifted(vc, vn_ref[0], vo_ref).astype(BF16)

    rows = N_HEADS * nq
    q = q_ref[0] * (HEAD_DIM ** -0.5)
    qt = jnp.concatenate([q] * N_HEADS, axis=0)
    rhead = lax.broadcasted_iota(jnp.int32, (rows, D_ATT), 0) // nq
    lhead = lax.broadcasted_iota(jnp.int32, (rows, D_ATT), 1) // HEAD_DIM
    qe = jnp.where(rhead == lhead, qt, 0.0).astype(BF16)
    kcb = kc.astype(BF16)
    vcb = vc.astype(BF16)
    s_buf = _dot(qe, kcb)
    s_new = _dot(qe, knb)

    outs, lses = [], []
    for d, b_ref in zip(DILATIONS, (b1_ref, b2_ref, b3_ref)):
        span = WIN_STEPS * d
        s = jnp.concatenate([s_buf[:, w - span:], s_new], axis=1) + b_ref[...]
        mx = jnp.max(s, axis=1, keepdims=True)
        p = jnp.exp(s - mx)
        l = jnp.sum(p, axis=1, keepdims=True)
        vcat = jnp.concatenate([vcb[:, w - span:], vnb], axis=1)
        o = lax.dot_general(p.astype(BF16), vcat, _NT, preferred_element_type=F32)
        outs.append(o * (1.0 / l))
        lses.append(mx + jnp.log(l))
    lmax = jnp.maximum(jnp.maximum(lses[0], lses[1]), lses[2])
    es = [jnp.exp(l - lmax) for l in lses]
    o = (es[0] * outs[0] + es[1] * outs[1] + es[2] * outs[2]) * (1.0 / (es[0] + es[1] + es[2]))
    o = jnp.where(rhead == lhead, o, 0.0)
    acc = o[0:nq]
    for h in range(1, N_HEADS):
        acc = acc + o[h * nq:(h + 1) * nq]
    o_ref[0] = acc


def _attention_kernel(nb, ndil, nsb, q_ref, kn_ref, vn_ref, kc_ref, vc_ref, b1_ref, b2_ref, b3_ref,
                      qp_ref, kp_ref, vp_ref, o_ref, ko_ref, vo_ref, op_ref,
                      kbuf, vbuf, acc, mrow, lrow):
    j = pl.program_id(0)

    @pl.when(j < nb)
    def _():
        _sample_attn_kernel(q_ref, kn_ref, vn_ref, kc_ref, vc_ref, b1_ref, b2_ref, b3_ref,
                            o_ref, ko_ref, vo_ref)

    @pl.when(j < ndil)
    def _():
        _dil_attn_part((j // DIL_PARTS) % nsb, j % DIL_PARTS, qp_ref, kp_ref, vp_ref, op_ref,
                       kbuf, vbuf, acc, mrow, lrow)


def _attention(q, kn, vn, kc, vc, qp, kp, vp):
    b, nq, _ = q.shape
    w = kc.shape[2]
    npair, s, _ = qp.shape
    sb = WIN_STEPS * DILATIONS[-1]
    assert w == WIN_MAX and nq == CHUNK and s % sb == 0 and DILATIONS[0] == 1
    nsb = s // sb
    ndil = npair * nsb * DIL_PARTS
    b1, b2, b3 = _sample_bias()
    seq = lambda j: (jnp.minimum(j, b - 1), 0, 0)
    new = pl.BlockSpec((1, nq, D_ATT), seq)
    buf = pl.BlockSpec((1, D_ATT, w), seq)

    def item(j):
        t = jnp.minimum(j, ndil - 1) // DIL_PARTS
        return (t // nsb, t % nsb, 0)

    blk = pl.BlockSpec((1, sb, LANES), item)
    return pl.pallas_call(
        functools.partial(_attention_kernel, b, ndil, nsb),
        out_shape=(jax.ShapeDtypeStruct(q.shape, F32), jax.ShapeDtypeStruct(kc.shape, F32),
                   jax.ShapeDtypeStruct(vc.shape, F32), jax.ShapeDtypeStruct(qp.shape, F32)),
        grid=(max(b, ndil),),
        in_specs=[new, new, new, buf, buf, _const_spec(b1.shape), _const_spec(b2.shape),
                  _const_spec(b3.shape), blk, blk, blk],
        out_specs=(new, buf, buf, blk),
        scratch_shapes=[pltpu.VMEM((2 * sb, LANES), F32), pltpu.VMEM((2 * sb, LANES), F32),
                        pltpu.VMEM((sb, LANES), F32), pltpu.VMEM((sb, LANES), F32),
                        pltpu.VMEM((sb, LANES), F32)],
        compiler_params=_params(("arbitrary",)),
        name="attention",
    )(q, kn, vn, kc, vc, b1, b2, b3, qp, kp, vp)


def _split(a):
    hi = a.astype(BF16)
    return hi, (a - hi.astype(F32)).astype(BF16)


def _s5_prep_kernel(ar_ref, ai_ref, ldt_ref, bre_ref, bim_ref, cre_ref, cim_ref,
                    p_ref, qt_ref, t_ref, a_ref, pf_scr, q2_scr):
    ar, ai = ar_ref[0], ai_ref[0]
    dt = jnp.exp(ldt_ref[0])
    mag = jnp.exp(ar * dt)
    lr, li = mag * jnp.cos(ai * dt), mag * jnp.sin(ai * dt)
    den = 1.0 / (ar * ar + ai * ai)
    kr = ((lr - 1.0) * ar + li * ai) * den
    ki = (li * ar - (lr - 1.0) * ai) * den
    bre, bim = bre_ref[0], bim_ref[0]
    bbr, bbi = kr * bre - ki * bim, kr * bim + ki * bre
    cre, cim = cre_ref[0], cim_ref[0]
    pr, pi = [jnp.ones_like(lr)], [jnp.zeros_like(lr)]
    for _ in range(CHUNK):
        pr.append(pr[-1] * lr - pi[-1] * li)
        pi.append(pr[-2] * li + pi[-1] * lr)
    dd = 1.0 / (pr[CHUNK] * pr[CHUNK] + pi[CHUNK] * pi[CHUNK])
    dr, di = pr[CHUNK] * dd, -pi[CHUNK] * dd
    for s in range(CHUNK):
        n = CHUNK - 1 - s
        rows = slice(s * LANES, (s + 1) * LANES)
        pf_scr[rows, 0:SG_STATE] = bbr * pr[n] - bbi * pi[n]
        pf_scr[rows, SG_STATE:] = bbr * pi[n] + bbi * pr[n]
        n = s + 1
        re, im = cre * pr[n] - cim * pi[n], cre * pi[n] + cim * pr[n]
        qt_ref[0, rows, 0:SG_STATE] = re.astype(BF16)
        qt_ref[0, rows, SG_STATE:] = (-im).astype(BF16)
        q2_scr[rows, 0:SG_STATE] = dr * re - di * im
        q2_scr[rows, SG_STATE:] = -(dr * im + di * re)
    pf = pf_scr[...]
    p_ref[0] = pf.astype(BF16)
    a_ref[0] = jnp.concatenate([pr[CHUNK], pi[CHUNK]], axis=-1)
    ph, plo = _split(pf)
    qh, qlo = _split(q2_scr[...])
    t = (lax.dot_general(ph, qh, _NT, preferred_element_type=F32)
         + lax.dot_general(ph, qlo, _NT, preferred_element_type=F32)
         + lax.dot_general(plo, qh, _NT, preferred_element_type=F32))
    rb = lax.broadcasted_iota(jnp.int32, (CW, CW), 0) // LANES
    cb = lax.broadcasted_iota(jnp.int32, (CW, CW), 1) // LANES
    t_ref[0] = jnp.where(rb <= cb, t, 0.0).astype(BF16)


def _block_diag(a):
    g = LANES // SSM_GROUP
    _, r, c = a.shape
    a = a.reshape(SUPER, g, r, c)
    eye = jnp.eye(g, dtype=a.dtype)
    return jnp.einsum("sgrc,gh->sgrhc", a, eye).reshape(SUPER, g * r, g * c)


def _s5_prep(a_re, a_im, log_dt, b_re, b_im, c_re, c_im):
    row = lambda a: a.reshape(SUPER, 1, SG_STATE)
    ldt = jnp.broadcast_to(log_dt[:, None], (N_SSM_GROUPS, SSM_STATE))
    bre = _block_diag(jnp.swapaxes(b_re, 1, 2))
    bim = _block_diag(jnp.swapaxes(b_im, 1, 2))
    cre = _block_diag(c_re)
    cim = _block_diag(c_im)
    vec = pl.BlockSpec((1, 1, SG_STATE), lambda g: (g, 0, 0))
    mat = pl.BlockSpec((1, LANES, SG_STATE), lambda g: (g, 0, 0))
    big = pl.BlockSpec((1, CW, CW), lambda g: (g, 0, 0))
    big_shape = jax.ShapeDtypeStruct((SUPER, CW, CW), BF16)
    return pl.pallas_call(
        _s5_prep_kernel,
        out_shape=(big_shape, big_shape, big_shape, jax.ShapeDtypeStruct((SUPER, 1, CW), F32)),
        grid=(SUPER,),
        in_specs=[vec, vec, vec, mat, mat, mat, mat],
        out_specs=(big, big, big, pl.BlockSpec((1, 1, CW), lambda g: (g, 0, 0))),
        scratch_shapes=[pltpu.VMEM((CW, CW), F32), pltpu.VMEM((CW, CW), F32)],
        compiler_params=_params(("arbitrary",)),
        name="s5_prep",
    )(row(a_re), row(a_im), row(ldt), bre, bim, cre, cim)


def _gelu_tanh(y):
    return 0.5 * y * (1.0 + jnp.tanh(math.sqrt(2.0 / math.pi) * (y + 0.044715 * (y * y * y))))


def _s5_out(u, ub, hin, qt_ref, t_ref, d_ref):
    y = (_dot(ub, t_ref[0])
         + lax.dot_general(hin.astype(BF16), qt_ref[0], _NT, preferred_element_type=F32)
         + d_ref[0] * u)
    return _gelu_tanh(y)


def _s5_scan_kernel(u_ref, p_ref, qt_ref, t_ref, a_ref, d_ref, y_ref, hout_ref,
                    h_scr, gs_scr, hin_scr):
    tc = u_ref.shape[1]

    @pl.when(pl.program_id(1) == 0)
    def _():
        h_scr[...] = jnp.zeros_like(h_scr)

    u = u_ref[0]
    ub = u.astype(BF16)
    gs_scr[...] = _dot(ub, p_ref[0])
    a = a_ref[0]
    ar, ai = a[:, 0:SG_STATE], a[:, SG_STATE:]

    def step(k, carry):
        hr, hi = carry
        hin_scr[pl.ds(k, 1), 0:SG_STATE] = hr
        hin_scr[pl.ds(k, 1), SG_STATE:] = hi
        g = gs_scr[pl.ds(k, 1), :]
        return (ar * hr - ai * hi + g[:, 0:SG_STATE], ar * hi + ai * hr + g[:, SG_STATE:])

    h0 = h_scr[...]
    hr, hi = lax.fori_loop(0, tc, step, (h0[:, 0:SG_STATE], h0[:, SG_STATE:]))
    h_new = jnp.concatenate([hr, hi], axis=-1)
    h_scr[...] = h_new
    hout_ref[0] = h_new
    y_ref[0] = _s5_out(u, ub, hin_scr[...], qt_ref, t_ref, d_ref)


def _s5_scan(u, mats, dvec, tc):
    p, qt, t, a = mats
    nch = u.shape[1]
    tc = min(tc, nch)
    tile = pl.BlockSpec((1, tc, CW), lambda g, i: (g, i, 0))
    big = pl.BlockSpec((1, CW, CW), lambda g, i: (g, 0, 0))
    vec = pl.BlockSpec((1, 1, CW), lambda g, i: (g, 0, 0))
    return pl.pallas_call(
        _s5_scan_kernel,
        out_shape=(jax.ShapeDtypeStruct(u.shape, F32), jax.ShapeDtypeStruct((SUPER, 1, CW), F32)),
        grid=(SUPER, nch // tc),
        in_specs=[tile, big, big, big, vec, vec],
        out_specs=(tile, vec),
        scratch_shapes=[pltpu.VMEM((1, CW), F32), pltpu.VMEM((tc, CW), F32),
                        pltpu.VMEM((tc, CW), F32)],
        compiler_params=_params(("arbitrary", "arbitrary")),
        name="s5_scan",
    )(u, p, qt, t, a, dvec)


def _s5_step_kernel(u_ref, h0_ref, p_ref, qt_ref, t_ref, a_ref, d_ref, y_ref, hout_ref):
    u = u_ref[0]
    ub = u.astype(BF16)
    h0 = h0_ref[0]
    hr, hi = h0[:, 0:SG_STATE], h0[:, SG_STATE:]
    a = a_ref[0]
    ar, ai = a[:, 0:SG_STATE], a[:, SG_STATE:]
    gs = _dot(ub, p_ref[0])
    hout_ref[0] = jnp.concatenate([ar * hr - ai * hi + gs[:, 0:SG_STATE],
                                   ar * hi + ai * hr + gs[:, SG_STATE:]], axis=-1)
    y_ref[0] = _s5_out(u, ub, h0, qt_ref, t_ref, d_ref)


def _s5_step(u, h0, mats, dvec):
    p, qt, t, a = mats
    b = u.shape[1]
    tile = pl.BlockSpec((1, b, CW), lambda g: (g, 0, 0))
    big = pl.BlockSpec((1, CW, CW), lambda g: (g, 0, 0))
    vec = pl.BlockSpec((1, 1, CW), lambda g: (g, 0, 0))
    shape = jax.ShapeDtypeStruct(u.shape, F32)
    return pl.pallas_call(
        _s5_step_kernel,
        out_shape=(shape, shape),
        grid=(SUPER,),
        in_specs=[tile, tile, big, big, big, vec, vec],
        out_specs=(tile, tile),
        compiler_params=_params(("arbitrary",)),
        name="s5_step",
    )(u, h0, p, qt, t, a, dvec)


def _stage_b_kernel(pair_major, x_ref, ada_ref, att_ref, gs_ref, g2_ref, wglu_ref, wout_ref,
                    wg_ref, wu_ref, wd_ref, y_ref, g_scr):
    kb, r, _ = x_ref.shape
    n = kb * r
    x = x_ref[...]
    ada = ada_ref[...]
    gt2 = ada[:, 5:6]
    sh3, sc3, gt3 = ada[:, 6:7], ada[:, 7:8], ada[:, 8:9]

    if pair_major:
        o_att = jnp.concatenate([att_ref[hp] for hp in range(D_ATT // LANES)], axis=-1)
    else:
        o_att = att_ref[...].reshape(n, D_ATT)

    for sg in range(SUPER):
        for s in range(CHUNK):
            g_scr[sg, pl.ds(s, n // CHUNK, stride=CHUNK), :] = gs_ref[sg, :, s * LANES:(s + 1) * LANES]
    g = jnp.concatenate([g_scr[sg] for sg in range(SUPER)], axis=-1)
    gl = _dot(g.astype(BF16), wglu_ref[...])
    o_ssm = gl[:, 0:D_SSM] * (1.0 / (1.0 + jnp.exp(-gl[:, D_SSM:])))
    mix = _dot(jnp.concatenate([o_att, o_ssm], axis=-1).astype(BF16), wout_ref[...])
    x2 = x + gt2 * mix.reshape(kb, r, D_MODEL)

    h = _rms(x2, g2_ref[...]) * (1.0 + sc3) + sh3
    f = _swiglu(h.reshape(n, D_MODEL).astype(BF16), wg_ref, wu_ref, wd_ref)
    y_ref[...] = x2 + 0.5 * gt3 * f.reshape(kb, r, D_MODEL)


def _stage_b(x3, ada3, att, gs, per_seq, wts, tm):
    b3, r3, _ = x3.shape
    if per_seq:
        kb, r = min(tm // r3, b3), r3
        grid = (b3 // kb,)
        tok = lambda i: (i, 0, 0)
        ada_spec = pl.BlockSpec((kb, N_ADA, D_MODEL), lambda i: (i, 0, 0))
    else:
        kb, r = 1, min(tm, r3)
        grid = (r3 // r,)
        tok = lambda i: (0, i, 0)
        ada_spec = pl.BlockSpec((1, N_ADA, D_MODEL), lambda i: (0, 0, 0))
    n = kb * r
    g2, wglu, wout, wg, wu, wd = wts
    if per_seq:
        att_spec = pl.BlockSpec((kb, r, D_ATT), tok)
    else:
        att_spec = pl.BlockSpec((D_ATT // LANES, n, LANES), lambda i: (0, i, 0))
    return pl.pallas_call(
        functools.partial(_stage_b_kernel, not per_seq),
        out_shape=jax.ShapeDtypeStruct(x3.shape, F32),
        grid=grid,
        in_specs=[pl.BlockSpec((kb, r, D_MODEL), tok), ada_spec, att_spec,
                  pl.BlockSpec((SUPER, n // CHUNK, CW), lambda i: (0, i, 0)),
                  _const_spec(g2.shape), _const_spec(wglu.shape), _const_spec(wout.shape),
                  _const_spec(wg.shape), _const_spec(wu.shape), _const_spec(wd.shape)],
        out_specs=pl.BlockSpec((kb, r, D_MODEL), tok),
        scratch_shapes=[pltpu.VMEM((SUPER, n, LANES), F32)],
        compiler_params=_params(("arbitrary",)),
        name="stage_b",
    )(x3, ada3, att, gs, g2, wglu, wout, wg, wu, wd)


def _layer(x_p, x_s, c_p, c_s, cache_k, cache_v, st_re, st_im, p):
    (w_ada, b_ada, g_ffn1, w1_gate, w1_up, w1_down, g_mix, w_in, g_q, g_k,
     a_re, a_im, log_dt, b_re, b_im, c_re, c_im, d_skip, w_glu, w_out,
     g_ffn2, w2_gate, w2_up, w2_down) = p
    bp, s, _ = x_p.shape
    bs, ns, _ = x_s.shape
    assert bp == 1 and ns == CHUNK and s % (WIN_STEPS * DILATIONS[-1]) == 0
    w_buf = cache_k.shape[2]
    tm = 256

    rows = bp + bs
    pad = (-rows) % 8
    c_all = jnp.concatenate([c_p, c_s, jnp.zeros((pad, D_MODEL), F32)], axis=0)
    ada = _ada(c_all, w_ada, b_ada)
    ada_p = ada[0:bp].reshape(bp, N_ADA, D_MODEL)
    ada_s = ada[bp:rows].reshape(bs, N_ADA, D_MODEL)

    vec = lambda g: g.reshape(1, 1, -1)
    head_gain = lambda g: jnp.tile(g, N_HEADS).reshape(1, D_ATT)
    seg = jnp.asarray(np.kron(np.eye(N_HEADS), np.full((HEAD_DIM, HEAD_DIM), 1.0 / HEAD_DIM)), BF16)
    bf = lambda w: w.astype(BF16)
    wts_a = (vec(g_ffn1), vec(g_mix), head_gain(g_q), head_gain(g_k), seg,
             bf(w1_gate), bf(w1_up), bf(w1_down), bf(w_in))
    wts_b = (vec(g_ffn2), bf(w_glu), bf(w_out), bf(w2_gate), bf(w2_up), bf(w2_down))

    tab_p = tuple(t[None] for t in _rope_tables(jnp.arange(s)))
    tab_s = tuple(t[None] for t in _rope_tables(PAST_LEN + jnp.arange(ns)))

    x1_p, q_p, k_p, v_p, u_p = _stage_a(x_p, ada_p, tab_p, False, wts_a, tm)
    x1_s, q_s, k_s, v_s, u_s = _stage_a(x_s, ada_s, tab_s, True, wts_a, tm)

    o_s, kwin_s, vwin_s, o_p = _attention(q_s, k_s, v_s, cache_k, cache_v, q_p, k_p, v_p)

    mats = _s5_prep(a_re, a_im, log_dt, b_re, b_im, c_re, c_im)
    dvec = jnp.tile(d_skip.reshape(SUPER, 1, LANES), (1, 1, CHUNK))
    gs_p, hfin_p = _s5_scan(u_p, mats, dvec, 512)
    h0 = jnp.stack([jnp.concatenate([st_re[:, g * SG_STATE:(g + 1) * SG_STATE],
                                     st_im[:, g * SG_STATE:(g + 1) * SG_STATE]], axis=-1)
                    for g in range(SUPER)], axis=0)
    gs_s, hfin_s = _s5_step(u_s, h0, mats, dvec)

    y_p = _stage_b(x1_p, ada_p, o_p, gs_p, False, wts_b, tm)
    y_s = _stage_b(x1_s, ada_s, o_s, gs_s, True, wts_b, tm)

    def unpack_state(h):
        b = h.shape[1]
        re = jnp.concatenate([h[g, :, 0:SG_STATE] for g in range(SUPER)], axis=-1)
        im = jnp.concatenate([h[g, :, SG_STATE:] for g in range(SUPER)], axis=-1)
        return (re.reshape(b, N_SSM_GROUPS, SSM_STATE), im.reshape(b, N_SSM_GROUPS, SSM_STATE))

    keep = min(WIN_MAX, s)
    tail = lambda a: jnp.transpose(a[:, s - keep:], (1, 0, 2)).reshape(bp, keep, N_HEADS, HEAD_DIM)
    kwin_p, vwin_p = tail(k_p), tail(v_p)
    hre_p, him_p = unpack_state(hfin_p)
    hre_s, him_s = unpack_state(hfin_s)
    unflip = lambda a: jnp.transpose(a.reshape(bs, N_HEADS, HEAD_DIM, w_buf), (0, 3, 1, 2))
    return (y_p, y_s, kwin_p, vwin_p, hre_p, him_p, unflip(kwin_s), unflip(vwin_s), hre_s, him_s)


def kernel(x_prompt, x_sample, c_prompt, c_sample, cache_k_win, cache_v_win, state_ssm_re, state_ssm_im, w_ada, b_ada, g_ffn1, w1_gate, w1_up, w1_down, g_mix, w_in, g_q, g_k, ssm_a_re, ssm_a_im, ssm_log_dt, ssm_b_re, ssm_b_im, ssm_c_re, ssm_c_im, ssm_d, w_glu, w_out, g_ffn2, w2_gate, w2_up, w2_down):
    depth = w_ada.shape[0]
    assert depth == 1
    bs = x_sample.shape[0]
    w_buf = cache_k_win.shape[2]
    p = tuple(a[0] for a in (w_ada, b_ada, g_ffn1, w1_gate, w1_up, w1_down, g_mix, w_in, g_q, g_k,
                             ssm_a_re, ssm_a_im, ssm_log_dt, ssm_b_re, ssm_b_im, ssm_c_re, ssm_c_im,
                             ssm_d, w_glu, w_out, g_ffn2, w2_gate, w2_up, w2_down))
    flip = lambda a: jnp.transpose(a[0], (0, 2, 3, 1)).reshape(bs, D_ATT, w_buf)
    outs = _layer(x_prompt, x_sample, c_prompt, c_sample, flip(cache_k_win), flip(cache_v_win),
                  state_ssm_re[0].reshape(bs, N_SSM_GROUPS * SSM_STATE),
                  state_ssm_im[0].reshape(bs, N_SSM_GROUPS * SSM_STATE), p)
    return tuple(o[None] if i >= 2 else o for i, o in enumerate(outs))
```

```python
import functools
import math

import numpy as np
import jax
import jax.numpy as jnp
from jax import lax
from jax.experimental import pallas as pl
from jax.experimental.pallas import tpu as pltpu

F32 = jnp.float32
BF16 = jnp.bfloat16

D_MODEL = 1024
D_ATT = 512
D_SSM = 512
HEAD_DIM = 64
N_HEADS = 8
ROT_DIM = 16
ROPE_THETA = 500000.0
DILATIONS = (1, 4, 16)
WIN_STEPS = 128
WIN_MAX = 2048
PAST_LEN = 8192
SSM_GROUP = 16
N_SSM_GROUPS = 32
SSM_STATE = 64
D_FF = 2816
N_ADA = 9
EPS = 1e-6

LANES = 128
CHUNK = 8
SUPER = D_SSM // LANES
SG_STATE = (LANES // SSM_GROUP) * SSM_STATE
CW = CHUNK * LANES
NEG = -1e30
VMEM_LIMIT = 56 * 1024 * 1024

_NT = (((1,), (1,)), ((), ()))


def _params(sem, vmem=VMEM_LIMIT):
    return pltpu.CompilerParams(dimension_semantics=sem, vmem_limit_bytes=vmem)


def _const_spec(shape):
    nd = len(shape)
    return pl.BlockSpec(shape, lambda *_: (0,) * nd, pipeline_mode=pl.Buffered(1))


def _dot(a, b):
    return jnp.dot(a, b, preferred_element_type=F32)


def _rms(x, g):
    ms = jnp.mean(x * x, axis=-1, keepdims=True)
    return x * lax.rsqrt(ms + EPS) * g


def _swiglu(h, wg_ref, wu_ref, wd_ref):
    a = _dot(h, wg_ref[...])
    b = _dot(h, wu_ref[...])
    t = (a * (1.0 / (1.0 + jnp.exp(-a))) * b).astype(BF16)
    return _dot(t, wd_ref[...])


def _ada_kernel(c_ref, w_ref, b_ref, o_ref):
    c = c_ref[...]
    s = (c * (1.0 / (1.0 + jnp.exp(-c)))).astype(BF16)
    o_ref[...] = _dot(s, w_ref[...].astype(BF16)) + b_ref[...]


def _ada(c, w_ada, b_ada):
    m = c.shape[0]
    n = w_ada.shape[1]
    tn = n // N_ADA
    return pl.pallas_call(
        _ada_kernel,
        out_shape=jax.ShapeDtypeStruct((m, n), F32),
        grid=(n // tn,),
        in_specs=[pl.BlockSpec((m, D_MODEL), lambda j: (0, 0)),
                  pl.BlockSpec((D_MODEL, tn), lambda j: (0, j)),
                  pl.BlockSpec((1, tn), lambda j: (0, j))],
        out_specs=pl.BlockSpec((m, tn), lambda j: (0, j)),
        compiler_params=_params(("arbitrary",)),
        name="ada",
    )(c, w_ada, b_ada.reshape(1, n))


def _stage_a_kernel(pair_major, x_ref, ada_ref, cos_ref, sina_ref, sinb_ref, gf_ref, gm_ref, gq_ref,
                    gk_ref, seg_ref, wg_ref, wu_ref, wd_ref, win_ref,
                    x1_ref, q_ref, k_ref, v_ref, u_ref, u_scr):
    kb, r, _ = x_ref.shape
    n = kb * r
    x = x_ref[...]
    ada = ada_ref[...]
    sh1, sc1, gt1 = ada[:, 0:1], ada[:, 1:2], ada[:, 2:3]
    sh2, sc2 = ada[:, 3:4], ada[:, 4:5]

    h = _rms(x, gf_ref[...]) * (1.0 + sc1) + sh1
    f = _swiglu(h.reshape(n, D_MODEL).astype(BF16), wg_ref, wu_ref, wd_ref)
    x1 = x + 0.5 * gt1 * f.reshape(kb, r, D_MODEL)
    x1_ref[...] = x1

    h = _rms(x1, gm_ref[...]) * (1.0 + sc2) + sh2
    proj = _dot(h.reshape(n, D_MODEL).astype(BF16), win_ref[...])

    cos = jnp.concatenate([cos_ref[...]] * (D_ATT // LANES), axis=-1)
    sina = jnp.concatenate([sina_ref[...]] * (D_ATT // LANES), axis=-1)
    sinb = jnp.concatenate([sinb_ref[...]] * (D_ATT // LANES), axis=-1)

    def head_norm_rope(z, g):
        ms = _dot((z * z).astype(BF16), seg_ref[...])
        zn = z * lax.rsqrt(ms + EPS) * g
        up = pltpu.roll(zn, D_ATT - ROT_DIM // 2, 1).reshape(kb, r, D_ATT)
        dn = pltpu.roll(zn, ROT_DIM // 2, 1).reshape(kb, r, D_ATT)
        return zn.reshape(kb, r, D_ATT) * cos + up * sina + dn * sinb

    q = head_norm_rope(proj[:, 0:D_ATT], gq_ref[...])
    k = head_norm_rope(proj[:, D_ATT:2 * D_ATT], gk_ref[...])
    v = proj[:, 2 * D_ATT:3 * D_ATT]
    if pair_major:
        for z, z_ref in ((q.reshape(n, D_ATT), q_ref), (k.reshape(n, D_ATT), k_ref), (v, v_ref)):
            for hp in range(D_ATT // LANES):
                z_ref[hp] = z[:, hp * LANES:(hp + 1) * LANES]
    else:
        q_ref[...] = q
        k_ref[...] = k
        v_ref[...] = v.reshape(kb, r, D_ATT)
    u = proj[:, 3 * D_ATT:]
    for sg in range(SUPER):
        u_scr[sg] = u[:, sg * LANES:(sg + 1) * LANES]
    for sg in range(SUPER):
        for s in range(CHUNK):
            u_ref[sg, :, s * LANES:(s + 1) * LANES] = u_scr[sg, pl.ds(s, n // CHUNK, stride=CHUNK), :]


def _rope_tables(pos):
    half = ROT_DIM // 2
    j = np.arange(LANES) % HEAD_DIM
    lo = jnp.asarray(j < half)
    hi = jnp.asarray((j >= half) & (j < ROT_DIM))
    inv = ROPE_THETA ** (-jnp.asarray(j % half, F32) / half)
    ang = pos.astype(F32)[:, None] * inv[None, :]
    c = jnp.cos(ang)
    s = jnp.sin(ang)
    cos = jnp.where(lo | hi, c, 1.0)
    sina = jnp.where(lo, -s, 0.0)
    sinb = jnp.where(hi, s, 0.0)
    return cos, sina, sinb


def _stage_a(x3, ada3, tables, per_seq, wts, tm):
    b3, r3, _ = x3.shape
    if per_seq:
        kb, r = min(tm // r3, b3), r3
        grid = (b3 // kb,)
        tok = lambda i: (i, 0, 0)
        tab = lambda i: (0, 0, 0)
        ada_spec = pl.BlockSpec((kb, N_ADA, D_MODEL), lambda i: (i, 0, 0))
    else:
        kb, r = 1, min(tm, r3)
        grid = (r3 // r,)
        tok = lambda i: (0, i, 0)
        tab = lambda i: (0, i, 0)
        ada_spec = pl.BlockSpec((1, N_ADA, D_MODEL), lambda i: (0, 0, 0))
    n = kb * r
    ntok = b3 * r3
    cos, sina, sinb = tables
    gf, gm, gq, gk, seg, wg, wu, wd, win = wts
    tab_spec = pl.BlockSpec((1, r, LANES), tab)
    if per_seq:
        att_spec = pl.BlockSpec((kb, r, D_ATT), tok)
        att_shape = jax.ShapeDtypeStruct((b3, r3, D_ATT), F32)
    else:
        att_spec = pl.BlockSpec((D_ATT // LANES, n, LANES), lambda i: (0, i, 0))
        att_shape = jax.ShapeDtypeStruct((D_ATT // LANES, ntok, LANES), F32)
    return pl.pallas_call(
        functools.partial(_stage_a_kernel, not per_seq),
        out_shape=(jax.ShapeDtypeStruct(x3.shape, F32), att_shape, att_shape, att_shape,
                   jax.ShapeDtypeStruct((SUPER, ntok // CHUNK, CW), F32)),
        grid=grid,
        in_specs=[pl.BlockSpec((kb, r, D_MODEL), tok), ada_spec, tab_spec, tab_spec, tab_spec,
                  _const_spec(gf.shape), _const_spec(gm.shape), _const_spec(gq.shape),
                  _const_spec(gk.shape), _const_spec(seg.shape), _const_spec(wg.shape),
                  _const_spec(wu.shape), _const_spec(wd.shape), _const_spec(win.shape)],
        out_specs=(pl.BlockSpec((kb, r, D_MODEL), tok), att_spec, att_spec, att_spec,
                   pl.BlockSpec((SUPER, n // CHUNK, CW), lambda i: (0, i, 0))),
        scratch_shapes=[pltpu.VMEM((SUPER, n, LANES), F32)],
        compiler_params=_params(("arbitrary",)),
        name="stage_a",
    )(x3, ada3, cos, sina, sinb, gf, gm, gq, gk, seg, wg, wu, wd, win)


UNITS_PER_ITER = 4


def _rows(start, size, stride):
    return pl.ds(start, size) if stride == 1 else pl.ds(start, size, stride=stride)


DIL_PARTS = 4
DIL_SHARE = len(DILATIONS) * DILATIONS[-1] // (UNITS_PER_ITER * DIL_PARTS)


def _dil_attn_part(i, part, side_work, q_ref, k_ref, v_ref, o_ref, kbuf, vbuf, acc, mrow, lrow):
    sb = q_ref.shape[1]
    tq = WIN_STEPS

    @pl.when(jnp.logical_and(part == 0, i == 0))
    def _():
        kbuf[0:sb, :] = jnp.zeros((sb, LANES), F32)
        vbuf[0:sb, :] = jnp.zeros((sb, LANES), F32)

    @pl.when(jnp.logical_and(part == 0, i > 0))
    def _():
        kbuf[0:sb, :] = kbuf[sb:2 * sb, :]
        vbuf[0:sb, :] = vbuf[sb:2 * sb, :]

    @pl.when(part == 0)
    def _():
        kbuf[sb:2 * sb, :] = k_ref[0]
        vbuf[sb:2 * sb, :] = v_ref[0]

    row = lax.broadcasted_iota(jnp.int32, (tq, 2 * tq), 0)
    col = lax.broadcasted_iota(jnp.int32, (tq, 2 * tq), 1)
    band = jnp.where(col >= row, 0.0, NEG)
    band = jnp.where(col <= row + tq, band, NEG)
    band0 = jnp.where(col >= tq, band, NEG)
    band_first = jnp.where(i == 0, band0, band)
    low = lax.broadcasted_iota(jnp.int32, (tq, LANES), 1) < HEAD_DIM
    high = jnp.logical_not(low)
    klow = lax.broadcasted_iota(jnp.int32, (2 * tq, LANES), 1) < HEAD_DIM
    ones_sel = jnp.concatenate([jnp.where(klow, 1.0, 0.0), jnp.where(klow, 0.0, 1.0)], axis=0).astype(BF16)

    def unit(d, qs, bias, mode):
        qp = (q_ref[0, _rows(qs, tq, d), :] * (HEAD_DIM ** -0.5)).astype(BF16)
        kp = kbuf[_rows(sb + qs - d * tq, 2 * tq, d), :].astype(BF16)
        vp = vbuf[_rows(sb + qs - d * tq, 2 * tq, d), :].astype(BF16)
        ps, mxs = [], []
        for sel in (low, high):
            qe = jnp.where(sel, qp, jnp.zeros_like(qp))
            s = lax.dot_general(qe, kp, _NT, preferred_element_type=F32) + bias
            mx = jnp.max(s, axis=1, keepdims=True)
            ps.append(jnp.exp(s - mx).astype(BF16))
            mxs.append(mx)
        zero = jnp.zeros_like(vp)
        vsel = jnp.concatenate([jnp.where(klow, vp, zero), jnp.where(klow, zero, vp)], axis=0)
        ol = _dot(jnp.concatenate(ps, axis=1), jnp.concatenate([vsel, ones_sel], axis=1))
        o, l = ol[:, 0:LANES], ol[:, LANES:]
        mx = jnp.where(low, mxs[0], mxs[1])
        rows = _rows(qs, tq, d)
        if mode != "init":
            m_old = mrow[rows, :]
            m_new = jnp.maximum(m_old, mx)
            a_old = jnp.exp(m_old - m_new)
            a_new = jnp.exp(mx - m_new)
            o = acc[rows, :] * a_old + o * a_new
            l = lrow[rows, :] * a_old + l * a_new
            mx = m_new
        if mode == "final":
            o_ref[0, rows, :] = o * (1.0 / l)
        else:
            acc[rows, :] = o
            mrow[rows, :] = mx
            lrow[rows, :] = l

    upi = UNITS_PER_ITER
    d16, d4 = DILATIONS[2], DILATIONS[1]
    nblk = sb // tq
    assert d16 == nblk and d4 % upi == 0

    def body16(g):
        for u in range(upi):
            unit(d16, g * upi + u, band_first, "init")

    def body4(g):
        mb = (g * upi) // d4
        bias = jnp.where(mb == 0, band_first, band)
        for u in range(upi):
            unit(d4, (g * upi) % d4 + u + d4 * tq * mb, bias, "merge")

    def body1(g):
        for u in range(upi):
            bias = jnp.where(g == 0, band_first, band) if u == 0 else band
            unit(1, (g * upi + u) * tq, bias, "final")

    counts = (nblk // upi,) * 3
    assert DIL_SHARE * DIL_PARTS == sum(counts)
    first = 0
    for body, cnt in zip((body16, body4, body1), counts):
        lo = jnp.clip(part * DIL_SHARE - first, 0, cnt)
        hi = jnp.clip((part + 1) * DIL_SHARE - first, 0, cnt)
        off = first - part * DIL_SHARE

        def wrapped(g, c, body=body, off=off):
            side_work(g + off)
            body(g)
            return c

        lax.fori_loop(lo, hi, wrapped, 0)
        first += cnt


def _sample_bias():
    w = WIN_MAX
    out = []
    i = (np.arange(N_HEADS * CHUNK) % CHUNK)[:, None]
    for d in DILATIONS:
        span = WIN_STEPS * d
        c = np.arange(w - span, w)[None, :]
        dist = w + i - c
        ok_buf = (dist % d == 0) & (dist <= span)
        cn = np.arange(LANES)[None, :] - (LANES - CHUNK)
        dn = i - cn
        ok_new = (cn >= 0) & (dn >= 0) & (dn % d == 0) & (dn <= span)
        ok = np.concatenate([ok_buf, ok_new], axis=1)
        out.append(jnp.asarray(np.where(ok, 0.0, NEG), F32))
    return out


SHIFT_HEAD_ROWS = 256
SHIFT_ROWS = -(-(D_ATT - SHIFT_HEAD_ROWS) // (8 * DIL_SHARE)) * 8


def _shift_rows(start, nrows, kc_ref, vc_ref, ko_ref, vo_ref, knt, vnt):
    w = kc_ref.shape[2]
    nq = CHUNK
    rows = pl.ds(start, nrows)
    lane = lax.broadcasted_iota(jnp.int32, (nrows, LANES), 1)
    for buf_ref, out_ref, new_ref in ((kc_ref, ko_ref, knt), (vc_ref, vo_ref, vnt)):
        rolled = pltpu.roll(buf_ref[0, rows, :], w - nq, 1)
        out_ref[0, rows, 0:w - LANES] = rolled[:, 0:w - LANES]
        out_ref[0, rows, w - LANES:] = jnp.where(lane >= LANES - nq, new_ref[rows, :],
                                                 rolled[:, w - LANES:])


def _sample_scores(q_ref, kn_ref, vn_ref, kc_ref, vc_ref, b1_ref, b2_ref, b3_ref, o_ref, knt, vnt):
    w = kc_ref.shape[2]
    nq = q_ref.shape[1]
    kc = kc_ref[0]
    vc = vc_ref[0]
    pad = jnp.zeros((LANES - nq, D_ATT), F32)
    knt[...] = jnp.concatenate([pad, kn_ref[0]], axis=0).T
    vnt[...] = jnp.concatenate([pad, vn_ref[0]], axis=0).T
    knb = knt[...].astype(BF16)
    vnb = vnt[...].astype(BF16)

    rows = N_HEADS * nq
    q = q_ref[0] * (HEAD_DIM ** -0.5)
    qt = jnp.concatenate([q] * N_HEADS, axis=0)
    rhead = lax.broadcasted_iota(jnp.int32, (rows, D_ATT), 0) // nq
    lhead = lax.broadcasted_iota(jnp.int32, (rows, D_ATT), 1) // HEAD_DIM
    qe = jnp.where(rhead == lhead, qt, 0.0).astype(BF16)
    kcb = kc.astype(BF16)
    vcb = vc.astype(BF16)
    s_buf = _dot(qe, kcb)
    s_new = _dot(qe, knb)

    outs, lses = [], []
    for d, b_ref in zip(DILATIONS, (b1_ref, b2_ref, b3_ref)):
        span = WIN_STEPS * d
        s = jnp.concatenate([s_buf[:, w - span:], s_new], axis=1) + b_ref[...]
        mx = jnp.max(s, axis=1, keepdims=True)
        p = jnp.exp(s - mx)
        l = jnp.sum(p, axis=1, keepdims=True)
        vcat = jnp.concatenate([vcb[:, w - span:], vnb], axis=1)
        o = lax.dot_general(p.astype(BF16), vcat, _NT, preferred_element_type=F32)
        outs.append(o * (1.0 / l))
        lses.append(mx + jnp.log(l))
    lmax = jnp.maximum(jnp.maximum(lses[0], lses[1]), lses[2])
    es = [jnp.exp(l - lmax) for l in lses]
    o = (es[0] * outs[0] + es[1] * outs[1] + es[2] * outs[2]) * (1.0 / (es[0] + es[1] + es[2]))
    o = jnp.where(rhead == lhead, o, 0.0)
    acc = o[0:nq]
    for h in range(1, N_HEADS):
        acc = acc + o[h * nq:(h + 1) * nq]
    o_ref[0] = acc


def _attention_kernel(nsb, q_ref, kn_ref, vn_ref, kc_ref, vc_ref, b1_ref, b2_ref, b3_ref,
                      qp_ref, kp_ref, vp_ref, o_ref, ko_ref, vo_ref, op_ref,
                      kbuf, vbuf, acc, mrow, lrow, knt, vnt):
    j = pl.program_id(0)
    _sample_scores(q_ref, kn_ref, vn_ref, kc_ref, vc_ref, b1_ref, b2_ref, b3_ref, o_ref, knt, vnt)
    shift = functools.partial(_shift_rows, kc_ref=kc_ref, vc_ref=vc_ref, ko_ref=ko_ref,
                              vo_ref=vo_ref, knt=knt, vnt=vnt)
    shift(0, SHIFT_HEAD_ROWS)

    def side_work(it):
        start = jnp.minimum(SHIFT_HEAD_ROWS + it * SHIFT_ROWS, D_ATT - SHIFT_ROWS)
        shift(pl.multiple_of(start, 8), SHIFT_ROWS)

    _dil_attn_part((j // DIL_PARTS) % nsb, j % DIL_PARTS, side_work, qp_ref, kp_ref, vp_ref, op_ref,
                   kbuf, vbuf, acc, mrow, lrow)


def _attention(q, kn, vn, kc, vc, qp, kp, vp):
    b, nq, _ = q.shape
    w = kc.shape[2]
    npair, s, _ = qp.shape
    sb = WIN_STEPS * DILATIONS[-1]
    assert w == WIN_MAX and nq == CHUNK and s % sb == 0 and DILATIONS[0] == 1
    nsb = s // sb
    ndil = npair * nsb * DIL_PARTS
    assert b <= ndil, "every sample sequence needs a grid step of the prompt attention"
    b1, b2, b3 = _sample_bias()
    seq = lambda j: (jnp.minimum(j, b - 1), 0, 0)
    new = pl.BlockSpec((1, nq, D_ATT), seq)
    buf = pl.BlockSpec((1, D_ATT, w), seq)
    blk = pl.BlockSpec((1, sb, LANES), lambda j: (j // (DIL_PARTS * nsb), (j // DIL_PARTS) % nsb, 0))
    return pl.pallas_call(
        functools.partial(_attention_kernel, nsb),
        out_shape=(jax.ShapeDtypeStruct(q.shape, F32), jax.ShapeDtypeStruct(kc.shape, F32),
                   jax.ShapeDtypeStruct(vc.shape, F32), jax.ShapeDtypeStruct(qp.shape, F32)),
        grid=(ndil,),
        in_specs=[new, new, new, buf, buf, _const_spec(b1.shape), _const_spec(b2.shape),
                  _const_spec(b3.shape), blk, blk, blk],
        out_specs=(new, buf, buf, blk),
        scratch_shapes=[pltpu.VMEM((2 * sb, LANES), F32), pltpu.VMEM((2 * sb, LANES), F32),
                        pltpu.VMEM((sb, LANES), F32), pltpu.VMEM((sb, LANES), F32),
                        pltpu.VMEM((sb, LANES), F32), pltpu.VMEM((D_ATT, LANES), F32),
                        pltpu.VMEM((D_ATT, LANES), F32)],
        compiler_params=_params(("arbitrary",)),
        name="attention",
    )(q, kn, vn, kc, vc, b1, b2, b3, qp, kp, vp)


def _split(a):
    hi = a.astype(BF16)
    return hi, (a - hi.astype(F32)).astype(BF16)


def _s5_prep_kernel(ar_ref, ai_ref, ldt_ref, bre_ref, bim_ref, cre_ref, cim_ref,
                    p_ref, qt_ref, t_ref, a_ref, pf_scr, q2_scr):
    ar, ai = ar_ref[0], ai_ref[0]
    dt = jnp.exp(ldt_ref[0])
    mag = jnp.exp(ar * dt)
    lr, li = mag * jnp.cos(ai * dt), mag * jnp.sin(ai * dt)
    den = 1.0 / (ar * ar + ai * ai)
    kr = ((lr - 1.0) * ar + li * ai) * den
    ki = (li * ar - (lr - 1.0) * ai) * den
    bre, bim = bre_ref[0], bim_ref[0]
    bbr, bbi = kr * bre - ki * bim, kr * bim + ki * bre
    cre, cim = cre_ref[0], cim_ref[0]
    pr, pi = [jnp.ones_like(lr)], [jnp.zeros_like(lr)]
    for _ in range(CHUNK):
        pr.append(pr[-1] * lr - pi[-1] * li)
        pi.append(pr[-2] * li + pi[-1] * lr)
    dd = 1.0 / (pr[CHUNK] * pr[CHUNK] + pi[CHUNK] * pi[CHUNK])
    dr, di = pr[CHUNK] * dd, -pi[CHUNK] * dd
    for s in range(CHUNK):
        n = CHUNK - 1 - s
        rows = slice(s * LANES, (s + 1) * LANES)
        pf_scr[rows, 0:SG_STATE] = bbr * pr[n] - bbi * pi[n]
        pf_scr[rows, SG_STATE:] = bbr * pi[n] + bbi * pr[n]
        n = s + 1
        re, im = cre * pr[n] - cim * pi[n], cre * pi[n] + cim * pr[n]
        qt_ref[0, rows, 0:SG_STATE] = re.astype(BF16)
        qt_ref[0, rows, SG_STATE:] = (-im).astype(BF16)
        q2_scr[rows, 0:SG_STATE] = dr * re - di * im
        q2_scr[rows, SG_STATE:] = -(dr * im + di * re)
    pf = pf_scr[...]
    p_ref[0] = pf.astype(BF16)
    a_ref[0] = jnp.concatenate([pr[CHUNK], pi[CHUNK]], axis=-1)
    ph, plo = _split(pf)
    qh, qlo = _split(q2_scr[...])
    t = (lax.dot_general(ph, qh, _NT, preferred_element_type=F32)
         + lax.dot_general(ph, qlo, _NT, preferred_element_type=F32)
         + lax.dot_general(plo, qh, _NT, preferred_element_type=F32))
    rb = lax.broadcasted_iota(jnp.int32, (CW, CW), 0) // LANES
    cb = lax.broadcasted_iota(jnp.int32, (CW, CW), 1) // LANES
    t_ref[0] = jnp.where(rb <= cb, t, 0.0).astype(BF16)


def _block_diag(a):
    g = LANES // SSM_GROUP
    _, r, c = a.shape
    a = a.reshape(SUPER, g, r, c)
    eye = jnp.eye(g, dtype=a.dtype)
    return jnp.einsum("sgrc,gh->sgrhc", a, eye).reshape(SUPER, g * r, g * c)


def _s5_prep(a_re, a_im, log_dt, b_re, b_im, c_re, c_im):
    row = lambda a: a.reshape(SUPER, 1, SG_STATE)
    ldt = jnp.broadcast_to(log_dt[:, None], (N_SSM_GROUPS, SSM_STATE))
    bre = _block_diag(jnp.swapaxes(b_re, 1, 2))
    bim = _block_diag(jnp.swapaxes(b_im, 1, 2))
    cre = _block_diag(c_re)
    cim = _block_diag(c_im)
    vec = pl.BlockSpec((1, 1, SG_STATE), lambda g: (g, 0, 0))
    mat = pl.BlockSpec((1, LANES, SG_STATE), lambda g: (g, 0, 0))
    big = pl.BlockSpec((1, CW, CW), lambda g: (g, 0, 0))
    big_shape = jax.ShapeDtypeStruct((SUPER, CW, CW), BF16)
    return pl.pallas_call(
        _s5_prep_kernel,
        out_shape=(big_shape, big_shape, big_shape, jax.ShapeDtypeStruct((SUPER, 1, CW), F32)),
        grid=(SUPER,),
        in_specs=[vec, vec, vec, mat, mat, mat, mat],
        out_specs=(big, big, big, pl.BlockSpec((1, 1, CW), lambda g: (g, 0, 0))),
        scratch_shapes=[pltpu.VMEM((CW, CW), F32), pltpu.VMEM((CW, CW), F32)],
        compiler_params=_params(("arbitrary",)),
        name="s5_prep",
    )(row(a_re), row(a_im), row(ldt), bre, bim, cre, cim)


def _gelu_tanh(y):
    return 0.5 * y * (1.0 + jnp.tanh(math.sqrt(2.0 / math.pi) * (y + 0.044715 * (y * y * y))))


def _s5_out(u, ub, hin, qt_ref, t_ref, d_ref):
    y = (_dot(ub, t_ref[0])
         + lax.dot_general(hin.astype(BF16), qt_ref[0], _NT, preferred_element_type=F32)
         + d_ref[0] * u)
    return _gelu_tanh(y)


def _s5_scan_kernel(u_ref, p_ref, qt_ref, t_ref, a_ref, d_ref, y_ref, hout_ref,
                    h_scr, gs_scr, hin_scr):
    tc = u_ref.shape[1]

    @pl.when(pl.program_id(1) == 0)
    def _():
        h_scr[...] = jnp.zeros_like(h_scr)

    u = u_ref[0]
    ub = u.astype(BF16)
    gs_scr[...] = _dot(ub, p_ref[0])
    a = a_ref[0]
    ar, ai = a[:, 0:SG_STATE], a[:, SG_STATE:]

    def step(k, carry):
        hr, hi = carry
        hin_scr[pl.ds(k, 1), 0:SG_STATE] = hr
        hin_scr[pl.ds(k, 1), SG_STATE:] = hi
        g = gs_scr[pl.ds(k, 1), :]
        return (ar * hr - ai * hi + g[:, 0:SG_STATE], ar * hi + ai * hr + g[:, SG_STATE:])

    h0 = h_scr[...]
    hr, hi = lax.fori_loop(0, tc, step, (h0[:, 0:SG_STATE], h0[:, SG_STATE:]))
    h_new = jnp.concatenate([hr, hi], axis=-1)
    h_scr[...] = h_new
    hout_ref[0] = h_new
    y_ref[0] = _s5_out(u, ub, hin_scr[...], qt_ref, t_ref, d_ref)


def _s5_scan(u, mats, dvec, tc):
    p, qt, t, a = mats
    nch = u.shape[1]
    tc = min(tc, nch)
    tile = pl.BlockSpec((1, tc, CW), lambda g, i: (g, i, 0))
    big = pl.BlockSpec((1, CW, CW), lambda g, i: (g, 0, 0))
    vec = pl.BlockSpec((1, 1, CW), lambda g, i: (g, 0, 0))
    return pl.pallas_call(
        _s5_scan_kernel,
        out_shape=(jax.ShapeDtypeStruct(u.shape, F32), jax.ShapeDtypeStruct((SUPER, 1, CW), F32)),
        grid=(SUPER, nch // tc),
        in_specs=[tile, big, big, big, vec, vec],
        out_specs=(tile, vec),
        scratch_shapes=[pltpu.VMEM((1, CW), F32), pltpu.VMEM((tc, CW), F32),
                        pltpu.VMEM((tc, CW), F32)],
        compiler_params=_params(("arbitrary", "arbitrary")),
        name="s5_scan",
    )(u, p, qt, t, a, dvec)


def _s5_step_kernel(u_ref, h0_ref, p_ref, qt_ref, t_ref, a_ref, d_ref, y_ref, hout_ref):
    u = u_ref[0]
    ub = u.astype(BF16)
    h0 = h0_ref[0]
    hr, hi = h0[:, 0:SG_STATE], h0[:, SG_STATE:]
    a = a_ref[0]
    ar, ai = a[:, 0:SG_STATE], a[:, SG_STATE:]
    gs = _dot(ub, p_ref[0])
    hout_ref[0] = jnp.concatenate([ar * hr - ai * hi + gs[:, 0:SG_STATE],
                                   ar * hi + ai * hr + gs[:, SG_STATE:]], axis=-1)
    y_ref[0] = _s5_out(u, ub, h0, qt_ref, t_ref, d_ref)


def _s5_step(u, h0, mats, dvec):
    p, qt, t, a = mats
    b = u.shape[1]
    tile = pl.BlockSpec((1, b, CW), lambda g: (g, 0, 0))
    big = pl.BlockSpec((1, CW, CW), lambda g: (g, 0, 0))
    vec = pl.BlockSpec((1, 1, CW), lambda g: (g, 0, 0))
    shape = jax.ShapeDtypeStruct(u.shape, F32)
    return pl.pallas_call(
        _s5_step_kernel,
        out_shape=(shape, shape),
        grid=(SUPER,),
        in_specs=[tile, tile, big, big, big, vec, vec],
        out_specs=(tile, tile),
        compiler_params=_params(("arbitrary",)),
        name="s5_step",
    )(u, h0, p, qt, t, a, dvec)


def _stage_b_kernel(pair_major, x_ref, ada_ref, att_ref, gs_ref, g2_ref, wglu_ref, wout_ref,
                    wg_ref, wu_ref, wd_ref, y_ref, g_scr):
    kb, r, _ = x_ref.shape
    n = kb * r
    x = x_ref[...]
    ada = ada_ref[...]
    gt2 = ada[:, 5:6]
    sh3, sc3, gt3 = ada[:, 6:7], ada[:, 7:8], ada[:, 8:9]

    if pair_major:
        o_att = jnp.concatenate([att_ref[hp] for hp in range(D_ATT // LANES)], axis=-1)
    else:
        o_att = att_ref[...].reshape(n, D_ATT)

    for sg in range(SUPER):
        for s in range(CHUNK):
            g_scr[sg, pl.ds(s, n // CHUNK, stride=CHUNK), :] = gs_ref[sg, :, s * LANES:(s + 1) * LANES]
    g = jnp.concatenate([g_scr[sg] for sg in range(SUPER)], axis=-1)
    gl = _dot(g.astype(BF16), wglu_ref[...])
    o_ssm = gl[:, 0:D_SSM] * (1.0 / (1.0 + jnp.exp(-gl[:, D_SSM:])))
    mix = _dot(jnp.concatenate([o_att, o_ssm], axis=-1).astype(BF16), wout_ref[...])
    x2 = x + gt2 * mix.reshape(kb, r, D_MODEL)

    h = _rms(x2, g2_ref[...]) * (1.0 + sc3) + sh3
    f = _swiglu(h.reshape(n, D_MODEL).astype(BF16), wg_ref, wu_ref, wd_ref)
    y_ref[...] = x2 + 0.5 * gt3 * f.reshape(kb, r, D_MODEL)


def _stage_b(x3, ada3, att, gs, per_seq, wts, tm):
    b3, r3, _ = x3.shape
    if per_seq:
        kb, r = min(tm // r3, b3), r3
        grid = (b3 // kb,)
        tok = lambda i: (i, 0, 0)
        ada_spec = pl.BlockSpec((kb, N_ADA, D_MODEL), lambda i: (i, 0, 0))
    else:
        kb, r = 1, min(tm, r3)
        grid = (r3 // r,)
        tok = lambda i: (0, i, 0)
        ada_spec = pl.BlockSpec((1, N_ADA, D_MODEL), lambda i: (0, 0, 0))
    n = kb * r
    g2, wglu, wout, wg, wu, wd = wts
    if per_seq:
        att_spec = pl.BlockSpec((kb, r, D_ATT), tok)
    else:
        att_spec = pl.BlockSpec((D_ATT // LANES, n, LANES), lambda i: (0, i, 0))
    return pl.pallas_call(
        functools.partial(_stage_b_kernel, not per_seq),
        out_shape=jax.ShapeDtypeStruct(x3.shape, F32),
        grid=grid,
        in_specs=[pl.BlockSpec((kb, r, D_MODEL), tok), ada_spec, att_spec,
                  pl.BlockSpec((SUPER, n // CHUNK, CW), lambda i: (0, i, 0)),
                  _const_spec(g2.shape), _const_spec(wglu.shape), _const_spec(wout.shape),
                  _const_spec(wg.shape), _const_spec(wu.shape), _const_spec(wd.shape)],
        out_specs=pl.BlockSpec((kb, r, D_MODEL), tok),
        scratch_shapes=[pltpu.VMEM((SUPER, n, LANES), F32)],
        compiler_params=_params(("arbitrary",)),
        name="stage_b",
    )(x3, ada3, att, gs, g2, wglu, wout, wg, wu, wd)


def _layer(x_p, x_s, c_p, c_s, cache_k, cache_v, st_re, st_im, p):
    (w_ada, b_ada, g_ffn1, w1_gate, w1_up, w1_down, g_mix, w_in, g_q, g_k,
     a_re, a_im, log_dt, b_re, b_im, c_re, c_im, d_skip, w_glu, w_out,
     g_ffn2, w2_gate, w2_up, w2_down) = p
    bp, s, _ = x_p.shape
    bs, ns, _ = x_s.shape
    assert bp == 1 and ns == CHUNK and s % (WIN_STEPS * DILATIONS[-1]) == 0
    w_buf = cache_k.shape[2]
    tm = 256

    rows = bp + bs
    pad = (-rows) % 8
    c_all = jnp.concatenate([c_p, c_s, jnp.zeros((pad, D_MODEL), F32)], axis=0)
    ada = _ada(c_all, w_ada, b_ada)
    ada_p = ada[0:bp].reshape(bp, N_ADA, D_MODEL)
    ada_s = ada[bp:rows].reshape(bs, N_ADA, D_MODEL)

    vec = lambda g: g.reshape(1, 1, -1)
    head_gain = lambda g: jnp.tile(g, N_HEADS).reshape(1, D_ATT)
    seg = jnp.asarray(np.kron(np.eye(N_HEADS), np.full((HEAD_DIM, HEAD_DIM), 1.0 / HEAD_DIM)), BF16)
    bf = lambda w: w.astype(BF16)
    wts_a = (vec(g_ffn1), vec(g_mix), head_gain(g_q), head_gain(g_k), seg,
             bf(w1_gate), bf(w1_up), bf(w1_down), bf(w_in))
    wts_b = (vec(g_ffn2), bf(w_glu), bf(w_out), bf(w2_gate), bf(w2_up), bf(w2_down))

    tab_p = tuple(t[None] for t in _rope_tables(jnp.arange(s)))
    tab_s = tuple(t[None] for t in _rope_tables(PAST_LEN + jnp.arange(ns)))

    x1_p, q_p, k_p, v_p, u_p = _stage_a(x_p, ada_p, tab_p, False, wts_a, tm)
    x1_s, q_s, k_s, v_s, u_s = _stage_a(x_s, ada_s, tab_s, True, wts_a, tm)

    o_s, kwin_s, vwin_s, o_p = _attention(q_s, k_s, v_s, cache_k, cache_v, q_p, k_p, v_p)

    mats = _s5_prep(a_re, a_im, log_dt, b_re, b_im, c_re, c_im)
    dvec = jnp.tile(d_skip.reshape(SUPER, 1, LANES), (1, 1, CHUNK))
    gs_p, hfin_p = _s5_scan(u_p, mats, dvec, 512)
    h0 = jnp.stack([jnp.concatenate([st_re[:, g * SG_STATE:(g + 1) * SG_STATE],
                                     st_im[:, g * SG_STATE:(g + 1) * SG_STATE]], axis=-1)
                    for g in range(SUPER)], axis=0)
    gs_s, hfin_s = _s5_step(u_s, h0, mats, dvec)

    y_p = _stage_b(x1_p, ada_p, o_p, gs_p, False, wts_b, tm)
    y_s = _stage_b(x1_s, ada_s, o_s, gs_s, True, wts_b, tm)

    def unpack_state(h):
        b = h.shape[1]
        re = jnp.concatenate([h[g, :, 0:SG_STATE] for g in range(SUPER)], axis=-1)
        im = jnp.concatenate([h[g, :, SG_STATE:] for g in range(SUPER)], axis=-1)
        return (re.reshape(b, N_SSM_GROUPS, SSM_STATE), im.reshape(b, N_SSM_GROUPS, SSM_STATE))

    keep = min(WIN_MAX, s)
    tail = lambda a: jnp.transpose(a[:, s - keep:], (1, 0, 2)).reshape(bp, keep, N_HEADS, HEAD_DIM)
    kwin_p, vwin_p = tail(k_p), tail(v_p)
    hre_p, him_p = unpack_state(hfin_p)
    hre_s, him_s = unpack_state(hfin_s)
    unflip = lambda a: jnp.transpose(a.reshape(bs, N_HEADS, HEAD_DIM, w_buf), (0, 3, 1, 2))
    return (y_p, y_s, kwin_p, vwin_p, hre_p, him_p, unflip(kwin_s), unflip(vwin_s), hre_s, him_s)


def kernel(x_prompt, x_sample, c_prompt, c_sample, cache_k_win, cache_v_win, state_ssm_re, state_ssm_im, w_ada, b_ada, g_ffn1, w1_gate, w1_up, w1_down, g_mix, w_in, g_q, g_k, ssm_a_re, ssm_a_im, ssm_log_dt, ssm_b_re, ssm_b_im, ssm_c_re, ssm_c_im, ssm_d, w_glu, w_out, g_ffn2, w2_gate, w2_up, w2_down):
    depth = w_ada.shape[0]
    assert depth == 1
    bs = x_sample.shape[0]
    w_buf = cache_k_win.shape[2]
    p = tuple(a[0] for a in (w_ada, b_ada, g_ffn1, w1_gate, w1_up, w1_down, g_mix, w_in, g_q, g_k,
                             ssm_a_re, ssm_a_im, ssm_log_dt, ssm_b_re, ssm_b_im, ssm_c_re, ssm_c_im,
                             ssm_d, w_glu, w_out, g_ffn2, w2_gate, w2_up, w2_down))
    flip = lambda a: jnp.transpose(a[0], (0, 2, 3, 1)).reshape(bs, D_ATT, w_buf)
    outs = _layer(x_prompt, x_sample, c_prompt, c_sample, flip(cache_k_win), flip(cache_v_win),
                  state_ssm_re[0].reshape(bs, N_SSM_GROUPS * SSM_STATE),
                  state_ssm_im[0].reshape(bs, N_SSM_GROUPS * SSM_STATE), p)
    return tuple(o[None] if i >= 2 else o for i, o in enumerate(outs))
```

```python
import functools
import math

import numpy as np
import jax
import jax.numpy as jnp
from jax import lax
from jax.experimental import pallas as pl
from jax.experimental.pallas import tpu as pltpu

F32 = jnp.float32
BF16 = jnp.bfloat16

D_MODEL = 1024
D_ATT = 512
D_SSM = 512
HEAD_DIM = 64
N_HEADS = 8
ROT_DIM = 16
ROPE_THETA = 500000.0
DILATIONS = (1, 4, 16)
WIN_STEPS = 128
WIN_MAX = 2048
PAST_LEN = 8192
SSM_GROUP = 16
N_SSM_GROUPS = 32
SSM_STATE = 64
D_FF = 2816
N_ADA = 9
EPS = 1e-6

LANES = 128
CHUNK = 8
SUPER = D_SSM // LANES
SG_STATE = (LANES // SSM_GROUP) * SSM_STATE
CW = CHUNK * LANES
NEG = -1e30
VMEM_LIMIT = 56 * 1024 * 1024

_NT = (((1,), (1,)), ((), ()))


def _params(sem, vmem=VMEM_LIMIT):
    return pltpu.CompilerParams(dimension_semantics=sem, vmem_limit_bytes=vmem)


def _const_spec(shape):
    nd = len(shape)
    return pl.BlockSpec(shape, lambda *_: (0,) * nd, pipeline_mode=pl.Buffered(1))


def _dot(a, b):
    return jnp.dot(a, b, preferred_element_type=F32)


def _rms(x, g):
    ms = jnp.mean(x * x, axis=-1, keepdims=True)
    return x * lax.rsqrt(ms + EPS) * g


def _swiglu(h, wg_ref, wu_ref, wd_ref):
    a = _dot(h, wg_ref[...])
    b = _dot(h, wu_ref[...])
    t = (a * (1.0 / (1.0 + jnp.exp(-a))) * b).astype(BF16)
    return _dot(t, wd_ref[...])


def _ada_kernel(c_ref, w_ref, b_ref, o_ref):
    c = c_ref[...]
    s = (c * (1.0 / (1.0 + jnp.exp(-c)))).astype(BF16)
    o_ref[...] = _dot(s, w_ref[...].astype(BF16)) + b_ref[...]


def _ada(c, w_ada, b_ada):
    m = c.shape[0]
    n = w_ada.shape[1]
    tn = n // N_ADA
    return pl.pallas_call(
        _ada_kernel,
        out_shape=jax.ShapeDtypeStruct((m, n), F32),
        grid=(n // tn,),
        in_specs=[pl.BlockSpec((m, D_MODEL), lambda j: (0, 0)),
                  pl.BlockSpec((D_MODEL, tn), lambda j: (0, j)),
                  pl.BlockSpec((1, tn), lambda j: (0, j))],
        out_specs=pl.BlockSpec((m, tn), lambda j: (0, j)),
        compiler_params=_params(("arbitrary",)),
        name="ada",
    )(c, w_ada, b_ada.reshape(1, n))


def _stage_a_kernel(pair_major, x_ref, ada_ref, cos_ref, sina_ref, sinb_ref, gf_ref, gm_ref, gq_ref,
                    gk_ref, seg_ref, wg_ref, wu_ref, wd_ref, win_ref,
                    x1_ref, q_ref, k_ref, v_ref, u_ref, u_scr):
    kb, r, _ = x_ref.shape
    n = kb * r
    x = x_ref[...]
    ada = ada_ref[...]
    sh1, sc1, gt1 = ada[:, 0:1], ada[:, 1:2], ada[:, 2:3]
    sh2, sc2 = ada[:, 3:4], ada[:, 4:5]

    h = _rms(x, gf_ref[...]) * (1.0 + sc1) + sh1
    f = _swiglu(h.reshape(n, D_MODEL).astype(BF16), wg_ref, wu_ref, wd_ref)
    x1 = x + 0.5 * gt1 * f.reshape(kb, r, D_MODEL)
    x1_ref[...] = x1

    h = _rms(x1, gm_ref[...]) * (1.0 + sc2) + sh2
    proj = _dot(h.reshape(n, D_MODEL).astype(BF16), win_ref[...])

    cos = jnp.concatenate([cos_ref[...]] * (D_ATT // LANES), axis=-1)
    sina = jnp.concatenate([sina_ref[...]] * (D_ATT // LANES), axis=-1)
    sinb = jnp.concatenate([sinb_ref[...]] * (D_ATT // LANES), axis=-1)

    def head_norm_rope(z, g):
        ms = _dot((z * z).astype(BF16), seg_ref[...])
        zn = z * lax.rsqrt(ms + EPS) * g
        up = pltpu.roll(zn, D_ATT - ROT_DIM // 2, 1).reshape(kb, r, D_ATT)
        dn = pltpu.roll(zn, ROT_DIM // 2, 1).reshape(kb, r, D_ATT)
        return zn.reshape(kb, r, D_ATT) * cos + up * sina + dn * sinb

    q = head_norm_rope(proj[:, 0:D_ATT], gq_ref[...])
    k = head_norm_rope(proj[:, D_ATT:2 * D_ATT], gk_ref[...])
    v = proj[:, 2 * D_ATT:3 * D_ATT]
    if pair_major:
        for z, z_ref in ((q.reshape(n, D_ATT), q_ref), (k.reshape(n, D_ATT), k_ref), (v, v_ref)):
            for hp in range(D_ATT // LANES):
                z_ref[hp] = z[:, hp * LANES:(hp + 1) * LANES]
    else:
        q_ref[...] = q
        k_ref[...] = k
        v_ref[...] = v.reshape(kb, r, D_ATT)
    u = proj[:, 3 * D_ATT:]
    for sg in range(SUPER):
        u_scr[sg] = u[:, sg * LANES:(sg + 1) * LANES]
    for sg in range(SUPER):
        for s in range(CHUNK):
            u_ref[sg, :, s * LANES:(s + 1) * LANES] = u_scr[sg, pl.ds(s, n // CHUNK, stride=CHUNK), :]


def _rope_tables(pos):
    half = ROT_DIM // 2
    inv = ROPE_THETA ** (-jnp.arange(half, dtype=F32) / half)
    ang = pos.astype(F32)[:, None] * inv[None, :]
    j = np.arange(LANES) % HEAD_DIM
    pick = (np.arange(half)[:, None] == (j % half)[None, :]).astype(np.float32)
    lo = (j < half).astype(np.float32)[None, :]
    hi = ((j >= half) & (j < ROT_DIM)).astype(np.float32)[None, :]
    spread = functools.partial(jnp.dot, precision=lax.Precision.HIGHEST)
    cos = spread(jnp.cos(ang), jnp.asarray(pick * (lo + hi))) + jnp.asarray(1.0 - lo - hi)
    sina = spread(jnp.sin(ang), jnp.asarray(-pick * lo))
    sinb = spread(jnp.sin(ang), jnp.asarray(pick * hi))
    return cos, sina, sinb


def _stage_a(x3, ada3, tables, per_seq, wts, tm):
    b3, r3, _ = x3.shape
    if per_seq:
        kb, r = min(tm // r3, b3), r3
        grid = (b3 // kb,)
        tok = lambda i: (i, 0, 0)
        tab = lambda i: (0, 0, 0)
        ada_spec = pl.BlockSpec((kb, N_ADA, D_MODEL), lambda i: (i, 0, 0))
    else:
        kb, r = 1, min(tm, r3)
        grid = (r3 // r,)
        tok = lambda i: (0, i, 0)
        tab = lambda i: (0, i, 0)
        ada_spec = pl.BlockSpec((1, N_ADA, D_MODEL), lambda i: (0, 0, 0))
    n = kb * r
    ntok = b3 * r3
    cos, sina, sinb = tables
    gf, gm, gq, gk, seg, wg, wu, wd, win = wts
    tab_spec = pl.BlockSpec((1, r, LANES), tab)
    if per_seq:
        att_spec = pl.BlockSpec((kb, r, D_ATT), tok)
        att_shape = jax.ShapeDtypeStruct((b3, r3, D_ATT), F32)
    else:
        att_spec = pl.BlockSpec((D_ATT // LANES, n, LANES), lambda i: (0, i, 0))
        att_shape = jax.ShapeDtypeStruct((D_ATT // LANES, ntok, LANES), F32)
    return pl.pallas_call(
        functools.partial(_stage_a_kernel, not per_seq),
        out_shape=(jax.ShapeDtypeStruct(x3.shape, F32), att_shape, att_shape, att_shape,
                   jax.ShapeDtypeStruct((SUPER, ntok // CHUNK, CW), F32)),
        grid=grid,
        in_specs=[pl.BlockSpec((kb, r, D_MODEL), tok), ada_spec, tab_spec, tab_spec, tab_spec,
                  _const_spec(gf.shape), _const_spec(gm.shape), _const_spec(gq.shape),
                  _const_spec(gk.shape), _const_spec(seg.shape), _const_spec(wg.shape),
                  _const_spec(wu.shape), _const_spec(wd.shape), _const_spec(win.shape)],
        out_specs=(pl.BlockSpec((kb, r, D_MODEL), tok), att_spec, att_spec, att_spec,
                   pl.BlockSpec((SUPER, n // CHUNK, CW), lambda i: (0, i, 0))),
        scratch_shapes=[pltpu.VMEM((SUPER, n, LANES), F32)],
        compiler_params=_params(("arbitrary",)),
        name="stage_a",
    )(x3, ada3, cos, sina, sinb, gf, gm, gq, gk, seg, wg, wu, wd, win)


UNITS_PER_ITER = 4


def _rows(start, size, stride):
    return pl.ds(start, size) if stride == 1 else pl.ds(start, size, stride=stride)


DIL_PARTS = 4
DIL_SHARE = len(DILATIONS) * DILATIONS[-1] // (UNITS_PER_ITER * DIL_PARTS)


def _dil_attn_part(i, part, side_work, q_ref, k_ref, v_ref, o_ref, kbuf, vbuf, acc, mrow, lrow):
    sb = q_ref.shape[1]
    tq = WIN_STEPS

    @pl.when(jnp.logical_and(part == 0, i == 0))
    def _():
        kbuf[0:sb, :] = jnp.zeros((sb, LANES), F32)
        vbuf[0:sb, :] = jnp.zeros((sb, LANES), F32)

    @pl.when(jnp.logical_and(part == 0, i > 0))
    def _():
        kbuf[0:sb, :] = kbuf[sb:2 * sb, :]
        vbuf[0:sb, :] = vbuf[sb:2 * sb, :]

    @pl.when(part == 0)
    def _():
        kbuf[sb:2 * sb, :] = k_ref[0]
        vbuf[sb:2 * sb, :] = v_ref[0]

    row = lax.broadcasted_iota(jnp.int32, (tq, 2 * tq), 0)
    col = lax.broadcasted_iota(jnp.int32, (tq, 2 * tq), 1)
    band = jnp.where(col >= row, 0.0, NEG)
    band = jnp.where(col <= row + tq, band, NEG)
    band0 = jnp.where(col >= tq, band, NEG)
    band_first = jnp.where(i == 0, band0, band)
    low = lax.broadcasted_iota(jnp.int32, (tq, LANES), 1) < HEAD_DIM
    high = jnp.logical_not(low)
    klow = lax.broadcasted_iota(jnp.int32, (2 * tq, LANES), 1) < HEAD_DIM
    ones_sel = jnp.concatenate([jnp.where(klow, 1.0, 0.0), jnp.where(klow, 0.0, 1.0)], axis=0).astype(BF16)

    def unit(d, qs, bias, mode):
        qp = (q_ref[0, _rows(qs, tq, d), :] * (HEAD_DIM ** -0.5)).astype(BF16)
        kp = kbuf[_rows(sb + qs - d * tq, 2 * tq, d), :].astype(BF16)
        vp = vbuf[_rows(sb + qs - d * tq, 2 * tq, d), :].astype(BF16)
        ps, mxs = [], []
        for sel in (low, high):
            qe = jnp.where(sel, qp, jnp.zeros_like(qp))
            s = lax.dot_general(qe, kp, _NT, preferred_element_type=F32) + bias
            mx = jnp.max(s, axis=1, keepdims=True)
            ps.append(jnp.exp(s - mx).astype(BF16))
            mxs.append(mx)
        zero = jnp.zeros_like(vp)
        vsel = jnp.concatenate([jnp.where(klow, vp, zero), jnp.where(klow, zero, vp)], axis=0)
        ol = _dot(jnp.concatenate(ps, axis=1), jnp.concatenate([vsel, ones_sel], axis=1))
        o, l = ol[:, 0:LANES], ol[:, LANES:]
        mx = jnp.where(low, mxs[0], mxs[1])
        rows = _rows(qs, tq, d)
        if mode != "init":
            m_old = mrow[rows, :]
            m_new = jnp.maximum(m_old, mx)
            a_old = jnp.exp(m_old - m_new)
            a_new = jnp.exp(mx - m_new)
            o = acc[rows, :] * a_old + o * a_new
            l = lrow[rows, :] * a_old + l * a_new
            mx = m_new
        if mode == "final":
            o_ref[0, rows, :] = o * (1.0 / l)
        else:
            acc[rows, :] = o
            mrow[rows, :] = mx
            lrow[rows, :] = l

    upi = UNITS_PER_ITER
    d16, d4 = DILATIONS[2], DILATIONS[1]
    nblk = sb // tq
    assert d16 == nblk and d4 % upi == 0

    def body16(g):
        for u in range(upi):
            unit(d16, g * upi + u, band_first, "init")

    def body4(g):
        mb = (g * upi) // d4
        bias = jnp.where(mb == 0, band_first, band)
        for u in range(upi):
            unit(d4, (g * upi) % d4 + u + d4 * tq * mb, bias, "merge")

    def body1(g):
        for u in range(upi):
            bias = jnp.where(g == 0, band_first, band) if u == 0 else band
            unit(1, (g * upi + u) * tq, bias, "final")

    counts = (nblk // upi,) * 3
    assert DIL_SHARE * DIL_PARTS == sum(counts)
    first = 0
    for body, cnt in zip((body16, body4, body1), counts):
        lo = jnp.clip(part * DIL_SHARE - first, 0, cnt)
        hi = jnp.clip((part + 1) * DIL_SHARE - first, 0, cnt)
        off = first - part * DIL_SHARE

        def wrapped(g, c, body=body, off=off):
            side_work(g + off)
            body(g)
            return c

        lax.fori_loop(lo, hi, wrapped, 0)
        first += cnt


def _sample_bias():
    w = WIN_MAX
    out = []
    i = (np.arange(N_HEADS * CHUNK) % CHUNK)[:, None]
    for d in DILATIONS:
        span = WIN_STEPS * d
        c = np.arange(w - span, w)[None, :]
        dist = w + i - c
        ok_buf = (dist % d == 0) & (dist <= span)
        cn = np.arange(LANES)[None, :] - (LANES - CHUNK)
        dn = i - cn
        ok_new = (cn >= 0) & (dn >= 0) & (dn % d == 0) & (dn <= span)
        ok = np.concatenate([ok_buf, ok_new], axis=1)
        out.append(jnp.asarray(np.where(ok, 0.0, NEG), F32))
    return out


SHIFT_HEAD_ROWS = 256
SHIFT_ROWS = -(-(D_ATT - SHIFT_HEAD_ROWS) // (8 * DIL_SHARE)) * 8


def _shift_rows(start, nrows, kc_ref, vc_ref, ko_ref, vo_ref, knt, vnt):
    w = kc_ref.shape[2]
    nq = CHUNK
    rows = pl.ds(start, nrows)
    lane = lax.broadcasted_iota(jnp.int32, (nrows, LANES), 1)
    for buf_ref, out_ref, new_ref in ((kc_ref, ko_ref, knt), (vc_ref, vo_ref, vnt)):
        rolled = pltpu.roll(buf_ref[0, rows, :], w - nq, 1)
        out_ref[0, rows, 0:w - LANES] = rolled[:, 0:w - LANES]
        out_ref[0, rows, w - LANES:] = jnp.where(lane >= LANES - nq, new_ref[rows, :],
                                                 rolled[:, w - LANES:])


def _sample_scores(q_ref, kn_ref, vn_ref, kc_ref, vc_ref, b1_ref, b2_ref, b3_ref, o_ref, knt, vnt):
    w = kc_ref.shape[2]
    nq = q_ref.shape[1]
    kc = kc_ref[0]
    vc = vc_ref[0]
    pad = jnp.zeros((LANES - nq, D_ATT), F32)
    knt[...] = jnp.concatenate([pad, kn_ref[0]], axis=0).T
    vnt[...] = jnp.concatenate([pad, vn_ref[0]], axis=0).T
    knb = knt[...].astype(BF16)
    vnb = vnt[...].astype(BF16)

    rows = N_HEADS * nq
    q = q_ref[0] * (HEAD_DIM ** -0.5)
    qt = jnp.concatenate([q] * N_HEADS, axis=0)
    rhead = lax.broadcasted_iota(jnp.int32, (rows, D_ATT), 0) // nq
    lhead = lax.broadcasted_iota(jnp.int32, (rows, D_ATT), 1) // HEAD_DIM
    qe = jnp.where(rhead == lhead, qt, 0.0).astype(BF16)
    kcb = kc.astype(BF16)
    vcb = vc.astype(BF16)
    s_buf = _dot(qe, kcb)
    s_new = _dot(qe, knb)

    outs, lses = [], []
    for d, b_ref in zip(DILATIONS, (b1_ref, b2_ref, b3_ref)):
        span = WIN_STEPS * d
        s = jnp.concatenate([s_buf[:, w - span:], s_new], axis=1) + b_ref[...]
        mx = jnp.max(s, axis=1, keepdims=True)
        p = jnp.exp(s - mx)
        l = jnp.sum(p, axis=1, keepdims=True)
        vcat = jnp.concatenate([vcb[:, w - span:], vnb], axis=1)
        o = lax.dot_general(p.astype(BF16), vcat, _NT, preferred_element_type=F32)
        outs.append(o * (1.0 / l))
        lses.append(mx + jnp.log(l))
    lmax = jnp.maximum(jnp.maximum(lses[0], lses[1]), lses[2])
    es = [jnp.exp(l - lmax) for l in lses]
    o = (es[0] * outs[0] + es[1] * outs[1] + es[2] * outs[2]) * (1.0 / (es[0] + es[1] + es[2]))
    o = jnp.where(rhead == lhead, o, 0.0)
    acc = o[0:nq]
    for h in range(1, N_HEADS):
        acc = acc + o[h * nq:(h + 1) * nq]
    o_ref[0] = acc


def _attention_kernel(nsb, q_ref, kn_ref, vn_ref, kc_ref, vc_ref, b1_ref, b2_ref, b3_ref,
                      qp_ref, kp_ref, vp_ref, o_ref, ko_ref, vo_ref, op_ref,
                      kbuf, vbuf, acc, mrow, lrow, knt, vnt):
    j = pl.program_id(0)
    _sample_scores(q_ref, kn_ref, vn_ref, kc_ref, vc_ref, b1_ref, b2_ref, b3_ref, o_ref, knt, vnt)
    shift = functools.partial(_shift_rows, kc_ref=kc_ref, vc_ref=vc_ref, ko_ref=ko_ref,
                              vo_ref=vo_ref, knt=knt, vnt=vnt)
    shift(0, SHIFT_HEAD_ROWS)

    def side_work(it):
        start = jnp.minimum(SHIFT_HEAD_ROWS + it * SHIFT_ROWS, D_ATT - SHIFT_ROWS)
        shift(pl.multiple_of(start, 8), SHIFT_ROWS)

    _dil_attn_part((j // DIL_PARTS) % nsb, j % DIL_PARTS, side_work, qp_ref, kp_ref, vp_ref, op_ref,
                   kbuf, vbuf, acc, mrow, lrow)


def _attention(q, kn, vn, kc, vc, qp, kp, vp):
    b, nq, _ = q.shape
    w = kc.shape[2]
    npair, s, _ = qp.shape
    sb = WIN_STEPS * DILATIONS[-1]
    assert w == WIN_MAX and nq == CHUNK and s % sb == 0 and DILATIONS[0] == 1
    nsb = s // sb
    ndil = npair * nsb * DIL_PARTS
    assert b <= ndil, "every sample sequence needs a grid step of the prompt attention"
    b1, b2, b3 = _sample_bias()
    seq = lambda j: (jnp.minimum(j, b - 1), 0, 0)
    new = pl.BlockSpec((1, nq, D_ATT), seq)
    buf = pl.BlockSpec((1, D_ATT, w), seq)
    blk = pl.BlockSpec((1, sb, LANES), lambda j: (j // (DIL_PARTS * nsb), (j // DIL_PARTS) % nsb, 0))
    return pl.pallas_call(
        functools.partial(_attention_kernel, nsb),
        out_shape=(jax.ShapeDtypeStruct(q.shape, F32), jax.ShapeDtypeStruct(kc.shape, F32),
                   jax.ShapeDtypeStruct(vc.shape, F32), jax.ShapeDtypeStruct(qp.shape, F32)),
        grid=(ndil,),
        in_specs=[new, new, new, buf, buf, _const_spec(b1.shape), _const_spec(b2.shape),
                  _const_spec(b3.shape), blk, blk, blk],
        out_specs=(new, buf, buf, blk),
        scratch_shapes=[pltpu.VMEM((2 * sb, LANES), F32), pltpu.VMEM((2 * sb, LANES), F32),
                        pltpu.VMEM((sb, LANES), F32), pltpu.VMEM((sb, LANES), F32),
                        pltpu.VMEM((sb, LANES), F32), pltpu.VMEM((D_ATT, LANES), F32),
                        pltpu.VMEM((D_ATT, LANES), F32)],
        compiler_params=_params(("arbitrary",)),
        name="attention",
    )(q, kn, vn, kc, vc, b1, b2, b3, qp, kp, vp)


def _split(a):
    hi = a.astype(BF16)
    return hi, (a - hi.astype(F32)).astype(BF16)


def _s5_prep_kernel(ar_ref, ai_ref, ldt_ref, bre_ref, bim_ref, cre_ref, cim_ref,
                    p_ref, qt_ref, t_ref, a_ref, pf_scr, q2_scr):
    ar, ai = ar_ref[0], ai_ref[0]
    dt = jnp.exp(ldt_ref[0])
    mag = jnp.exp(ar * dt)
    lr, li = mag * jnp.cos(ai * dt), mag * jnp.sin(ai * dt)
    den = 1.0 / (ar * ar + ai * ai)
    kr = ((lr - 1.0) * ar + li * ai) * den
    ki = (li * ar - (lr - 1.0) * ai) * den
    bre, bim = bre_ref[0], bim_ref[0]
    bbr, bbi = kr * bre - ki * bim, kr * bim + ki * bre
    cre, cim = cre_ref[0], cim_ref[0]
    pr, pi = [jnp.ones_like(lr)], [jnp.zeros_like(lr)]
    for _ in range(CHUNK):
        pr.append(pr[-1] * lr - pi[-1] * li)
        pi.append(pr[-2] * li + pi[-1] * lr)
    dd = 1.0 / (pr[CHUNK] * pr[CHUNK] + pi[CHUNK] * pi[CHUNK])
    dr, di = pr[CHUNK] * dd, -pi[CHUNK] * dd
    for s in range(CHUNK):
        n = CHUNK - 1 - s
        rows = slice(s * LANES, (s + 1) * LANES)
        pf_scr[rows, 0:SG_STATE] = bbr * pr[n] - bbi * pi[n]
        pf_scr[rows, SG_STATE:] = bbr * pi[n] + bbi * pr[n]
        n = s + 1
        re, im = cre * pr[n] - cim * pi[n], cre * pi[n] + cim * pr[n]
        qt_ref[0, rows, 0:SG_STATE] = re.astype(BF16)
        qt_ref[0, rows, SG_STATE:] = (-im).astype(BF16)
        q2_scr[rows, 0:SG_STATE] = dr * re - di * im
        q2_scr[rows, SG_STATE:] = -(dr * im + di * re)
    pf = pf_scr[...]
    p_ref[0] = pf.astype(BF16)
    a_ref[0] = jnp.concatenate([pr[CHUNK], pi[CHUNK]], axis=-1)
    ph, plo = _split(pf)
    qh, qlo = _split(q2_scr[...])
    t = (lax.dot_general(ph, qh, _NT, preferred_element_type=F32)
         + lax.dot_general(ph, qlo, _NT, preferred_element_type=F32)
         + lax.dot_general(plo, qh, _NT, preferred_element_type=F32))
    rb = lax.broadcasted_iota(jnp.int32, (CW, CW), 0) // LANES
    cb = lax.broadcasted_iota(jnp.int32, (CW, CW), 1) // LANES
    t_ref[0] = jnp.where(rb <= cb, t, 0.0).astype(BF16)


def _block_diag(a):
    g = LANES // SSM_GROUP
    _, r, c = a.shape
    a = a.reshape(SUPER, g, r, c)
    eye = jnp.eye(g, dtype=a.dtype)
    return jnp.einsum("sgrc,gh->sgrhc", a, eye).reshape(SUPER, g * r, g * c)


def _s5_prep(a_re, a_im, log_dt, b_re, b_im, c_re, c_im):
    row = lambda a: a.reshape(SUPER, 1, SG_STATE)
    ldt = jnp.broadcast_to(log_dt[:, None], (N_SSM_GROUPS, SSM_STATE))
    bre = _block_diag(jnp.swapaxes(b_re, 1, 2))
    bim = _block_diag(jnp.swapaxes(b_im, 1, 2))
    cre = _block_diag(c_re)
    cim = _block_diag(c_im)
    vec = pl.BlockSpec((1, 1, SG_STATE), lambda g: (g, 0, 0))
    mat = pl.BlockSpec((1, LANES, SG_STATE), lambda g: (g, 0, 0))
    big = pl.BlockSpec((1, CW, CW), lambda g: (g, 0, 0))
    big_shape = jax.ShapeDtypeStruct((SUPER, CW, CW), BF16)
    return pl.pallas_call(
        _s5_prep_kernel,
        out_shape=(big_shape, big_shape, big_shape, jax.ShapeDtypeStruct((SUPER, 1, CW), F32)),
        grid=(SUPER,),
        in_specs=[vec, vec, vec, mat, mat, mat, mat],
        out_specs=(big, big, big, pl.BlockSpec((1, 1, CW), lambda g: (g, 0, 0))),
        scratch_shapes=[pltpu.VMEM((CW, CW), F32), pltpu.VMEM((CW, CW), F32)],
        compiler_params=_params(("arbitrary",)),
        name="s5_prep",
    )(row(a_re), row(a_im), row(ldt), bre, bim, cre, cim)


def _gelu_tanh(y):
    return 0.5 * y * (1.0 + jnp.tanh(math.sqrt(2.0 / math.pi) * (y + 0.044715 * (y * y * y))))


def _s5_out(u, ub, hin, qt, t, d):
    y = _dot(ub, t) + lax.dot_general(hin.astype(BF16), qt, _NT, preferred_element_type=F32) + d * u
    return _gelu_tanh(y)


def _s5_scan_kernel(u_ref, p_ref, qt_ref, t_ref, a_ref, d_ref, y_ref, hout_ref,
                    h_scr, gs_scr, hin_scr):
    tc = u_ref.shape[1]

    @pl.when(pl.program_id(0) == 0)
    def _():
        h_scr[...] = jnp.zeros_like(h_scr)

    ubs = [u_ref[sg].astype(BF16) for sg in range(SUPER)]
    for sg in range(SUPER):
        gs_scr[sg] = _dot(ubs[sg], p_ref[sg])
    ar = [a_ref[sg][:, 0:SG_STATE] for sg in range(SUPER)]
    ai = [a_ref[sg][:, SG_STATE:] for sg in range(SUPER)]
    hr = [h_scr[sg][:, 0:SG_STATE] for sg in range(SUPER)]
    hi = [h_scr[sg][:, SG_STATE:] for sg in range(SUPER)]
    for k in range(tc):
        for sg in range(SUPER):
            hin_scr[sg, k:k + 1, 0:SG_STATE] = hr[sg]
            hin_scr[sg, k:k + 1, SG_STATE:] = hi[sg]
            g = gs_scr[sg, k:k + 1, :]
            hr[sg], hi[sg] = (ar[sg] * hr[sg] - ai[sg] * hi[sg] + g[:, 0:SG_STATE],
                              ar[sg] * hi[sg] + ai[sg] * hr[sg] + g[:, SG_STATE:])
    for sg in range(SUPER):
        h_new = jnp.concatenate([hr[sg], hi[sg]], axis=-1)
        h_scr[sg] = h_new
        hout_ref[sg] = h_new
        y_ref[sg] = _s5_out(u_ref[sg], ubs[sg], hin_scr[sg], qt_ref[sg], t_ref[sg], d_ref[sg])


def _s5_scan(u, mats, dvec, tc):
    p, qt, t, a = mats
    nch = u.shape[1]
    tc = min(tc, nch)
    tile = pl.BlockSpec((SUPER, tc, CW), lambda i: (0, i, 0))
    return pl.pallas_call(
        _s5_scan_kernel,
        out_shape=(jax.ShapeDtypeStruct(u.shape, F32), jax.ShapeDtypeStruct((SUPER, 1, CW), F32)),
        grid=(nch // tc,),
        in_specs=[tile, _const_spec(p.shape), _const_spec(qt.shape), _const_spec(t.shape),
                  _const_spec(a.shape), _const_spec(dvec.shape)],
        out_specs=(tile, _const_spec((SUPER, 1, CW))),
        scratch_shapes=[pltpu.VMEM((SUPER, 1, CW), F32), pltpu.VMEM((SUPER, tc, CW), F32),
                        pltpu.VMEM((SUPER, tc, CW), F32)],
        compiler_params=_params(("arbitrary",)),
        name="s5_scan",
    )(u, p, qt, t, a, dvec)


def _s5_step_kernel(u_ref, h0_ref, p_ref, qt_ref, t_ref, a_ref, d_ref, y_ref, hout_ref):
    u = u_ref[0]
    ub = u.astype(BF16)
    h0 = h0_ref[0]
    hr, hi = h0[:, 0:SG_STATE], h0[:, SG_STATE:]
    a = a_ref[0]
    ar, ai = a[:, 0:SG_STATE], a[:, SG_STATE:]
    gs = _dot(ub, p_ref[0])
    hout_ref[0] = jnp.concatenate([ar * hr - ai * hi + gs[:, 0:SG_STATE],
                                   ar * hi + ai * hr + gs[:, SG_STATE:]], axis=-1)
    y_ref[0] = _s5_out(u, ub, h0, qt_ref[0], t_ref[0], d_ref[0])


def _s5_step(u, h0, mats, dvec):
    p, qt, t, a = mats
    b = u.shape[1]
    tile = pl.BlockSpec((1, b, CW), lambda g: (g, 0, 0))
    big = pl.BlockSpec((1, CW, CW), lambda g: (g, 0, 0))
    vec = pl.BlockSpec((1, 1, CW), lambda g: (g, 0, 0))
    shape = jax.ShapeDtypeStruct(u.shape, F32)
    return pl.pallas_call(
        _s5_step_kernel,
        out_shape=(shape, shape),
        grid=(SUPER,),
        in_specs=[tile, tile, big, big, big, vec, vec],
        out_specs=(tile, tile),
        compiler_params=_params(("arbitrary",)),
        name="s5_step",
    )(u, h0, p, qt, t, a, dvec)


def _stage_b_kernel(pair_major, x_ref, ada_ref, att_ref, gs_ref, g2_ref, wglu_ref, wout_ref,
                    wg_ref, wu_ref, wd_ref, y_ref, g_scr):
    kb, r, _ = x_ref.shape
    n = kb * r
    x = x_ref[...]
    ada = ada_ref[...]
    gt2 = ada[:, 5:6]
    sh3, sc3, gt3 = ada[:, 6:7], ada[:, 7:8], ada[:, 8:9]

    if pair_major:
        o_att = jnp.concatenate([att_ref[hp] for hp in range(D_ATT // LANES)], axis=-1)
    else:
        o_att = att_ref[...].reshape(n, D_ATT)

    for sg in range(SUPER):
        for s in range(CHUNK):
            g_scr[sg, pl.ds(s, n // CHUNK, stride=CHUNK), :] = gs_ref[sg, :, s * LANES:(s + 1) * LANES]
    g = jnp.concatenate([g_scr[sg] for sg in range(SUPER)], axis=-1)
    gl = _dot(g.astype(BF16), wglu_ref[...])
    o_ssm = gl[:, 0:D_SSM] * (1.0 / (1.0 + jnp.exp(-gl[:, D_SSM:])))
    mix = _dot(jnp.concatenate([o_att, o_ssm], axis=-1).astype(BF16), wout_ref[...])
    x2 = x + gt2 * mix.reshape(kb, r, D_MODEL)

    h = _rms(x2, g2_ref[...]) * (1.0 + sc3) + sh3
    f = _swiglu(h.reshape(n, D_MODEL).astype(BF16), wg_ref, wu_ref, wd_ref)
    y_ref[...] = x2 + 0.5 * gt3 * f.reshape(kb, r, D_MODEL)


def _stage_b(x3, ada3, att, gs, per_seq, wts, tm):
    b3, r3, _ = x3.shape
    if per_seq:
        kb, r = min(tm // r3, b3), r3
        grid = (b3 // kb,)
        tok = lambda i: (i, 0, 0)
        ada_spec = pl.BlockSpec((kb, N_ADA, D_MODEL), lambda i: (i, 0, 0))
    else:
        kb, r = 1, min(tm, r3)
        grid = (r3 // r,)
        tok = lambda i: (0, i, 0)
        ada_spec = pl.BlockSpec((1, N_ADA, D_MODEL), lambda i: (0, 0, 0))
    n = kb * r
    g2, wglu, wout, wg, wu, wd = wts
    if per_seq:
        att_spec = pl.BlockSpec((kb, r, D_ATT), tok)
    else:
        att_spec = pl.BlockSpec((D_ATT // LANES, n, LANES), lambda i: (0, i, 0))
    return pl.pallas_call(
        functools.partial(_stage_b_kernel, not per_seq),
        out_shape=jax.ShapeDtypeStruct(x3.shape, F32),
        grid=grid,
        in_specs=[pl.BlockSpec((kb, r, D_MODEL), tok), ada_spec, att_spec,
                  pl.BlockSpec((SUPER, n // CHUNK, CW), lambda i: (0, i, 0)),
                  _const_spec(g2.shape), _const_spec(wglu.shape), _const_spec(wout.shape),
                  _const_spec(wg.shape), _const_spec(wu.shape), _const_spec(wd.shape)],
        out_specs=pl.BlockSpec((kb, r, D_MODEL), tok),
        scratch_shapes=[pltpu.VMEM((SUPER, n, LANES), F32)],
        compiler_params=_params(("arbitrary",)),
        name="stage_b",
    )(x3, ada3, att, gs, g2, wglu, wout, wg, wu, wd)


def _layer(x_p, x_s, c_p, c_s, cache_k, cache_v, st_re, st_im, p):
    (w_ada, b_ada, g_ffn1, w1_gate, w1_up, w1_down, g_mix, w_in, g_q, g_k,
     a_re, a_im, log_dt, b_re, b_im, c_re, c_im, d_skip, w_glu, w_out,
     g_ffn2, w2_gate, w2_up, w2_down) = p
    bp, s, _ = x_p.shape
    bs, ns, _ = x_s.shape
    assert bp == 1 and ns == CHUNK and s % (WIN_STEPS * DILATIONS[-1]) == 0
    w_buf = cache_k.shape[2]
    tm = 512

    rows = bp + bs
    pad = (-rows) % 8
    c_all = jnp.concatenate([c_p, c_s, jnp.zeros((pad, D_MODEL), F32)], axis=0)
    ada = _ada(c_all, w_ada, b_ada)
    ada_p = ada[0:bp].reshape(bp, N_ADA, D_MODEL)
    ada_s = ada[bp:rows].reshape(bs, N_ADA, D_MODEL)

    vec = lambda g: g.reshape(1, 1, -1)
    head_gain = lambda g: jnp.tile(g, N_HEADS).reshape(1, D_ATT)
    seg = jnp.asarray(np.kron(np.eye(N_HEADS), np.full((HEAD_DIM, HEAD_DIM), 1.0 / HEAD_DIM)), BF16)
    bf = lambda w: w.astype(BF16)
    wts_a = (vec(g_ffn1), vec(g_mix), head_gain(g_q), head_gain(g_k), seg,
             bf(w1_gate), bf(w1_up), bf(w1_down), bf(w_in))
    wts_b = (vec(g_ffn2), bf(w_glu), bf(w_out), bf(w2_gate), bf(w2_up), bf(w2_down))

    tab_p = tuple(t[None] for t in _rope_tables(jnp.arange(s)))
    tab_s = tuple(t[None] for t in _rope_tables(PAST_LEN + jnp.arange(ns)))

    x1_p, q_p, k_p, v_p, u_p = _stage_a(x_p, ada_p, tab_p, False, wts_a, tm)
    x1_s, q_s, k_s, v_s, u_s = _stage_a(x_s, ada_s, tab_s, True, wts_a, tm)

    o_s, kwin_s, vwin_s, o_p = _attention(q_s, k_s, v_s, cache_k, cache_v, q_p, k_p, v_p)

    mats = _s5_prep(a_re, a_im, log_dt, b_re, b_im, c_re, c_im)
    dvec = jnp.tile(d_skip.reshape(SUPER, 1, LANES), (1, 1, CHUNK))
    gs_p, hfin_p = _s5_scan(u_p, mats, dvec, 256)
    h0 = jnp.stack([jnp.concatenate([st_re[:, g * SG_STATE:(g + 1) * SG_STATE],
                                     st_im[:, g * SG_STATE:(g + 1) * SG_STATE]], axis=-1)
                    for g in range(SUPER)], axis=0)
    gs_s, hfin_s = _s5_step(u_s, h0, mats, dvec)

    y_p = _stage_b(x1_p, ada_p, o_p, gs_p, False, wts_b, tm)
    y_s = _stage_b(x1_s, ada_s, o_s, gs_s, True, wts_b, tm)

    def unpack_state(h):
        b = h.shape[1]
        re = jnp.concatenate([h[g, :, 0:SG_STATE] for g in range(SUPER)], axis=-1)
        im = jnp.concatenate([h[g, :, SG_STATE:] for g in range(SUPER)], axis=-1)
        return (re.reshape(b, N_SSM_GROUPS, SSM_STATE), im.reshape(b, N_SSM_GROUPS, SSM_STATE))

    keep = min(WIN_MAX, s)
    tail = lambda a: jnp.transpose(a[:, s - keep:], (1, 0, 2)).reshape(bp, keep, N_HEADS, HEAD_DIM)
    kwin_p, vwin_p = tail(k_p), tail(v_p)
    hre_p, him_p = unpack_state(hfin_p)
    hre_s, him_s = unpack_state(hfin_s)
    unflip = lambda a: jnp.transpose(a.reshape(bs, N_HEADS, HEAD_DIM, w_buf), (0, 3, 1, 2))
    return (y_p, y_s, kwin_p, vwin_p, hre_p, him_p, unflip(kwin_s), unflip(vwin_s), hre_s, him_s)


def kernel(x_prompt, x_sample, c_prompt, c_sample, cache_k_win, cache_v_win, state_ssm_re, state_ssm_im, w_ada, b_ada, g_ffn1, w1_gate, w1_up, w1_down, g_mix, w_in, g_q, g_k, ssm_a_re, ssm_a_im, ssm_log_dt, ssm_b_re, ssm_b_im, ssm_c_re, ssm_c_im, ssm_d, w_glu, w_out, g_ffn2, w2_gate, w2_up, w2_down):
    depth = w_ada.shape[0]
    assert depth == 1
    bs = x_sample.shape[0]
    w_buf = cache_k_win.shape[2]
    p = tuple(a[0] for a in (w_ada, b_ada, g_ffn1, w1_gate, w1_up, w1_down, g_mix, w_in, g_q, g_k,
                             ssm_a_re, ssm_a_im, ssm_log_dt, ssm_b_re, ssm_b_im, ssm_c_re, ssm_c_im,
                             ssm_d, w_glu, w_out, g_ffn2, w2_gate, w2_up, w2_down))
    flip = lambda a: jnp.transpose(a[0], (0, 2, 3, 1)).reshape(bs, D_ATT, w_buf)
    outs = _layer(x_prompt, x_sample, c_prompt, c_sample, flip(cache_k_win), flip(cache_v_win),
                  state_ssm_re[0].reshape(bs, N_SSM_GROUPS * SSM_STATE),
                  state_ssm_im[0].reshape(bs, N_SSM_GROUPS * SSM_STATE), p)
    return tuple(o[None] if i >= 2 else o for i, o in enumerate(outs))
```

```python
import functools
import math

import numpy as np
import jax
import jax.numpy as jnp
from jax import lax
from jax.experimental import pallas as pl
from jax.experimental.pallas import tpu as pltpu

F32 = jnp.float32
BF16 = jnp.bfloat16

D_MODEL = 1024
D_ATT = 512
D_SSM = 512
HEAD_DIM = 64
N_HEADS = 8
ROT_DIM = 16
ROPE_THETA = 500000.0
DILATIONS = (1, 4, 16)
WIN_STEPS = 128
WIN_MAX = 2048
PAST_LEN = 8192
SSM_GROUP = 16
N_SSM_GROUPS = 32
SSM_STATE = 64
D_FF = 2816
N_ADA = 9
EPS = 1e-6

LANES = 128
CHUNK = 8
SUPER = D_SSM // LANES
SG_STATE = (LANES // SSM_GROUP) * SSM_STATE
CW = CHUNK * LANES
NEG = -1e30
VMEM_LIMIT = 56 * 1024 * 1024
RING_VMEM_LIMIT = 60 * 1024 * 1024

_NT = (((1,), (1,)), ((), ()))


def _params(sem, vmem=VMEM_LIMIT):
    return pltpu.CompilerParams(dimension_semantics=sem, vmem_limit_bytes=vmem)


def _const_spec(shape):
    nd = len(shape)
    return pl.BlockSpec(shape, lambda *_: (0,) * nd, pipeline_mode=pl.Buffered(1))


def _dot(a, b):
    return jnp.dot(a, b, preferred_element_type=F32)


def _rms(x, g):
    ms = jnp.mean(x * x, axis=-1, keepdims=True)
    return x * lax.rsqrt(ms + EPS) * g


def _swiglu(h, wg_ref, wu_ref, wd_ref):
    a = _dot(h, wg_ref[...])
    b = _dot(h, wu_ref[...])
    t = (a * (1.0 / (1.0 + jnp.exp(-a))) * b).astype(BF16)
    return _dot(t, wd_ref[...])


def _ada_kernel(c_ref, w_ref, b_ref, o_ref):
    c = c_ref[...]
    s = (c * (1.0 / (1.0 + jnp.exp(-c)))).astype(BF16)
    o_ref[...] = _dot(s, w_ref[...].astype(BF16)) + b_ref[...]


def _ada(c, w_ada, b_ada):
    m = c.shape[0]
    n = w_ada.shape[1]
    tn = n // N_ADA
    return pl.pallas_call(
        _ada_kernel,
        out_shape=jax.ShapeDtypeStruct((m, n), F32),
        grid=(n // tn,),
        in_specs=[pl.BlockSpec((m, D_MODEL), lambda j: (0, 0)),
                  pl.BlockSpec((D_MODEL, tn), lambda j: (0, j)),
                  pl.BlockSpec((1, tn), lambda j: (0, j))],
        out_specs=pl.BlockSpec((m, tn), lambda j: (0, j)),
        compiler_params=_params(("arbitrary",)),
        name="ada",
    )(c, w_ada, b_ada.reshape(1, n))


def _stage_a_kernel(pair_major, x_ref, ada_ref, base_ref, off_ref, lanes_ref, gf_ref, gm_ref, gq_ref,
                    gk_ref, seg_ref, wg_ref, wu_ref, wd_ref, win_ref,
                    x1_ref, q_ref, k_ref, v_ref, u_ref, u_scr):
    kb, r, _ = x_ref.shape
    n = kb * r
    x = x_ref[...]
    ada = ada_ref[...]
    sh1, sc1, gt1 = ada[:, 0:1], ada[:, 1:2], ada[:, 2:3]
    sh2, sc2 = ada[:, 3:4], ada[:, 4:5]

    h = _rms(x, gf_ref[...]) * (1.0 + sc1) + sh1
    f = _swiglu(h.reshape(n, D_MODEL).astype(BF16), wg_ref, wu_ref, wd_ref)
    x1 = x + 0.5 * gt1 * f.reshape(kb, r, D_MODEL)
    x1_ref[...] = x1

    h = _rms(x1, gm_ref[...]) * (1.0 + sc2) + sh2
    proj = _dot(h.reshape(n, D_MODEL).astype(BF16), win_ref[...])

    ca, sa = base_ref[0, 0:1, :], base_ref[0, 1:2, :]
    cb, sb = off_ref[0], off_ref[1]
    c, s = ca * cb - sa * sb, sa * cb + ca * sb
    rot, neg_lo, pos_hi = lanes_ref[0], lanes_ref[1], lanes_ref[2]
    tile4 = lambda t: jnp.concatenate([t] * (D_ATT // LANES), axis=-1)[None]
    cos, sina, sinb = tile4(c * rot + (1.0 - rot)), tile4(s * neg_lo), tile4(s * pos_hi)

    def head_norm_rope(z, g):
        ms = _dot((z * z).astype(BF16), seg_ref[...])
        zn = z * lax.rsqrt(ms + EPS) * g
        up = pltpu.roll(zn, D_ATT - ROT_DIM // 2, 1).reshape(kb, r, D_ATT)
        dn = pltpu.roll(zn, ROT_DIM // 2, 1).reshape(kb, r, D_ATT)
        return zn.reshape(kb, r, D_ATT) * cos + up * sina + dn * sinb

    q = head_norm_rope(proj[:, 0:D_ATT], gq_ref[...])
    k = head_norm_rope(proj[:, D_ATT:2 * D_ATT], gk_ref[...])
    v = proj[:, 2 * D_ATT:3 * D_ATT]
    if pair_major:
        for z, z_ref in ((q.reshape(n, D_ATT), q_ref), (k.reshape(n, D_ATT), k_ref), (v, v_ref)):
            for hp in range(D_ATT // LANES):
                z_ref[hp] = z[:, hp * LANES:(hp + 1) * LANES]
    else:
        q_ref[...] = q
        k_ref[...] = k
        v_ref[...] = v.reshape(kb, r, D_ATT)
    u = proj[:, 3 * D_ATT:]
    for sg in range(SUPER):
        u_scr[sg] = u[:, sg * LANES:(sg + 1) * LANES]
    for sg in range(SUPER):
        for s in range(CHUNK):
            u_ref[sg, :, s * LANES:(s + 1) * LANES] = u_scr[sg, pl.ds(s, n // CHUNK, stride=CHUNK), :]


def _rope_inputs(start, n_tiles, tile):
    half = ROT_DIM // 2
    j = np.arange(LANES) % HEAD_DIM
    inv = ROPE_THETA ** (-jnp.asarray(j % half, F32) / half)
    a = (start + tile * jnp.arange(n_tiles)).astype(F32)[:, None] * inv[None, :]
    b = jnp.arange(tile).astype(F32)[:, None] * inv[None, :]
    lo = (j < half).astype(np.float32)
    hi = ((j >= half) & (j < ROT_DIM)).astype(np.float32)
    lanes = jnp.asarray(np.stack([lo + hi, -lo, hi])[:, None, :])
    return jnp.stack([jnp.cos(a), jnp.sin(a)], axis=1), jnp.stack([jnp.cos(b), jnp.sin(b)]), lanes


def _stage_a(x3, ada3, pos0, per_seq, wts, tm):
    b3, r3, _ = x3.shape
    if per_seq:
        kb, r = min(tm // r3, b3), r3
        grid = (b3 // kb,)
        tok = lambda i: (i, 0, 0)
        tab = lambda i: (0, 0, 0)
        ada_spec = pl.BlockSpec((kb, N_ADA, D_MODEL), lambda i: (i, 0, 0))
        base, off, lanes = _rope_inputs(pos0, 1, r)
    else:
        kb, r = 1, min(tm, r3)
        grid = (r3 // r,)
        tok = lambda i: (0, i, 0)
        tab = lambda i: (i, 0, 0)
        ada_spec = pl.BlockSpec((1, N_ADA, D_MODEL), lambda i: (0, 0, 0))
        base, off, lanes = _rope_inputs(pos0, r3 // r, r)
    n = kb * r
    ntok = b3 * r3
    gf, gm, gq, gk, seg, wg, wu, wd, win = wts
    base_spec = pl.BlockSpec((1, 2, LANES), tab)
    if per_seq:
        att_spec = pl.BlockSpec((kb, r, D_ATT), tok)
        att_shape = jax.ShapeDtypeStruct((b3, r3, D_ATT), F32)
    else:
        att_spec = pl.BlockSpec((D_ATT // LANES, n, LANES), lambda i: (0, i, 0))
        att_shape = jax.ShapeDtypeStruct((D_ATT // LANES, ntok, LANES), F32)
    return pl.pallas_call(
        functools.partial(_stage_a_kernel, not per_seq),
        out_shape=(jax.ShapeDtypeStruct(x3.shape, F32), att_shape, att_shape, att_shape,
                   jax.ShapeDtypeStruct((SUPER, ntok // CHUNK, CW), F32)),
        grid=grid,
        in_specs=[pl.BlockSpec((kb, r, D_MODEL), tok), ada_spec, base_spec,
                  _const_spec(off.shape), _const_spec(lanes.shape),
                  _const_spec(gf.shape), _const_spec(gm.shape), _const_spec(gq.shape),
                  _const_spec(gk.shape), _const_spec(seg.shape), _const_spec(wg.shape),
                  _const_spec(wu.shape), _const_spec(wd.shape), _const_spec(win.shape)],
        out_specs=(pl.BlockSpec((kb, r, D_MODEL), tok), att_spec, att_spec, att_spec,
                   pl.BlockSpec((SUPER, n // CHUNK, CW), lambda i: (0, i, 0))),
        scratch_shapes=[pltpu.VMEM((SUPER, n, LANES), F32)],
        compiler_params=_params(("arbitrary",)),
        name="stage_a",
    )(x3, ada3, base, off, lanes, gf, gm, gq, gk, seg, wg, wu, wd, win)


UNITS_PER_ITER = 4


def _rows(start, size, stride):
    return pl.ds(start, size) if stride == 1 else pl.ds(start, size, stride=stride)


DIL_PARTS = 4
DIL_SHARE = len(DILATIONS) * DILATIONS[-1] // (UNITS_PER_ITER * DIL_PARTS)


def _dil_attn_part(i, part, side_work, q_ref, k_ref, v_ref, o_ref, kbuf, vbuf, acc, mrow, lrow):
    sb = q_ref.shape[1]
    tq = WIN_STEPS

    @pl.when(jnp.logical_and(part == 0, i == 0))
    def _():
        kbuf[0:sb, :] = jnp.zeros((sb, LANES), F32)
        vbuf[0:sb, :] = jnp.zeros((sb, LANES), F32)

    @pl.when(jnp.logical_and(part == 0, i > 0))
    def _():
        kbuf[0:sb, :] = kbuf[sb:2 * sb, :]
        vbuf[0:sb, :] = vbuf[sb:2 * sb, :]

    @pl.when(part == 0)
    def _():
        kbuf[sb:2 * sb, :] = k_ref[0]
        vbuf[sb:2 * sb, :] = v_ref[0]

    row = lax.broadcasted_iota(jnp.int32, (tq, 2 * tq), 0)
    col = lax.broadcasted_iota(jnp.int32, (tq, 2 * tq), 1)
    band = jnp.where(col >= row, 0.0, NEG)
    band = jnp.where(col <= row + tq, band, NEG)
    band0 = jnp.where(col >= tq, band, NEG)
    band_first = jnp.where(i == 0, band0, band)
    low = lax.broadcasted_iota(jnp.int32, (tq, LANES), 1) < HEAD_DIM
    high = jnp.logical_not(low)
    klow = lax.broadcasted_iota(jnp.int32, (2 * tq, LANES), 1) < HEAD_DIM
    ones_sel = jnp.concatenate([jnp.where(klow, 1.0, 0.0), jnp.where(klow, 0.0, 1.0)], axis=0).astype(BF16)

    def unit(d, qs, bias, mode):
        qp = (q_ref[0, _rows(qs, tq, d), :] * (HEAD_DIM ** -0.5)).astype(BF16)
        kp = kbuf[_rows(sb + qs - d * tq, 2 * tq, d), :].astype(BF16)
        vp = vbuf[_rows(sb + qs - d * tq, 2 * tq, d), :].astype(BF16)
        ps, mxs = [], []
        for sel in (low, high):
            qe = jnp.where(sel, qp, jnp.zeros_like(qp))
            s = lax.dot_general(qe, kp, _NT, preferred_element_type=F32) + bias
            mx = jnp.max(s, axis=1, keepdims=True)
            ps.append(jnp.exp(s - mx).astype(BF16))
            mxs.append(mx)
        zero = jnp.zeros_like(vp)
        vsel = jnp.concatenate([jnp.where(klow, vp, zero), jnp.where(klow, zero, vp)], axis=0)
        ol = _dot(jnp.concatenate(ps, axis=1), jnp.concatenate([vsel, ones_sel], axis=1))
        o, l = ol[:, 0:LANES], ol[:, LANES:]
        mx = jnp.where(low, mxs[0], mxs[1])
        rows = _rows(qs, tq, d)
        if mode != "init":
            m_old = mrow[rows, :]
            m_new = jnp.maximum(m_old, mx)
            a_old = jnp.exp(m_old - m_new)
            a_new = jnp.exp(mx - m_new)
            o = acc[rows, :] * a_old + o * a_new
            l = lrow[rows, :] * a_old + l * a_new
            mx = m_new
        if mode == "final":
            o_ref[0, rows, :] = o * (1.0 / l)
        else:
            acc[rows, :] = o
            mrow[rows, :] = mx
            lrow[rows, :] = l

    upi = UNITS_PER_ITER
    d16, d4 = DILATIONS[2], DILATIONS[1]
    nblk = sb // tq
    assert d16 == nblk and d4 % upi == 0

    def body16(g):
        for u in range(upi):
            unit(d16, g * upi + u, band_first, "init")

    def body4(g):
        mb = (g * upi) // d4
        bias = jnp.where(mb == 0, band_first, band)
        for u in range(upi):
            unit(d4, (g * upi) % d4 + u + d4 * tq * mb, bias, "merge")

    def body1(g):
        for u in range(upi):
            bias = jnp.where(g == 0, band_first, band) if u == 0 else band
            unit(1, (g * upi + u) * tq, bias, "final")

    counts = (nblk // upi,) * 3
    assert DIL_SHARE * DIL_PARTS == sum(counts)
    first = 0
    for body, cnt in zip((body16, body4, body1), counts):
        lo = jnp.clip(part * DIL_SHARE - first, 0, cnt)
        hi = jnp.clip((part + 1) * DIL_SHARE - first, 0, cnt)
        off = first - part * DIL_SHARE

        def wrapped(g, c, body=body, off=off):
            side_work(g + off)
            body(g)
            return c

        lax.fori_loop(lo, hi, wrapped, 0)
        first += cnt


def _sample_bias():
    w = WIN_MAX
    out = []
    i = (np.arange(N_HEADS * CHUNK) % CHUNK)[:, None]
    for d in DILATIONS:
        span = WIN_STEPS * d
        c = np.arange(w - span, w)[None, :]
        dist = w + i - c
        ok_buf = (dist % d == 0) & (dist <= span)
        cn = np.arange(LANES)[None, :] - (LANES - CHUNK)
        dn = i - cn
        ok_new = (cn >= 0) & (dn >= 0) & (dn % d == 0) & (dn <= span)
        ok = np.concatenate([ok_buf, ok_new], axis=1)
        out.append(jnp.asarray(np.where(ok, 0.0, NEG), F32))
    return out


SHIFT_HEAD_ROWS = 256
SHIFT_ROWS = -(-(D_ATT - SHIFT_HEAD_ROWS) // (8 * DIL_SHARE)) * 8


def _shift_rows(start, nrows, kc_ref, vc_ref, ko_ref, vo_ref, knt, vnt):
    w = kc_ref.shape[1]
    nq = CHUNK
    rows = pl.ds(start, nrows)
    lane = lax.broadcasted_iota(jnp.int32, (nrows, LANES), 1)
    for buf_ref, out_ref, new_ref in ((kc_ref, ko_ref, knt), (vc_ref, vo_ref, vnt)):
        rolled = pltpu.roll(buf_ref[rows, :], w - nq, 1)
        out_ref[0, rows, 0:w - LANES] = rolled[:, 0:w - LANES]
        out_ref[0, rows, w - LANES:] = jnp.where(lane >= LANES - nq, new_ref[rows, :],
                                                 rolled[:, w - LANES:])


def _sample_scores(q_ref, kn_ref, vn_ref, kc_ref, vc_ref, b1_ref, b2_ref, b3_ref, o_ref, knt, vnt):
    w = kc_ref.shape[1]
    nq = q_ref.shape[1]
    kc = kc_ref[...]
    vc = vc_ref[...]
    pad = jnp.zeros((LANES - nq, D_ATT), F32)
    knt[...] = jnp.concatenate([pad, kn_ref[0]], axis=0).T
    vnt[...] = jnp.concatenate([pad, vn_ref[0]], axis=0).T
    knb = knt[...].astype(BF16)
    vnb = vnt[...].astype(BF16)

    rows = N_HEADS * nq
    q = q_ref[0] * (HEAD_DIM ** -0.5)
    qt = jnp.concatenate([q] * N_HEADS, axis=0)
    rhead = lax.broadcasted_iota(jnp.int32, (rows, D_ATT), 0) // nq
    lhead = lax.broadcasted_iota(jnp.int32, (rows, D_ATT), 1) // HEAD_DIM
    qe = jnp.where(rhead == lhead, qt, 0.0).astype(BF16)
    kcb = kc.astype(BF16)
    vcb = vc.astype(BF16)
    s_buf = _dot(qe, kcb)
    s_new = _dot(qe, knb)

    outs, lses = [], []
    for d, b_ref in zip(DILATIONS, (b1_ref, b2_ref, b3_ref)):
        span = WIN_STEPS * d
        s = jnp.concatenate([s_buf[:, w - span:], s_new], axis=1) + b_ref[...]
        mx = jnp.max(s, axis=1, keepdims=True)
        p = jnp.exp(s - mx)
        l = jnp.sum(p, axis=1, keepdims=True)
        vcat = jnp.concatenate([vcb[:, w - span:], vnb], axis=1)
        o = lax.dot_general(p.astype(BF16), vcat, _NT, preferred_element_type=F32)
        outs.append(o * (1.0 / l))
        lses.append(mx + jnp.log(l))
    lmax = jnp.maximum(jnp.maximum(lses[0], lses[1]), lses[2])
    es = [jnp.exp(l - lmax) for l in lses]
    o = (es[0] * outs[0] + es[1] * outs[1] + es[2] * outs[2]) * (1.0 / (es[0] + es[1] + es[2]))
    o = jnp.where(rhead == lhead, o, 0.0)
    acc = o[0:nq]
    for h in range(1, N_HEADS):
        acc = acc + o[h * nq:(h + 1) * nq]
    o_ref[0] = acc


RING = 3


def _attention_kernel(nseq, nsteps, nsb, q_ref, kn_ref, vn_ref, kc_hbm, vc_hbm, b1_ref, b2_ref, b3_ref,
                      qp_ref, kp_ref, vp_ref, o_ref, ko_ref, vo_ref, op_ref,
                      kbuf, vbuf, acc, mrow, lrow, knt, vnt, kring, vring, sems):
    j = pl.program_id(0)

    def fetch(step):
        seq = jnp.minimum(step, nseq - 1)
        slot = step % RING
        return (pltpu.make_async_copy(kc_hbm.at[seq], kring.at[slot], sems.at[0, slot]),
                pltpu.make_async_copy(vc_hbm.at[seq], vring.at[slot], sems.at[1, slot]))

    @pl.when(j == 0)
    def _():
        for step in range(min(RING - 1, nsteps)):
            for cp in fetch(step):
                cp.start()

    @pl.when(j + RING - 1 < nsteps)
    def _():
        for cp in fetch(j + RING - 1):
            cp.start()

    for cp in fetch(j):
        cp.wait()
    kc_ref, vc_ref = kring.at[j % RING], vring.at[j % RING]
    _sample_scores(q_ref, kn_ref, vn_ref, kc_ref, vc_ref, b1_ref, b2_ref, b3_ref, o_ref, knt, vnt)
    shift = functools.partial(_shift_rows, kc_ref=kc_ref, vc_ref=vc_ref, ko_ref=ko_ref,
                              vo_ref=vo_ref, knt=knt, vnt=vnt)
    shift(0, SHIFT_HEAD_ROWS)

    def side_work(it):
        start = jnp.minimum(SHIFT_HEAD_ROWS + it * SHIFT_ROWS, D_ATT - SHIFT_ROWS)
        shift(pl.multiple_of(start, 8), SHIFT_ROWS)

    _dil_attn_part((j // DIL_PARTS) % nsb, j % DIL_PARTS, side_work, qp_ref, kp_ref, vp_ref, op_ref,
                   kbuf, vbuf, acc, mrow, lrow)


def _attention(q, kn, vn, kc, vc, qp, kp, vp):
    b, nq, _ = q.shape
    w = kc.shape[2]
    npair, s, _ = qp.shape
    sb = WIN_STEPS * DILATIONS[-1]
    assert w == WIN_MAX and nq == CHUNK and s % sb == 0 and DILATIONS[0] == 1
    nsb = s // sb
    ndil = npair * nsb * DIL_PARTS
    assert b <= ndil, "every sample sequence needs a grid step of the prompt attention"
    b1, b2, b3 = _sample_bias()
    seq = lambda j: (jnp.minimum(j, b - 1), 0, 0)
    new = pl.BlockSpec((1, nq, D_ATT), seq)
    buf = pl.BlockSpec((1, D_ATT, w), seq)
    blk = pl.BlockSpec((1, sb, LANES), lambda j: (j // (DIL_PARTS * nsb), (j // DIL_PARTS) % nsb, 0))
    hbm = pl.BlockSpec(memory_space=pl.ANY)
    return pl.pallas_call(
        functools.partial(_attention_kernel, b, ndil, nsb),
        out_shape=(jax.ShapeDtypeStruct(q.shape, F32), jax.ShapeDtypeStruct(kc.shape, F32),
                   jax.ShapeDtypeStruct(vc.shape, F32), jax.ShapeDtypeStruct(qp.shape, F32)),
        grid=(ndil,),
        in_specs=[new, new, new, hbm, hbm, _const_spec(b1.shape), _const_spec(b2.shape),
                  _const_spec(b3.shape), blk, blk, blk],
        out_specs=(new, buf, buf, blk),
        scratch_shapes=[pltpu.VMEM((2 * sb, LANES), F32), pltpu.VMEM((2 * sb, LANES), F32),
                        pltpu.VMEM((sb, LANES), F32), pltpu.VMEM((sb, LANES), F32),
                        pltpu.VMEM((sb, LANES), F32), pltpu.VMEM((D_ATT, LANES), F32),
                        pltpu.VMEM((D_ATT, LANES), F32), pltpu.VMEM((RING, D_ATT, w), F32),
                        pltpu.VMEM((RING, D_ATT, w), F32), pltpu.SemaphoreType.DMA((2, RING))],
        compiler_params=_params(("arbitrary",), RING_VMEM_LIMIT),
        name="attention",
    )(q, kn, vn, kc, vc, b1, b2, b3, qp, kp, vp)


def _split(a):
    hi = a.astype(BF16)
    return hi, (a - hi.astype(F32)).astype(BF16)


def _s5_prep_kernel(ar_ref, ai_ref, ldt_ref, bre_ref, bim_ref, cre_ref, cim_ref,
                    p_ref, qt_ref, t_ref, a_ref, pf_scr, q2_scr):
    ar, ai = ar_ref[0], ai_ref[0]
    dt = jnp.exp(ldt_ref[0])
    mag = jnp.exp(ar * dt)
    lr, li = mag * jnp.cos(ai * dt), mag * jnp.sin(ai * dt)
    den = 1.0 / (ar * ar + ai * ai)
    kr = ((lr - 1.0) * ar + li * ai) * den
    ki = (li * ar - (lr - 1.0) * ai) * den
    bre, bim = bre_ref[0], bim_ref[0]
    bbr, bbi = kr * bre - ki * bim, kr * bim + ki * bre
    cre, cim = cre_ref[0], cim_ref[0]
    pr, pi = [jnp.ones_like(lr)], [jnp.zeros_like(lr)]
    for _ in range(CHUNK):
        pr.append(pr[-1] * lr - pi[-1] * li)
        pi.append(pr[-2] * li + pi[-1] * lr)
    dd = 1.0 / (pr[CHUNK] * pr[CHUNK] + pi[CHUNK] * pi[CHUNK])
    dr, di = pr[CHUNK] * dd, -pi[CHUNK] * dd
    for s in range(CHUNK):
        n = CHUNK - 1 - s
        rows = slice(s * LANES, (s + 1) * LANES)
        pf_scr[rows, 0:SG_STATE] = bbr * pr[n] - bbi * pi[n]
        pf_scr[rows, SG_STATE:] = bbr * pi[n] + bbi * pr[n]
        n = s + 1
        re, im = cre * pr[n] - cim * pi[n], cre * pi[n] + cim * pr[n]
        qt_ref[0, rows, 0:SG_STATE] = re.astype(BF16)
        qt_ref[0, rows, SG_STATE:] = (-im).astype(BF16)
        q2_scr[rows, 0:SG_STATE] = dr * re - di * im
        q2_scr[rows, SG_STATE:] = -(dr * im + di * re)
    pf = pf_scr[...]
    p_ref[0] = pf.astype(BF16)
    a_ref[0] = jnp.concatenate([pr[CHUNK], pi[CHUNK]], axis=-1)
    ph, plo = _split(pf)
    qh, qlo = _split(q2_scr[...])
    t = (lax.dot_general(ph, qh, _NT, preferred_element_type=F32)
         + lax.dot_general(ph, qlo, _NT, preferred_element_type=F32)
         + lax.dot_general(plo, qh, _NT, preferred_element_type=F32))
    rb = lax.broadcasted_iota(jnp.int32, (CW, CW), 0) // LANES
    cb = lax.broadcasted_iota(jnp.int32, (CW, CW), 1) // LANES
    t_ref[0] = jnp.where(rb <= cb, t, 0.0).astype(BF16)


def _block_diag(a):
    g = LANES // SSM_GROUP
    _, r, c = a.shape
    a = a.reshape(SUPER, g, r, c)
    eye = jnp.eye(g, dtype=a.dtype)
    return jnp.einsum("sgrc,gh->sgrhc", a, eye).reshape(SUPER, g * r, g * c)


def _s5_prep(a_re, a_im, log_dt, b_re, b_im, c_re, c_im):
    row = lambda a: a.reshape(SUPER, 1, SG_STATE)
    ldt = jnp.broadcast_to(log_dt[:, None], (N_SSM_GROUPS, SSM_STATE))
    bre = _block_diag(jnp.swapaxes(b_re, 1, 2))
    bim = _block_diag(jnp.swapaxes(b_im, 1, 2))
    cre = _block_diag(c_re)
    cim = _block_diag(c_im)
    vec = pl.BlockSpec((1, 1, SG_STATE), lambda g: (g, 0, 0))
    mat = pl.BlockSpec((1, LANES, SG_STATE), lambda g: (g, 0, 0))
    big = pl.BlockSpec((1, CW, CW), lambda g: (g, 0, 0))
    big_shape = jax.ShapeDtypeStruct((SUPER, CW, CW), BF16)
    return pl.pallas_call(
        _s5_prep_kernel,
        out_shape=(big_shape, big_shape, big_shape, jax.ShapeDtypeStruct((SUPER, 1, CW), F32)),
        grid=(SUPER,),
        in_specs=[vec, vec, vec, mat, mat, mat, mat],
        out_specs=(big, big, big, pl.BlockSpec((1, 1, CW), lambda g: (g, 0, 0))),
        scratch_shapes=[pltpu.VMEM((CW, CW), F32), pltpu.VMEM((CW, CW), F32)],
        compiler_params=_params(("arbitrary",)),
        name="s5_prep",
    )(row(a_re), row(a_im), row(ldt), bre, bim, cre, cim)


def _gelu_tanh(y):
    return 0.5 * y * (1.0 + jnp.tanh(math.sqrt(2.0 / math.pi) * (y + 0.044715 * (y * y * y))))


def _s5_out(u, ub, hin, qt, t, d):
    y = _dot(ub, t) + lax.dot_general(hin.astype(BF16), qt, _NT, preferred_element_type=F32) + d * u
    return _gelu_tanh(y)


def _s5_scan_kernel(u_ref, p_ref, qt_ref, t_ref, a_ref, d_ref, y_ref, hout_ref,
                    h_scr, gs_scr, hin_scr):
    tc = u_ref.shape[1]

    @pl.when(pl.program_id(0) == 0)
    def _():
        h_scr[...] = jnp.zeros_like(h_scr)

    ubs = [u_ref[sg].astype(BF16) for sg in range(SUPER)]
    for sg in range(SUPER):
        gs_scr[sg] = _dot(ubs[sg], p_ref[sg])
    ar = [a_ref[sg][:, 0:SG_STATE] for sg in range(SUPER)]
    ai = [a_ref[sg][:, SG_STATE:] for sg in range(SUPER)]
    hr = [h_scr[sg][:, 0:SG_STATE] for sg in range(SUPER)]
    hi = [h_scr[sg][:, SG_STATE:] for sg in range(SUPER)]
    for k in range(tc):
        for sg in range(SUPER):
            hin_scr[sg, k:k + 1, 0:SG_STATE] = hr[sg]
            hin_scr[sg, k:k + 1, SG_STATE:] = hi[sg]
            g = gs_scr[sg, k:k + 1, :]
            hr[sg], hi[sg] = (ar[sg] * hr[sg] - ai[sg] * hi[sg] + g[:, 0:SG_STATE],
                              ar[sg] * hi[sg] + ai[sg] * hr[sg] + g[:, SG_STATE:])
    for sg in range(SUPER):
        h_new = jnp.concatenate([hr[sg], hi[sg]], axis=-1)
        h_scr[sg] = h_new
        hout_ref[sg] = h_new
        y_ref[sg] = _s5_out(u_ref[sg], ubs[sg], hin_scr[sg], qt_ref[sg], t_ref[sg], d_ref[sg])


def _s5_scan(u, mats, dvec, tc):
    p, qt, t, a = mats
    nch = u.shape[1]
    tc = min(tc, nch)
    tile = pl.BlockSpec((SUPER, tc, CW), lambda i: (0, i, 0))
    return pl.pallas_call(
        _s5_scan_kernel,
        out_shape=(jax.ShapeDtypeStruct(u.shape, F32), jax.ShapeDtypeStruct((SUPER, 1, CW), F32)),
        grid=(nch // tc,),
        in_specs=[tile, _const_spec(p.shape), _const_spec(qt.shape), _const_spec(t.shape),
                  _const_spec(a.shape), _const_spec(dvec.shape)],
        out_specs=(tile, _const_spec((SUPER, 1, CW))),
        scratch_shapes=[pltpu.VMEM((SUPER, 1, CW), F32), pltpu.VMEM((SUPER, tc, CW), F32),
                        pltpu.VMEM((SUPER, tc, CW), F32)],
        compiler_params=_params(("arbitrary",)),
        name="s5_scan",
    )(u, p, qt, t, a, dvec)


def _s5_step_kernel(u_ref, h0_ref, p_ref, qt_ref, t_ref, a_ref, d_ref, y_ref, hout_ref):
    u = u_ref[0]
    ub = u.astype(BF16)
    h0 = h0_ref[0]
    hr, hi = h0[:, 0:SG_STATE], h0[:, SG_STATE:]
    a = a_ref[0]
    ar, ai = a[:, 0:SG_STATE], a[:, SG_STATE:]
    gs = _dot(ub, p_ref[0])
    hout_ref[0] = jnp.concatenate([ar * hr - ai * hi + gs[:, 0:SG_STATE],
                                   ar * hi + ai * hr + gs[:, SG_STATE:]], axis=-1)
    y_ref[0] = _s5_out(u, ub, h0, qt_ref[0], t_ref[0], d_ref[0])


def _s5_step(u, h0, mats, dvec):
    p, qt, t, a = mats
    b = u.shape[1]
    tile = pl.BlockSpec((1, b, CW), lambda g: (g, 0, 0))
    big = pl.BlockSpec((1, CW, CW), lambda g: (g, 0, 0))
    vec = pl.BlockSpec((1, 1, CW), lambda g: (g, 0, 0))
    shape = jax.ShapeDtypeStruct(u.shape, F32)
    return pl.pallas_call(
        _s5_step_kernel,
        out_shape=(shape, shape),
        grid=(SUPER,),
        in_specs=[tile, tile, big, big, big, vec, vec],
        out_specs=(tile, tile),
        compiler_params=_params(("arbitrary",)),
        name="s5_step",
    )(u, h0, p, qt, t, a, dvec)


def _stage_b_kernel(pair_major, x_ref, ada_ref, att_ref, gs_ref, g2_ref, wglu_ref, wout_ref,
                    wg_ref, wu_ref, wd_ref, y_ref, g_scr):
    kb, r, _ = x_ref.shape
    n = kb * r
    x = x_ref[...]
    ada = ada_ref[...]
    gt2 = ada[:, 5:6]
    sh3, sc3, gt3 = ada[:, 6:7], ada[:, 7:8], ada[:, 8:9]

    if pair_major:
        o_att = jnp.concatenate([att_ref[hp] for hp in range(D_ATT // LANES)], axis=-1)
    else:
        o_att = att_ref[...].reshape(n, D_ATT)

    for sg in range(SUPER):
        for s in range(CHUNK):
            g_scr[sg, pl.ds(s, n // CHUNK, stride=CHUNK), :] = gs_ref[sg, :, s * LANES:(s + 1) * LANES]
    g = jnp.concatenate([g_scr[sg] for sg in range(SUPER)], axis=-1)
    gl = _dot(g.astype(BF16), wglu_ref[...])
    o_ssm = gl[:, 0:D_SSM] * (1.0 / (1.0 + jnp.exp(-gl[:, D_SSM:])))
    mix = _dot(jnp.concatenate([o_att, o_ssm], axis=-1).astype(BF16), wout_ref[...])
    x2 = x + gt2 * mix.reshape(kb, r, D_MODEL)

    h = _rms(x2, g2_ref[...]) * (1.0 + sc3) + sh3
    f = _swiglu(h.reshape(n, D_MODEL).astype(BF16), wg_ref, wu_ref, wd_ref)
    y_ref[...] = x2 + 0.5 * gt3 * f.reshape(kb, r, D_MODEL)


def _stage_b(x3, ada3, att, gs, per_seq, wts, tm):
    b3, r3, _ = x3.shape
    if per_seq:
        kb, r = min(tm // r3, b3), r3
        grid = (b3 // kb,)
        tok = lambda i: (i, 0, 0)
        ada_spec = pl.BlockSpec((kb, N_ADA, D_MODEL), lambda i: (i, 0, 0))
    else:
        kb, r = 1, min(tm, r3)
        grid = (r3 // r,)
        tok = lambda i: (0, i, 0)
        ada_spec = pl.BlockSpec((1, N_ADA, D_MODEL), lambda i: (0, 0, 0))
    n = kb * r
    g2, wglu, wout, wg, wu, wd = wts
    if per_seq:
        att_spec = pl.BlockSpec((kb, r, D_ATT), tok)
    else:
        att_spec = pl.BlockSpec((D_ATT // LANES, n, LANES), lambda i: (0, i, 0))
    return pl.pallas_call(
        functools.partial(_stage_b_kernel, not per_seq),
        out_shape=jax.ShapeDtypeStruct(x3.shape, F32),
        grid=grid,
        in_specs=[pl.BlockSpec((kb, r, D_MODEL), tok), ada_spec, att_spec,
                  pl.BlockSpec((SUPER, n // CHUNK, CW), lambda i: (0, i, 0)),
                  _const_spec(g2.shape), _const_spec(wglu.shape), _const_spec(wout.shape),
                  _const_spec(wg.shape), _const_spec(wu.shape), _const_spec(wd.shape)],
        out_specs=pl.BlockSpec((kb, r, D_MODEL), tok),
        scratch_shapes=[pltpu.VMEM((SUPER, n, LANES), F32)],
        compiler_params=_params(("arbitrary",)),
        name="stage_b",
    )(x3, ada3, att, gs, g2, wglu, wout, wg, wu, wd)


def _layer(x_p, x_s, c_p, c_s, cache_k, cache_v, st_re, st_im, p):
    (w_ada, b_ada, g_ffn1, w1_gate, w1_up, w1_down, g_mix, w_in, g_q, g_k,
     a_re, a_im, log_dt, b_re, b_im, c_re, c_im, d_skip, w_glu, w_out,
     g_ffn2, w2_gate, w2_up, w2_down) = p
    bp, s, _ = x_p.shape
    bs, ns, _ = x_s.shape
    assert bp == 1 and ns == CHUNK and s % (WIN_STEPS * DILATIONS[-1]) == 0
    w_buf = cache_k.shape[2]
    tm = 512

    rows = bp + bs
    pad = (-rows) % 8
    c_all = jnp.concatenate([c_p, c_s, jnp.zeros((pad, D_MODEL), F32)], axis=0)
    ada = _ada(c_all, w_ada, b_ada)
    ada_p = ada[0:bp].reshape(bp, N_ADA, D_MODEL)
    ada_s = ada[bp:rows].reshape(bs, N_ADA, D_MODEL)

    vec = lambda g: g.reshape(1, 1, -1)
    head_gain = lambda g: jnp.tile(g, N_HEADS).reshape(1, D_ATT)
    seg = jnp.asarray(np.kron(np.eye(N_HEADS), np.full((HEAD_DIM, HEAD_DIM), 1.0 / HEAD_DIM)), BF16)
    bf = lambda w: w.astype(BF16)
    wts_a = (vec(g_ffn1), vec(g_mix), head_gain(g_q), head_gain(g_k), seg,
             bf(w1_gate), bf(w1_up), bf(w1_down), bf(w_in))
    wts_b = (vec(g_ffn2), bf(w_glu), bf(w_out), bf(w2_gate), bf(w2_up), bf(w2_down))

    x1_p, q_p, k_p, v_p, u_p = _stage_a(x_p, ada_p, 0, False, wts_a, tm)
    x1_s, q_s, k_s, v_s, u_s = _stage_a(x_s, ada_s, PAST_LEN, True, wts_a, tm)

    o_s, kwin_s, vwin_s, o_p = _attention(q_s, k_s, v_s, cache_k, cache_v, q_p, k_p, v_p)

    mats = _s5_prep(a_re, a_im, log_dt, b_re, b_im, c_re, c_im)
    dvec = jnp.tile(d_skip.reshape(SUPER, 1, LANES), (1, 1, CHUNK))
    gs_p, hfin_p = _s5_scan(u_p, mats, dvec, 256)
    h0 = jnp.stack([jnp.concatenate([st_re[:, g * SG_STATE:(g + 1) * SG_STATE],
                                     st_im[:, g * SG_STATE:(g + 1) * SG_STATE]], axis=-1)
                    for g in range(SUPER)], axis=0)
    gs_s, hfin_s = _s5_step(u_s, h0, mats, dvec)

    y_p = _stage_b(x1_p, ada_p, o_p, gs_p, False, wts_b, tm)
    y_s = _stage_b(x1_s, ada_s, o_s, gs_s, True, wts_b, tm)

    def unpack_state(h):
        b = h.shape[1]
        re = jnp.concatenate([h[g, :, 0:SG_STATE] for g in range(SUPER)], axis=-1)
        im = jnp.concatenate([h[g, :, SG_STATE:] for g in range(SUPER)], axis=-1)
        return (re.reshape(b, N_SSM_GROUPS, SSM_STATE), im.reshape(b, N_SSM_GROUPS, SSM_STATE))

    keep = min(WIN_MAX, s)
    tail = lambda a: jnp.transpose(a[:, s - keep:], (1, 0, 2)).reshape(bp, keep, N_HEADS, HEAD_DIM)
    kwin_p, vwin_p = tail(k_p), tail(v_p)
    hre_p, him_p = unpack_state(hfin_p)
    hre_s, him_s = unpack_state(hfin_s)
    unflip = lambda a: jnp.transpose(a.reshape(bs, N_HEADS, HEAD_DIM, w_buf), (0, 3, 1, 2))
    return (y_p, y_s, kwin_p, vwin_p, hre_p, him_p, unflip(kwin_s), unflip(vwin_s), hre_s, him_s)


def kernel(x_prompt, x_sample, c_prompt, c_sample, cache_k_win, cache_v_win, state_ssm_re, state_ssm_im, w_ada, b_ada, g_ffn1, w1_gate, w1_up, w1_down, g_mix, w_in, g_q, g_k, ssm_a_re, ssm_a_im, ssm_log_dt, ssm_b_re, ssm_b_im, ssm_c_re, ssm_c_im, ssm_d, w_glu, w_out, g_ffn2, w2_gate, w2_up, w2_down):
    depth = w_ada.shape[0]
    assert depth == 1
    bs = x_sample.shape[0]
    w_buf = cache_k_win.shape[2]
    p = tuple(a[0] for a in (w_ada, b_ada, g_ffn1, w1_gate, w1_up, w1_down, g_mix, w_in, g_q, g_k,
                             ssm_a_re, ssm_a_im, ssm_log_dt, ssm_b_re, ssm_b_im, ssm_c_re, ssm_c_im,
                             ssm_d, w_glu, w_out, g_ffn2, w2_gate, w2_up, w2_down))
    flip = lambda a: jnp.transpose(a[0], (0, 2, 3, 1)).reshape(bs, D_ATT, w_buf)
    outs = _layer(x_prompt, x_sample, c_prompt, c_sample, flip(cache_k_win), flip(cache_v_win),
                  state_ssm_re[0].reshape(bs, N_SSM_GROUPS * SSM_STATE),
                  state_ssm_im[0].reshape(bs, N_SSM_GROUPS * SSM_STATE), p)
    return tuple(o[None] if i >= 2 else o for i, o in enumerate(outs))
```

```python
import functools
import math

import numpy as np
import jax
import jax.numpy as jnp
from jax import lax
from jax.experimental import pallas as pl
from jax.experimental.pallas import tpu as pltpu

F32 = jnp.float32
BF16 = jnp.bfloat16

D_MODEL = 1024
D_ATT = 512
D_SSM = 512
HEAD_DIM = 64
N_HEADS = 8
ROT_DIM = 16
ROPE_THETA = 500000.0
DILATIONS = (1, 4, 16)
WIN_STEPS = 128
WIN_MAX = 2048
PAST_LEN = 8192
SSM_GROUP = 16
N_SSM_GROUPS = 32
SSM_STATE = 64
D_FF = 2816
N_ADA = 9
EPS = 1e-6

LANES = 128
CHUNK = 8
SUPER = D_SSM // LANES
SG_STATE = (LANES // SSM_GROUP) * SSM_STATE
CW = CHUNK * LANES
NEG = -1e30
VMEM_LIMIT = 56 * 1024 * 1024
RING_VMEM_LIMIT = 60 * 1024 * 1024

_NT = (((1,), (1,)), ((), ()))


def _params(sem, vmem=VMEM_LIMIT):
    return pltpu.CompilerParams(dimension_semantics=sem, vmem_limit_bytes=vmem)


def _const_spec(shape):
    nd = len(shape)
    return pl.BlockSpec(shape, lambda *_: (0,) * nd, pipeline_mode=pl.Buffered(1))


def _dot(a, b):
    return jnp.dot(a, b, preferred_element_type=F32)


def _rms(x, g):
    ms = jnp.mean(x * x, axis=-1, keepdims=True)
    return x * lax.rsqrt(ms + EPS) * g


def _swiglu(h, wg_ref, wu_ref, wd_ref):
    a = _dot(h, wg_ref[...])
    b = _dot(h, wu_ref[...])
    t = (a * (1.0 / (1.0 + jnp.exp(-a))) * b).astype(BF16)
    return _dot(t, wd_ref[...])


def _ada_kernel(c_ref, w_ref, b_ref, o_ref):
    c = c_ref[...]
    s = (c * (1.0 / (1.0 + jnp.exp(-c)))).astype(BF16)
    o_ref[...] = _dot(s, w_ref[...].astype(BF16)) + b_ref[...]


def _ada(c, w_ada, b_ada):
    m = c.shape[0]
    n = w_ada.shape[1]
    tn = n // N_ADA
    return pl.pallas_call(
        _ada_kernel,
        out_shape=jax.ShapeDtypeStruct((m, n), F32),
        grid=(n // tn,),
        in_specs=[pl.BlockSpec((m, D_MODEL), lambda j: (0, 0)),
                  pl.BlockSpec((D_MODEL, tn), lambda j: (0, j)),
                  pl.BlockSpec((1, tn), lambda j: (0, j))],
        out_specs=pl.BlockSpec((m, tn), lambda j: (0, j)),
        compiler_params=_params(("arbitrary",)),
        name="ada",
    )(c, w_ada, b_ada.reshape(1, n))


def _stage_a_kernel(pair_major, x_ref, ada_ref, base_ref, off_ref, lanes_ref, gf_ref, gm_ref, gq_ref,
                    gk_ref, seg_ref, wg_ref, wu_ref, wd_ref, win_ref,
                    x1_ref, q_ref, k_ref, v_ref, u_ref, u_scr):
    kb, r, _ = x_ref.shape
    n = kb * r
    x = x_ref[...]
    ada = ada_ref[...]
    sh1, sc1, gt1 = ada[:, 0:1], ada[:, 1:2], ada[:, 2:3]
    sh2, sc2 = ada[:, 3:4], ada[:, 4:5]

    h = _rms(x, gf_ref[...]) * (1.0 + sc1) + sh1
    f = _swiglu(h.reshape(n, D_MODEL).astype(BF16), wg_ref, wu_ref, wd_ref)
    x1 = x + 0.5 * gt1 * f.reshape(kb, r, D_MODEL)
    x1_ref[...] = x1

    h = _rms(x1, gm_ref[...]) * (1.0 + sc2) + sh2
    proj = _dot(h.reshape(n, D_MODEL).astype(BF16), win_ref[...])

    ca, sa = base_ref[0, 0:1, :], base_ref[0, 1:2, :]
    cb, sb = off_ref[0], off_ref[1]
    c, s = ca * cb - sa * sb, sa * cb + ca * sb
    rot, neg_lo, pos_hi = lanes_ref[0], lanes_ref[1], lanes_ref[2]
    tile4 = lambda t: jnp.concatenate([t] * (D_ATT // LANES), axis=-1)[None]
    cos, sina, sinb = tile4(c * rot + (1.0 - rot)), tile4(s * neg_lo), tile4(s * pos_hi)

    def head_norm_rope(z, g):
        ms = _dot((z * z).astype(BF16), seg_ref[...])
        zn = z * lax.rsqrt(ms + EPS) * g
        up = pltpu.roll(zn, D_ATT - ROT_DIM // 2, 1).reshape(kb, r, D_ATT)
        dn = pltpu.roll(zn, ROT_DIM // 2, 1).reshape(kb, r, D_ATT)
        return zn.reshape(kb, r, D_ATT) * cos + up * sina + dn * sinb

    q = head_norm_rope(proj[:, 0:D_ATT], gq_ref[...])
    k = head_norm_rope(proj[:, D_ATT:2 * D_ATT], gk_ref[...])
    v = proj[:, 2 * D_ATT:3 * D_ATT]
    if pair_major:
        for z, z_ref in ((q.reshape(n, D_ATT), q_ref), (k.reshape(n, D_ATT), k_ref), (v, v_ref)):
            for hp in range(D_ATT // LANES):
                z_ref[hp] = z[:, hp * LANES:(hp + 1) * LANES]
    else:
        q_ref[...] = q
        k_ref[...] = k
        v_ref[...] = v.reshape(kb, r, D_ATT)
    u = proj[:, 3 * D_ATT:]
    for sg in range(SUPER):
        u_scr[sg] = u[:, sg * LANES:(sg + 1) * LANES]
    for sg in range(SUPER):
        for s in range(CHUNK):
            u_ref[sg, :, s * LANES:(s + 1) * LANES] = u_scr[sg, pl.ds(s, n // CHUNK, stride=CHUNK), :]


def _rope_inputs(start, n_tiles, tile):
    half = ROT_DIM // 2
    j = np.arange(LANES) % HEAD_DIM
    inv = ROPE_THETA ** (-(j % half).astype(np.float64) / half)
    a = (start + tile * np.arange(n_tiles))[:, None] * inv[None, :]
    b = np.arange(tile)[:, None] * inv[None, :]
    lo = (j < half).astype(np.float32)
    hi = ((j >= half) & (j < ROT_DIM)).astype(np.float32)
    lanes = np.stack([lo + hi, -lo, hi])[:, None, :]
    base = np.stack([np.cos(a), np.sin(a)], axis=1).astype(np.float32)
    off = np.stack([np.cos(b), np.sin(b)]).astype(np.float32)
    return jnp.asarray(base), jnp.asarray(off), jnp.asarray(lanes)


def _stage_a(x3, ada3, pos0, per_seq, wts, tm):
    b3, r3, _ = x3.shape
    if per_seq:
        kb, r = min(tm // r3, b3), r3
        grid = (b3 // kb,)
        tok = lambda i: (i, 0, 0)
        tab = lambda i: (0, 0, 0)
        ada_spec = pl.BlockSpec((kb, N_ADA, D_MODEL), lambda i: (i, 0, 0))
        base, off, lanes = _rope_inputs(pos0, 1, r)
    else:
        kb, r = 1, min(tm, r3)
        grid = (r3 // r,)
        tok = lambda i: (0, i, 0)
        tab = lambda i: (i, 0, 0)
        ada_spec = pl.BlockSpec((1, N_ADA, D_MODEL), lambda i: (0, 0, 0))
        base, off, lanes = _rope_inputs(pos0, r3 // r, r)
    n = kb * r
    ntok = b3 * r3
    gf, gm, gq, gk, seg, wg, wu, wd, win = wts
    base_spec = pl.BlockSpec((1, 2, LANES), tab)
    if per_seq:
        att_spec = pl.BlockSpec((kb, r, D_ATT), tok)
        att_shape = jax.ShapeDtypeStruct((b3, r3, D_ATT), F32)
    else:
        att_spec = pl.BlockSpec((D_ATT // LANES, n, LANES), lambda i: (0, i, 0))
        att_shape = jax.ShapeDtypeStruct((D_ATT // LANES, ntok, LANES), F32)
    return pl.pallas_call(
        functools.partial(_stage_a_kernel, not per_seq),
        out_shape=(jax.ShapeDtypeStruct(x3.shape, F32), att_shape, att_shape, att_shape,
                   jax.ShapeDtypeStruct((SUPER, ntok // CHUNK, CW), F32)),
        grid=grid,
        in_specs=[pl.BlockSpec((kb, r, D_MODEL), tok), ada_spec, base_spec,
                  _const_spec(off.shape), _const_spec(lanes.shape),
                  _const_spec(gf.shape), _const_spec(gm.shape), _const_spec(gq.shape),
                  _const_spec(gk.shape), _const_spec(seg.shape), _const_spec(wg.shape),
                  _const_spec(wu.shape), _const_spec(wd.shape), _const_spec(win.shape)],
        out_specs=(pl.BlockSpec((kb, r, D_MODEL), tok), att_spec, att_spec, att_spec,
                   pl.BlockSpec((SUPER, n // CHUNK, CW), lambda i: (0, i, 0))),
        scratch_shapes=[pltpu.VMEM((SUPER, n, LANES), F32)],
        compiler_params=_params(("arbitrary",)),
        name="stage_a",
    )(x3, ada3, base, off, lanes, gf, gm, gq, gk, seg, wg, wu, wd, win)


UNITS_PER_ITER = 4


def _rows(start, size, stride):
    return pl.ds(start, size) if stride == 1 else pl.ds(start, size, stride=stride)


DIL_PARTS = 4
DIL_SHARE = len(DILATIONS) * DILATIONS[-1] // (UNITS_PER_ITER * DIL_PARTS)


def _dil_attn_part(i, part, side_work, q_ref, k_ref, v_ref, o_ref, kbuf, vbuf, acc, mrow, lrow):
    sb = q_ref.shape[1]
    tq = WIN_STEPS

    @pl.when(jnp.logical_and(part == 0, i == 0))
    def _():
        kbuf[0:sb, :] = jnp.zeros((sb, LANES), F32)
        vbuf[0:sb, :] = jnp.zeros((sb, LANES), F32)

    @pl.when(jnp.logical_and(part == 0, i > 0))
    def _():
        kbuf[0:sb, :] = kbuf[sb:2 * sb, :]
        vbuf[0:sb, :] = vbuf[sb:2 * sb, :]

    @pl.when(part == 0)
    def _():
        kbuf[sb:2 * sb, :] = k_ref[0]
        vbuf[sb:2 * sb, :] = v_ref[0]

    row = lax.broadcasted_iota(jnp.int32, (tq, 2 * tq), 0)
    col = lax.broadcasted_iota(jnp.int32, (tq, 2 * tq), 1)
    band = jnp.where(col >= row, 0.0, NEG)
    band = jnp.where(col <= row + tq, band, NEG)
    band0 = jnp.where(col >= tq, band, NEG)
    band_first = jnp.where(i == 0, band0, band)
    low = lax.broadcasted_iota(jnp.int32, (tq, LANES), 1) < HEAD_DIM
    high = jnp.logical_not(low)
    klow = lax.broadcasted_iota(jnp.int32, (2 * tq, LANES), 1) < HEAD_DIM
    ones_sel = jnp.concatenate([jnp.where(klow, 1.0, 0.0), jnp.where(klow, 0.0, 1.0)], axis=0).astype(BF16)

    def unit(d, qs, bias, mode):
        qp = (q_ref[0, _rows(qs, tq, d), :] * (HEAD_DIM ** -0.5)).astype(BF16)
        kp = kbuf[_rows(sb + qs - d * tq, 2 * tq, d), :].astype(BF16)
        vp = vbuf[_rows(sb + qs - d * tq, 2 * tq, d), :].astype(BF16)
        ps, mxs = [], []
        for sel in (low, high):
            qe = jnp.where(sel, qp, jnp.zeros_like(qp))
            s = lax.dot_general(qe, kp, _NT, preferred_element_type=F32) + bias
            mx = jnp.max(s, axis=1, keepdims=True)
            ps.append(jnp.exp(s - mx).astype(BF16))
            mxs.append(mx)
        zero = jnp.zeros_like(vp)
        vsel = jnp.concatenate([jnp.where(klow, vp, zero), jnp.where(klow, zero, vp)], axis=0)
        ol = _dot(jnp.concatenate(ps, axis=1), jnp.concatenate([vsel, ones_sel], axis=1))
        o, l = ol[:, 0:LANES], ol[:, LANES:]
        mx = jnp.where(low, mxs[0], mxs[1])
        rows = _rows(qs, tq, d)
        if mode != "init":
            m_old = mrow[rows, :]
            m_new = jnp.maximum(m_old, mx)
            a_old = jnp.exp(m_old - m_new)
            a_new = jnp.exp(mx - m_new)
            o = acc[rows, :] * a_old + o * a_new
            l = lrow[rows, :] * a_old + l * a_new
            mx = m_new
        if mode == "final":
            o_ref[0, rows, :] = o * (1.0 / l)
        else:
            acc[rows, :] = o
            mrow[rows, :] = mx
            lrow[rows, :] = l

    upi = UNITS_PER_ITER
    d16, d4 = DILATIONS[2], DILATIONS[1]
    nblk = sb // tq
    assert d16 == nblk and d4 % upi == 0

    def body16(g):
        for u in range(upi):
            unit(d16, g * upi + u, band_first, "init")

    def body4(g):
        mb = (g * upi) // d4
        bias = jnp.where(mb == 0, band_first, band)
        for u in range(upi):
            unit(d4, (g * upi) % d4 + u + d4 * tq * mb, bias, "merge")

    def body1(g):
        for u in range(upi):
            bias = jnp.where(g == 0, band_first, band) if u == 0 else band
            unit(1, (g * upi + u) * tq, bias, "final")

    counts = (nblk // upi,) * 3
    assert DIL_SHARE * DIL_PARTS == sum(counts)
    first = 0
    for body, cnt in zip((body16, body4, body1), counts):
        lo = jnp.clip(part * DIL_SHARE - first, 0, cnt)
        hi = jnp.clip((part + 1) * DIL_SHARE - first, 0, cnt)
        off = first - part * DIL_SHARE

        def wrapped(g, c, body=body, off=off):
            side_work(g + off)
            body(g)
            return c

        lax.fori_loop(lo, hi, wrapped, 0)
        first += cnt


def _sample_bias():
    w = WIN_MAX
    out = []
    i = (np.arange(N_HEADS * CHUNK) % CHUNK)[:, None]
    for d in DILATIONS:
        span = WIN_STEPS * d
        c = np.arange(w - span, w)[None, :]
        dist = w + i - c
        ok_buf = (dist % d == 0) & (dist <= span)
        cn = np.arange(LANES)[None, :] - (LANES - CHUNK)
        dn = i - cn
        ok_new = (cn >= 0) & (dn >= 0) & (dn % d == 0) & (dn <= span)
        ok = np.concatenate([ok_buf, ok_new], axis=1)
        out.append(jnp.asarray(np.where(ok, 0.0, NEG), F32))
    return out


SHIFT_HEAD_ROWS = 200
SHIFT_ROWS = -(-(D_ATT - SHIFT_HEAD_ROWS) // (8 * DIL_SHARE)) * 8


def _shift_rows(start, nrows, kc_ref, vc_ref, ko_ref, vo_ref, knt, vnt):
    w = kc_ref.shape[1]
    nq = CHUNK
    rows = pl.ds(start, nrows)
    lane = lax.broadcasted_iota(jnp.int32, (nrows, LANES), 1)
    for buf_ref, out_ref, new_ref in ((kc_ref, ko_ref, knt), (vc_ref, vo_ref, vnt)):
        rolled = pltpu.roll(buf_ref[rows, :], w - nq, 1)
        out_ref[0, rows, 0:w - LANES] = rolled[:, 0:w - LANES]
        out_ref[0, rows, w - LANES:] = jnp.where(lane >= LANES - nq, new_ref[rows, :],
                                                 rolled[:, w - LANES:])


def _sample_scores(q_ref, kn_ref, vn_ref, kc_ref, vc_ref, b1_ref, b2_ref, b3_ref, o_ref, knt, vnt):
    w = kc_ref.shape[1]
    nq = q_ref.shape[1]
    kc = kc_ref[...]
    vc = vc_ref[...]
    pad = jnp.zeros((LANES - nq, D_ATT), F32)
    knt[...] = jnp.concatenate([pad, kn_ref[0]], axis=0).T
    vnt[...] = jnp.concatenate([pad, vn_ref[0]], axis=0).T
    knb = knt[...].astype(BF16)
    vnb = vnt[...].astype(BF16)

    rows = N_HEADS * nq
    q = q_ref[0] * (HEAD_DIM ** -0.5)
    qt = jnp.concatenate([q] * N_HEADS, axis=0)
    rhead = lax.broadcasted_iota(jnp.int32, (rows, D_ATT), 0) // nq
    lhead = lax.broadcasted_iota(jnp.int32, (rows, D_ATT), 1) // HEAD_DIM
    qe = jnp.where(rhead == lhead, qt, 0.0).astype(BF16)
    kcb = kc.astype(BF16)
    vcb = vc.astype(BF16)
    s_buf = _dot(qe, kcb)
    s_new = _dot(qe, knb)

    outs, lses = [], []
    for d, b_ref in zip(DILATIONS, (b1_ref, b2_ref, b3_ref)):
        span = WIN_STEPS * d
        s = jnp.concatenate([s_buf[:, w - span:], s_new], axis=1) + b_ref[...]
        mx = jnp.max(s, axis=1, keepdims=True)
        p = jnp.exp(s - mx)
        l = jnp.sum(p, axis=1, keepdims=True)
        vcat = jnp.concatenate([vcb[:, w - span:], vnb], axis=1)
        o = lax.dot_general(p.astype(BF16), vcat, _NT, preferred_element_type=F32)
        outs.append(o * (1.0 / l))
        lses.append(mx + jnp.log(l))
    lmax = jnp.maximum(jnp.maximum(lses[0], lses[1]), lses[2])
    es = [jnp.exp(l - lmax) for l in lses]
    o = (es[0] * outs[0] + es[1] * outs[1] + es[2] * outs[2]) * (1.0 / (es[0] + es[1] + es[2]))
    o = jnp.where(rhead == lhead, o, 0.0)
    acc = o[0:nq]
    for h in range(1, N_HEADS):
        acc = acc + o[h * nq:(h + 1) * nq]
    o_ref[0] = acc


RING = 3


def _attention_kernel(nseq, nsteps, nsb, q_ref, kn_ref, vn_ref, kc_hbm, vc_hbm, b1_ref, b2_ref, b3_ref,
                      qp_ref, kp_ref, vp_ref, o_ref, ko_ref, vo_ref, op_ref,
                      kbuf, vbuf, acc, mrow, lrow, knt, vnt, kring, vring, sems):
    j = pl.program_id(0)

    def fetch(step):
        seq = jnp.minimum(step, nseq - 1)
        slot = step % RING
        return (pltpu.make_async_copy(kc_hbm.at[seq], kring.at[slot], sems.at[0, slot]),
                pltpu.make_async_copy(vc_hbm.at[seq], vring.at[slot], sems.at[1, slot]))

    @pl.when(j == 0)
    def _():
        for step in range(min(RING - 1, nsteps)):
            for cp in fetch(step):
                cp.start()

    @pl.when(j + RING - 1 < nsteps)
    def _():
        for cp in fetch(j + RING - 1):
            cp.start()

    for cp in fetch(j):
        cp.wait()
    kc_ref, vc_ref = kring.at[j % RING], vring.at[j % RING]
    _sample_scores(q_ref, kn_ref, vn_ref, kc_ref, vc_ref, b1_ref, b2_ref, b3_ref, o_ref, knt, vnt)
    shift = functools.partial(_shift_rows, kc_ref=kc_ref, vc_ref=vc_ref, ko_ref=ko_ref,
                              vo_ref=vo_ref, knt=knt, vnt=vnt)
    shift(0, SHIFT_HEAD_ROWS)

    def side_work(it):
        start = jnp.minimum(SHIFT_HEAD_ROWS + it * SHIFT_ROWS, D_ATT - SHIFT_ROWS)
        shift(pl.multiple_of(start, 8), SHIFT_ROWS)

    _dil_attn_part((j // DIL_PARTS) % nsb, j % DIL_PARTS, side_work, qp_ref, kp_ref, vp_ref, op_ref,
                   kbuf, vbuf, acc, mrow, lrow)


def _attention(q, kn, vn, kc, vc, qp, kp, vp):
    b, nq, _ = q.shape
    w = kc.shape[2]
    npair, s, _ = qp.shape
    sb = WIN_STEPS * DILATIONS[-1]
    assert w == WIN_MAX and nq == CHUNK and s % sb == 0 and DILATIONS[0] == 1
    nsb = s // sb
    ndil = npair * nsb * DIL_PARTS
    assert b <= ndil, "every sample sequence needs a grid step of the prompt attention"
    b1, b2, b3 = _sample_bias()
    seq = lambda j: (jnp.minimum(j, b - 1), 0, 0)
    new = pl.BlockSpec((1, nq, D_ATT), seq)
    buf = pl.BlockSpec((1, D_ATT, w), seq)
    blk = pl.BlockSpec((1, sb, LANES), lambda j: (j // (DIL_PARTS * nsb), (j // DIL_PARTS) % nsb, 0))
    hbm = pl.BlockSpec(memory_space=pl.ANY)
    return pl.pallas_call(
        functools.partial(_attention_kernel, b, ndil, nsb),
        out_shape=(jax.ShapeDtypeStruct(q.shape, F32), jax.ShapeDtypeStruct(kc.shape, F32),
                   jax.ShapeDtypeStruct(vc.shape, F32), jax.ShapeDtypeStruct(qp.shape, F32)),
        grid=(ndil,),
        in_specs=[new, new, new, hbm, hbm, _const_spec(b1.shape), _const_spec(b2.shape),
                  _const_spec(b3.shape), blk, blk, blk],
        out_specs=(new, buf, buf, blk),
        scratch_shapes=[pltpu.VMEM((2 * sb, LANES), F32), pltpu.VMEM((2 * sb, LANES), F32),
                        pltpu.VMEM((sb, LANES), F32), pltpu.VMEM((sb, LANES), F32),
                        pltpu.VMEM((sb, LANES), F32), pltpu.VMEM((D_ATT, LANES), F32),
                        pltpu.VMEM((D_ATT, LANES), F32), pltpu.VMEM((RING, D_ATT, w), F32),
                        pltpu.VMEM((RING, D_ATT, w), F32), pltpu.SemaphoreType.DMA((2, RING))],
        compiler_params=_params(("arbitrary",), RING_VMEM_LIMIT),
        name="attention",
    )(q, kn, vn, kc, vc, b1, b2, b3, qp, kp, vp)


def _split(a):
    hi = a.astype(BF16)
    return hi, (a - hi.astype(F32)).astype(BF16)


def _s5_prep_kernel(vec_ref, mat_ref, p_ref, qt_ref, t_ref, a_ref, pf_scr, q2_scr):
    ar, ai = vec_ref[0, 0:1, :], vec_ref[0, 1:2, :]
    dt = jnp.exp(vec_ref[0, 2:3, :])
    bre_ref, bim_ref, cre_ref, cim_ref = (mat_ref.at[m] for m in range(4))
    mag = jnp.exp(ar * dt)
    lr, li = mag * jnp.cos(ai * dt), mag * jnp.sin(ai * dt)
    den = 1.0 / (ar * ar + ai * ai)
    kr = ((lr - 1.0) * ar + li * ai) * den
    ki = (li * ar - (lr - 1.0) * ai) * den
    bre, bim = bre_ref[0], bim_ref[0]
    bbr, bbi = kr * bre - ki * bim, kr * bim + ki * bre
    cre, cim = cre_ref[0], cim_ref[0]
    pr, pi = [jnp.ones_like(lr)], [jnp.zeros_like(lr)]
    for _ in range(CHUNK):
        pr.append(pr[-1] * lr - pi[-1] * li)
        pi.append(pr[-2] * li + pi[-1] * lr)
    dd = 1.0 / (pr[CHUNK] * pr[CHUNK] + pi[CHUNK] * pi[CHUNK])
    dr, di = pr[CHUNK] * dd, -pi[CHUNK] * dd
    for s in range(CHUNK):
        n = CHUNK - 1 - s
        rows = slice(s * LANES, (s + 1) * LANES)
        pf_scr[rows, 0:SG_STATE] = bbr * pr[n] - bbi * pi[n]
        pf_scr[rows, SG_STATE:] = bbr * pi[n] + bbi * pr[n]
        n = s + 1
        re, im = cre * pr[n] - cim * pi[n], cre * pi[n] + cim * pr[n]
        qt_ref[0, rows, 0:SG_STATE] = re.astype(BF16)
        qt_ref[0, rows, SG_STATE:] = (-im).astype(BF16)
        q2_scr[rows, 0:SG_STATE] = dr * re - di * im
        q2_scr[rows, SG_STATE:] = -(dr * im + di * re)
    pf = pf_scr[...]
    p_ref[0] = pf.astype(BF16)
    a_ref[0] = jnp.concatenate([pr[CHUNK], pi[CHUNK]], axis=-1)
    ph, plo = _split(pf)
    qh, qlo = _split(q2_scr[...])
    band = 2 * LANES
    for c0 in range(0, CW, band):
        nr = c0 + band
        cols = slice(c0, c0 + band)
        t = (lax.dot_general(ph[0:nr], qh[cols], _NT, preferred_element_type=F32)
             + lax.dot_general(ph[0:nr], qlo[cols], _NT, preferred_element_type=F32)
             + lax.dot_general(plo[0:nr], qh[cols], _NT, preferred_element_type=F32))
        rb = lax.broadcasted_iota(jnp.int32, (nr, band), 0) // LANES
        cb = (lax.broadcasted_iota(jnp.int32, (nr, band), 1) + c0) // LANES
        t_ref[0, 0:nr, cols] = jnp.where(rb <= cb, t, 0.0).astype(BF16)
        if nr < CW:
            t_ref[0, nr:, cols] = jnp.zeros((CW - nr, band), BF16)


def _block_diag(a):
    g = LANES // SSM_GROUP
    m, _, r, c = a.shape
    a = a.reshape(m, SUPER, g, r, c)
    eye = jnp.eye(g, dtype=a.dtype)
    return jnp.einsum("msgrc,gh->msgrhc", a, eye).reshape(m, SUPER, g * r, g * c)


def _s5_prep(a_re, a_im, log_dt, b_re, b_im, c_re, c_im):
    ldt = jnp.broadcast_to(log_dt[:, None], (N_SSM_GROUPS, SSM_STATE))
    vecs = jnp.stack([a.reshape(SUPER, SG_STATE) for a in (a_re, a_im, ldt)], axis=1)
    mats = _block_diag(jnp.stack([jnp.swapaxes(b_re, 1, 2), jnp.swapaxes(b_im, 1, 2), c_re, c_im]))
    vec = pl.BlockSpec((1, 3, SG_STATE), lambda g: (g, 0, 0))
    mat = pl.BlockSpec((4, 1, LANES, SG_STATE), lambda g: (0, g, 0, 0))
    big = pl.BlockSpec((1, CW, CW), lambda g: (g, 0, 0))
    big_shape = jax.ShapeDtypeStruct((SUPER, CW, CW), BF16)
    return pl.pallas_call(
        _s5_prep_kernel,
        out_shape=(big_shape, big_shape, big_shape, jax.ShapeDtypeStruct((SUPER, 1, CW), F32)),
        grid=(SUPER,),
        in_specs=[vec, mat],
        out_specs=(big, big, big, pl.BlockSpec((1, 1, CW), lambda g: (g, 0, 0))),
        scratch_shapes=[pltpu.VMEM((CW, CW), F32), pltpu.VMEM((CW, CW), F32)],
        compiler_params=_params(("arbitrary",)),
        name="s5_prep",
    )(vecs, mats)


def _gelu_tanh(y):
    return 0.5 * y * (1.0 + jnp.tanh(math.sqrt(2.0 / math.pi) * (y + 0.044715 * (y * y * y))))


def _s5_out(u, ub, hin, qt, t, d):
    y = _dot(ub, t) + lax.dot_general(hin.astype(BF16), qt, _NT, preferred_element_type=F32) + d * u
    return _gelu_tanh(y)


def _s5_scan_kernel(u_ref, p_ref, qt_ref, t_ref, a_ref, d_ref, y_ref, hout_ref,
                    h_scr, gs_scr, hin_scr):
    tc = u_ref.shape[1]

    @pl.when(pl.program_id(0) == 0)
    def _():
        h_scr[...] = jnp.zeros_like(h_scr)

    ubs = [u_ref[sg].astype(BF16) for sg in range(SUPER)]
    for sg in range(SUPER):
        gs_scr[sg] = _dot(ubs[sg], p_ref[sg])
    ar = [a_ref[sg][:, 0:SG_STATE] for sg in range(SUPER)]
    ai = [a_ref[sg][:, SG_STATE:] for sg in range(SUPER)]
    hr = [h_scr[sg][:, 0:SG_STATE] for sg in range(SUPER)]
    hi = [h_scr[sg][:, SG_STATE:] for sg in range(SUPER)]
    for k in range(tc):
        for sg in range(SUPER):
            hin_scr[sg, k:k + 1, 0:SG_STATE] = hr[sg]
            hin_scr[sg, k:k + 1, SG_STATE:] = hi[sg]
            g = gs_scr[sg, k:k + 1, :]
            hr[sg], hi[sg] = (ar[sg] * hr[sg] - ai[sg] * hi[sg] + g[:, 0:SG_STATE],
                              ar[sg] * hi[sg] + ai[sg] * hr[sg] + g[:, SG_STATE:])
    for sg in range(SUPER):
        h_new = jnp.concatenate([hr[sg], hi[sg]], axis=-1)
        h_scr[sg] = h_new
        hout_ref[sg] = h_new
        y_ref[sg] = _s5_out(u_ref[sg], ubs[sg], hin_scr[sg], qt_ref[sg], t_ref[sg], d_ref[sg])


def _s5_scan(u, mats, dvec, tc):
    p, qt, t, a = mats
    nch = u.shape[1]
    tc = min(tc, nch)
    tile = pl.BlockSpec((SUPER, tc, CW), lambda i: (0, i, 0))
    return pl.pallas_call(
        _s5_scan_kernel,
        out_shape=(jax.ShapeDtypeStruct(u.shape, F32), jax.ShapeDtypeStruct((SUPER, 1, CW), F32)),
        grid=(nch // tc,),
        in_specs=[tile, _const_spec(p.shape), _const_spec(qt.shape), _const_spec(t.shape),
                  _const_spec(a.shape), _const_spec(dvec.shape)],
        out_specs=(tile, _const_spec((SUPER, 1, CW))),
        scratch_shapes=[pltpu.VMEM((SUPER, 1, CW), F32), pltpu.VMEM((SUPER, tc, CW), F32),
                        pltpu.VMEM((SUPER, tc, CW), F32)],
        compiler_params=_params(("arbitrary",)),
        name="s5_scan",
    )(u, p, qt, t, a, dvec)


def _s5_step_kernel(u_ref, hr_ref, hi_ref, p_ref, qt_ref, t_ref, a_ref, d_ref,
                    y_ref, hro_ref, hio_ref):
    u = u_ref[0]
    ub = u.astype(BF16)
    hr, hi = hr_ref[...], hi_ref[...]
    a = a_ref[0]
    ar, ai = a[:, 0:SG_STATE], a[:, SG_STATE:]
    gs = _dot(ub, p_ref[0])
    hro_ref[...] = ar * hr - ai * hi + gs[:, 0:SG_STATE]
    hio_ref[...] = ar * hi + ai * hr + gs[:, SG_STATE:]
    y_ref[0] = _s5_out(u, ub, jnp.concatenate([hr, hi], axis=-1), qt_ref[0], t_ref[0], d_ref[0])


def _s5_step(u, h_re, h_im, mats, dvec):
    p, qt, t, a = mats
    b = u.shape[1]
    tile = pl.BlockSpec((1, b, CW), lambda g: (g, 0, 0))
    st = pl.BlockSpec((b, SG_STATE), lambda g: (0, g))
    big = pl.BlockSpec((1, CW, CW), lambda g: (g, 0, 0))
    vec = pl.BlockSpec((1, 1, CW), lambda g: (g, 0, 0))
    st_shape = jax.ShapeDtypeStruct(h_re.shape, F32)
    return pl.pallas_call(
        _s5_step_kernel,
        out_shape=(jax.ShapeDtypeStruct(u.shape, F32), st_shape, st_shape),
        grid=(SUPER,),
        in_specs=[tile, st, st, big, big, big, vec, vec],
        out_specs=(tile, st, st),
        compiler_params=_params(("arbitrary",)),
        name="s5_step",
    )(u, h_re, h_im, p, qt, t, a, dvec)


def _stage_b_kernel(pair_major, x_ref, ada_ref, att_ref, gs_ref, g2_ref, wglu_ref, wout_ref,
                    wg_ref, wu_ref, wd_ref, y_ref, g_scr):
    kb, r, _ = x_ref.shape
    n = kb * r
    x = x_ref[...]
    ada = ada_ref[...]
    gt2 = ada[:, 5:6]
    sh3, sc3, gt3 = ada[:, 6:7], ada[:, 7:8], ada[:, 8:9]

    if pair_major:
        o_att = jnp.concatenate([att_ref[hp] for hp in range(D_ATT // LANES)], axis=-1)
    else:
        o_att = att_ref[...].reshape(n, D_ATT)

    for sg in range(SUPER):
        for s in range(CHUNK):
            g_scr[sg, pl.ds(s, n // CHUNK, stride=CHUNK), :] = gs_ref[sg, :, s * LANES:(s + 1) * LANES]
    g = jnp.concatenate([g_scr[sg] for sg in range(SUPER)], axis=-1)
    gl = _dot(g.astype(BF16), wglu_ref[...])
    o_ssm = gl[:, 0:D_SSM] * (1.0 / (1.0 + jnp.exp(-gl[:, D_SSM:])))
    mix = _dot(jnp.concatenate([o_att, o_ssm], axis=-1).astype(BF16), wout_ref[...])
    x2 = x + gt2 * mix.reshape(kb, r, D_MODEL)

    h = _rms(x2, g2_ref[...]) * (1.0 + sc3) + sh3
    f = _swiglu(h.reshape(n, D_MODEL).astype(BF16), wg_ref, wu_ref, wd_ref)
    y_ref[...] = x2 + 0.5 * gt3 * f.reshape(kb, r, D_MODEL)


def _stage_b(x3, ada3, att, gs, per_seq, wts, tm):
    b3, r3, _ = x3.shape
    if per_seq:
        kb, r = min(tm // r3, b3), r3
        grid = (b3 // kb,)
        tok = lambda i: (i, 0, 0)
        ada_spec = pl.BlockSpec((kb, N_ADA, D_MODEL), lambda i: (i, 0, 0))
    else:
        kb, r = 1, min(tm, r3)
        grid = (r3 // r,)
        tok = lambda i: (0, i, 0)
        ada_spec = pl.BlockSpec((1, N_ADA, D_MODEL), lambda i: (0, 0, 0))
    n = kb * r
    g2, wglu, wout, wg, wu, wd = wts
    if per_seq:
        att_spec = pl.BlockSpec((kb, r, D_ATT), tok)
    else:
        att_spec = pl.BlockSpec((D_ATT // LANES, n, LANES), lambda i: (0, i, 0))
    return pl.pallas_call(
        functools.partial(_stage_b_kernel, not per_seq),
        out_shape=jax.ShapeDtypeStruct(x3.shape, F32),
        grid=grid,
        in_specs=[pl.BlockSpec((kb, r, D_MODEL), tok), ada_spec, att_spec,
                  pl.BlockSpec((SUPER, n // CHUNK, CW), lambda i: (0, i, 0)),
                  _const_spec(g2.shape), _const_spec(wglu.shape), _const_spec(wout.shape),
                  _const_spec(wg.shape), _const_spec(wu.shape), _const_spec(wd.shape)],
        out_specs=pl.BlockSpec((kb, r, D_MODEL), tok),
        scratch_shapes=[pltpu.VMEM((SUPER, n, LANES), F32)],
        compiler_params=_params(("arbitrary",)),
        name="stage_b",
    )(x3, ada3, att, gs, g2, wglu, wout, wg, wu, wd)


def _layer(x_p, x_s, c_p, c_s, cache_k, cache_v, st_re, st_im, p):
    (w_ada, b_ada, g_ffn1, w1_gate, w1_up, w1_down, g_mix, w_in, g_q, g_k,
     a_re, a_im, log_dt, b_re, b_im, c_re, c_im, d_skip, w_glu, w_out,
     g_ffn2, w2_gate, w2_up, w2_down) = p
    bp, s, _ = x_p.shape
    bs, ns, _ = x_s.shape
    assert bp == 1 and ns == CHUNK and s % (WIN_STEPS * DILATIONS[-1]) == 0
    w_buf = cache_k.shape[2]
    tm = 512

    rows = bp + bs
    pad = (-rows) % 8
    c_all = jnp.concatenate([c_p, c_s, jnp.zeros((pad, D_MODEL), F32)], axis=0)
    ada = _ada(c_all, w_ada, b_ada)
    ada_p = ada[0:bp].reshape(bp, N_ADA, D_MODEL)
    ada_s = ada[bp:rows].reshape(bs, N_ADA, D_MODEL)

    vec = lambda g: g.reshape(1, 1, -1)
    head_gain = lambda g: jnp.tile(g, N_HEADS).reshape(1, D_ATT)
    seg = jnp.asarray(np.kron(np.eye(N_HEADS), np.full((HEAD_DIM, HEAD_DIM), 1.0 / HEAD_DIM)), BF16)
    bf = lambda w: w.astype(BF16)
    wts_a = (vec(g_ffn1), vec(g_mix), head_gain(g_q), head_gain(g_k), seg,
             bf(w1_gate), bf(w1_up), bf(w1_down), bf(w_in))
    wts_b = (vec(g_ffn2), bf(w_glu), bf(w_out), bf(w2_gate), bf(w2_up), bf(w2_down))

    x1_p, q_p, k_p, v_p, u_p = _stage_a(x_p, ada_p, 0, False, wts_a, tm)
    x1_s, q_s, k_s, v_s, u_s = _stage_a(x_s, ada_s, PAST_LEN, True, wts_a, tm)

    o_s, kwin_s, vwin_s, o_p = _attention(q_s, k_s, v_s, cache_k, cache_v, q_p, k_p, v_p)

    mats = _s5_prep(a_re, a_im, log_dt, b_re, b_im, c_re, c_im)
    dvec = jnp.tile(d_skip.reshape(SUPER, 1, LANES), (1, 1, CHUNK))
    gs_p, hfin_p = _s5_scan(u_p, mats, dvec, 256)
    gs_s, hre_s, him_s = _s5_step(u_s, st_re, st_im, mats, dvec)

    y_p = _stage_b(x1_p, ada_p, o_p, gs_p, False, wts_b, tm)
    y_s = _stage_b(x1_s, ada_s, o_s, gs_s, True, wts_b, tm)

    keep = min(WIN_MAX, s)
    tail = lambda a: jnp.transpose(a[:, s - keep:], (1, 0, 2)).reshape(bp, keep, N_HEADS, HEAD_DIM)
    kwin_p, vwin_p = tail(k_p), tail(v_p)
    hfin_p = hfin_p.reshape(SUPER, 2, SG_STATE)
    states = lambda h: h.reshape(-1, N_SSM_GROUPS, SSM_STATE)
    unflip = lambda a: jnp.transpose(a.reshape(bs, N_HEADS, HEAD_DIM, w_buf), (0, 3, 1, 2))
    return (y_p, y_s, kwin_p, vwin_p, states(hfin_p[:, 0]), states(hfin_p[:, 1]),
            unflip(kwin_s), unflip(vwin_s), states(hre_s), states(him_s))


def kernel(x_prompt, x_sample, c_prompt, c_sample, cache_k_win, cache_v_win, state_ssm_re, state_ssm_im, w_ada, b_ada, g_ffn1, w1_gate, w1_up, w1_down, g_mix, w_in, g_q, g_k, ssm_a_re, ssm_a_im, ssm_log_dt, ssm_b_re, ssm_b_im, ssm_c_re, ssm_c_im, ssm_d, w_glu, w_out, g_ffn2, w2_gate, w2_up, w2_down):
    depth = w_ada.shape[0]
    assert depth == 1
    bs = x_sample.shape[0]
    w_buf = cache_k_win.shape[2]
    p = tuple(a[0] for a in (w_ada, b_ada, g_ffn1, w1_gate, w1_up, w1_down, g_mix, w_in, g_q, g_k,
                             ssm_a_re, ssm_a_im, ssm_log_dt, ssm_b_re, ssm_b_im, ssm_c_re, ssm_c_im,
                             ssm_d, w_glu, w_out, g_ffn2, w2_gate, w2_up, w2_down))
    flip = lambda a: jnp.transpose(a[0], (0, 2, 3, 1)).reshape(bs, D_ATT, w_buf)
    outs = _layer(x_prompt, x_sample, c_prompt, c_sample, flip(cache_k_win), flip(cache_v_win),
                  state_ssm_re[0].reshape(bs, N_SSM_GROUPS * SSM_STATE),
                  state_ssm_im[0].reshape(bs, N_SSM_GROUPS * SSM_STATE), p)
    return tuple(o[None] if i >= 2 else o for i, o in enumerate(outs))
```

```python
import functools
import math

import numpy as np
import jax
import jax.numpy as jnp
from jax import lax
from jax.experimental import pallas as pl
from jax.experimental.pallas import tpu as pltpu

F32 = jnp.float32
BF16 = jnp.bfloat16

D_MODEL = 1024
D_ATT = 512
D_SSM = 512
HEAD_DIM = 64
N_HEADS = 8
ROT_DIM = 16
ROPE_THETA = 500000.0
DILATIONS = (1, 4, 16)
WIN_STEPS = 128
WIN_MAX = 2048
PAST_LEN = 8192
SSM_GROUP = 16
N_SSM_GROUPS = 32
SSM_STATE = 64
D_FF = 2816
N_ADA = 9
EPS = 1e-6

LANES = 128
CHUNK = 8
SUPER = D_SSM // LANES
SG_STATE = (LANES // SSM_GROUP) * SSM_STATE
CW = CHUNK * LANES
NEG = -1e30
VMEM_LIMIT = 56 * 1024 * 1024
RING_VMEM_LIMIT = 60 * 1024 * 1024

_NT = (((1,), (1,)), ((), ()))


def _params(sem, vmem=VMEM_LIMIT):
    return pltpu.CompilerParams(dimension_semantics=sem, vmem_limit_bytes=vmem)


def _const_spec(shape):
    nd = len(shape)
    return pl.BlockSpec(shape, lambda *_: (0,) * nd, pipeline_mode=pl.Buffered(1))


def _dot(a, b):
    return jnp.dot(a, b, preferred_element_type=F32)


def _rms(x, g):
    ms = jnp.mean(x * x, axis=-1, keepdims=True)
    return x * lax.rsqrt(ms + EPS) * g


def _swiglu(h, wg_ref, wu_ref, wd_ref):
    a = _dot(h, wg_ref[...])
    b = _dot(h, wu_ref[...])
    t = (a * (1.0 / (1.0 + jnp.exp(-a))) * b).astype(BF16)
    return _dot(t, wd_ref[...])


def _ada_kernel(c_ref, w_ref, b_ref, o_ref):
    c = c_ref[...]
    s = (c * (1.0 / (1.0 + jnp.exp(-c)))).astype(BF16)
    o_ref[...] = _dot(s, w_ref[...].astype(BF16)) + b_ref[...]


def _ada(c, w_ada, b_ada):
    m = c.shape[0]
    n = w_ada.shape[1]
    tn = n // 3
    return pl.pallas_call(
        _ada_kernel,
        out_shape=jax.ShapeDtypeStruct((m, n), F32),
        grid=(n // tn,),
        in_specs=[pl.BlockSpec((m, D_MODEL), lambda j: (0, 0)),
                  pl.BlockSpec((D_MODEL, tn), lambda j: (0, j)),
                  pl.BlockSpec((1, tn), lambda j: (0, j))],
        out_specs=pl.BlockSpec((m, tn), lambda j: (0, j)),
        compiler_params=_params(("arbitrary",)),
        name="ada",
    )(c, w_ada, b_ada.reshape(1, n))


def _stage_a_kernel(pair_major, x_ref, ada_ref, base_ref, off_ref, lanes_ref, gf_ref, gm_ref, gq_ref,
                    gk_ref, seg_ref, wg_ref, wu_ref, wd_ref, win_ref,
                    x1_ref, q_ref, k_ref, v_ref, u_ref, u_scr):
    kb, r, _ = x_ref.shape
    n = kb * r
    x = x_ref[...]
    ada = ada_ref[...]
    sh1, sc1, gt1 = ada[:, 0:1], ada[:, 1:2], ada[:, 2:3]
    sh2, sc2 = ada[:, 3:4], ada[:, 4:5]

    h = _rms(x, gf_ref[...]) * (1.0 + sc1) + sh1
    f = _swiglu(h.reshape(n, D_MODEL).astype(BF16), wg_ref, wu_ref, wd_ref)
    x1 = x + 0.5 * gt1 * f.reshape(kb, r, D_MODEL)
    x1_ref[...] = x1

    h = _rms(x1, gm_ref[...]) * (1.0 + sc2) + sh2
    proj = _dot(h.reshape(n, D_MODEL).astype(BF16), win_ref[...])

    ca, sa = base_ref[0, 0:1, :], base_ref[0, 1:2, :]
    cb, sb = off_ref[0], off_ref[1]
    c, s = ca * cb - sa * sb, sa * cb + ca * sb
    rot, neg_lo, pos_hi = lanes_ref[0], lanes_ref[1], lanes_ref[2]
    tile4 = lambda t: jnp.concatenate([t] * (D_ATT // LANES), axis=-1)[None]
    cos, sina, sinb = tile4(c * rot + (1.0 - rot)), tile4(s * neg_lo), tile4(s * pos_hi)

    def head_norm_rope(z, g):
        ms = _dot((z * z).astype(BF16), seg_ref[...])
        zn = z * lax.rsqrt(ms + EPS) * g
        up = pltpu.roll(zn, D_ATT - ROT_DIM // 2, 1).reshape(kb, r, D_ATT)
        dn = pltpu.roll(zn, ROT_DIM // 2, 1).reshape(kb, r, D_ATT)
        return zn.reshape(kb, r, D_ATT) * cos + up * sina + dn * sinb

    q = head_norm_rope(proj[:, 0:D_ATT], gq_ref[...])
    k = head_norm_rope(proj[:, D_ATT:2 * D_ATT], gk_ref[...])
    v = proj[:, 2 * D_ATT:3 * D_ATT]
    if pair_major:
        for z, z_ref in ((q.reshape(n, D_ATT), q_ref), (k.reshape(n, D_ATT), k_ref), (v, v_ref)):
            for hp in range(D_ATT // LANES):
                z_ref[hp] = z[:, hp * LANES:(hp + 1) * LANES]
    else:
        q_ref[...] = q
        k_ref[...] = k
        v_ref[...] = v.reshape(kb, r, D_ATT)
    u = proj[:, 3 * D_ATT:]
    for sg in range(SUPER):
        u_scr[sg] = u[:, sg * LANES:(sg + 1) * LANES]
    for sg in range(SUPER):
        for s in range(CHUNK):
            u_ref[sg, :, s * LANES:(s + 1) * LANES] = u_scr[sg, pl.ds(s, n // CHUNK, stride=CHUNK), :]


def _rope_inputs(start, n_tiles, tile):
    half = ROT_DIM // 2
    j = np.arange(LANES) % HEAD_DIM
    inv = ROPE_THETA ** (-(j % half).astype(np.float64) / half)
    a = (start + tile * np.arange(n_tiles))[:, None] * inv[None, :]
    b = np.arange(tile)[:, None] * inv[None, :]
    lo = (j < half).astype(np.float32)
    hi = ((j >= half) & (j < ROT_DIM)).astype(np.float32)
    lanes = np.stack([lo + hi, -lo, hi])[:, None, :]
    base = np.stack([np.cos(a), np.sin(a)], axis=1).astype(np.float32)
    off = np.stack([np.cos(b), np.sin(b)]).astype(np.float32)
    return jnp.asarray(base), jnp.asarray(off), jnp.asarray(lanes)


def _stage_a(x3, ada3, pos0, per_seq, wts, tm):
    b3, r3, _ = x3.shape
    if per_seq:
        kb, r = min(tm // r3, b3), r3
        grid = (b3 // kb,)
        tok = lambda i: (i, 0, 0)
        tab = lambda i: (0, 0, 0)
        ada_spec = pl.BlockSpec((kb, N_ADA, D_MODEL), lambda i: (i, 0, 0))
        base, off, lanes = _rope_inputs(pos0, 1, r)
    else:
        kb, r = 1, min(tm, r3)
        grid = (r3 // r,)
        tok = lambda i: (0, i, 0)
        tab = lambda i: (i, 0, 0)
        ada_spec = pl.BlockSpec((1, N_ADA, D_MODEL), lambda i: (0, 0, 0))
        base, off, lanes = _rope_inputs(pos0, r3 // r, r)
    n = kb * r
    ntok = b3 * r3
    gf, gm, gq, gk, seg, wg, wu, wd, win = wts
    base_spec = pl.BlockSpec((1, 2, LANES), tab)
    if per_seq:
        att_spec = pl.BlockSpec((kb, r, D_ATT), tok)
        att_shape = jax.ShapeDtypeStruct((b3, r3, D_ATT), F32)
    else:
        att_spec = pl.BlockSpec((D_ATT // LANES, n, LANES), lambda i: (0, i, 0))
        att_shape = jax.ShapeDtypeStruct((D_ATT // LANES, ntok, LANES), F32)
    return pl.pallas_call(
        functools.partial(_stage_a_kernel, not per_seq),
        out_shape=(jax.ShapeDtypeStruct(x3.shape, F32), att_shape, att_shape, att_shape,
                   jax.ShapeDtypeStruct((SUPER, ntok // CHUNK, CW), F32)),
        grid=grid,
        in_specs=[pl.BlockSpec((kb, r, D_MODEL), tok), ada_spec, base_spec,
                  _const_spec(off.shape), _const_spec(lanes.shape),
                  _const_spec(gf.shape), _const_spec(gm.shape), _const_spec(gq.shape),
                  _const_spec(gk.shape), _const_spec(seg.shape), _const_spec(wg.shape),
                  _const_spec(wu.shape), _const_spec(wd.shape), _const_spec(win.shape)],
        out_specs=(pl.BlockSpec((kb, r, D_MODEL), tok), att_spec, att_spec, att_spec,
                   pl.BlockSpec((SUPER, n // CHUNK, CW), lambda i: (0, i, 0))),
        scratch_shapes=[pltpu.VMEM((SUPER, n, LANES), F32)],
        compiler_params=_params(("arbitrary",)),
        name="stage_a",
    )(x3, ada3, base, off, lanes, gf, gm, gq, gk, seg, wg, wu, wd, win)


UNITS_PER_ITER = 4


def _rows(start, size, stride):
    return pl.ds(start, size) if stride == 1 else pl.ds(start, size, stride=stride)


DIL_PARTS = 4
DIL_SHARE = len(DILATIONS) * DILATIONS[-1] // (UNITS_PER_ITER * DIL_PARTS)


def _dil_attn_part(i, part, side_work, q_ref, k_ref, v_ref, o_ref, kbuf, vbuf, acc, mrow, lrow):
    sb = q_ref.shape[1]
    tq = WIN_STEPS

    @pl.when(jnp.logical_and(part == 0, i == 0))
    def _():
        kbuf[0:sb, :] = jnp.zeros((sb, LANES), F32)
        vbuf[0:sb, :] = jnp.zeros((sb, LANES), F32)

    @pl.when(jnp.logical_and(part == 0, i > 0))
    def _():
        kbuf[0:sb, :] = kbuf[sb:2 * sb, :]
        vbuf[0:sb, :] = vbuf[sb:2 * sb, :]

    @pl.when(part == 0)
    def _():
        kbuf[sb:2 * sb, :] = k_ref[0]
        vbuf[sb:2 * sb, :] = v_ref[0]

    row = lax.broadcasted_iota(jnp.int32, (tq, 2 * tq), 0)
    col = lax.broadcasted_iota(jnp.int32, (tq, 2 * tq), 1)
    band = jnp.where(col >= row, 0.0, NEG)
    band = jnp.where(col <= row + tq, band, NEG)
    band0 = jnp.where(col >= tq, band, NEG)
    band_first = jnp.where(i == 0, band0, band)
    low = lax.broadcasted_iota(jnp.int32, (tq, LANES), 1) < HEAD_DIM
    high = jnp.logical_not(low)
    klow = lax.broadcasted_iota(jnp.int32, (2 * tq, LANES), 1) < HEAD_DIM
    ones_sel = jnp.concatenate([jnp.where(klow, 1.0, 0.0), jnp.where(klow, 0.0, 1.0)], axis=0).astype(BF16)

    def unit(d, qs, bias, mode):
        qp = (q_ref[0, _rows(qs, tq, d), :] * (HEAD_DIM ** -0.5)).astype(BF16)
        kp = kbuf[_rows(sb + qs - d * tq, 2 * tq, d), :].astype(BF16)
        vp = vbuf[_rows(sb + qs - d * tq, 2 * tq, d), :].astype(BF16)
        ps, mxs = [], []
        for sel in (low, high):
            qe = jnp.where(sel, qp, jnp.zeros_like(qp))
            s = lax.dot_general(qe, kp, _NT, preferred_element_type=F32) + bias
            mx = jnp.max(s, axis=1, keepdims=True)
            ps.append(jnp.exp(s - mx).astype(BF16))
            mxs.append(mx)
        zero = jnp.zeros_like(vp)
        vsel = jnp.concatenate([jnp.where(klow, vp, zero), jnp.where(klow, zero, vp)], axis=0)
        ol = _dot(jnp.concatenate(ps, axis=1), jnp.concatenate([vsel, ones_sel], axis=1))
        o, l = ol[:, 0:LANES], ol[:, LANES:]
        mx = jnp.where(low, mxs[0], mxs[1])
        rows = _rows(qs, tq, d)
        if mode != "init":
            m_old = mrow[rows, :]
            m_new = jnp.maximum(m_old, mx)
            a_old = jnp.exp(m_old - m_new)
            a_new = jnp.exp(mx - m_new)
            o = acc[rows, :] * a_old + o * a_new
            l = lrow[rows, :] * a_old + l * a_new
            mx = m_new
        if mode == "final":
            o_ref[0, rows, :] = o * (1.0 / l)
        else:
            acc[rows, :] = o
            mrow[rows, :] = mx
            lrow[rows, :] = l

    upi = UNITS_PER_ITER
    d16, d4 = DILATIONS[2], DILATIONS[1]
    nblk = sb // tq
    assert d16 == nblk and d4 % upi == 0

    def body16(g):
        for u in range(upi):
            unit(d16, g * upi + u, band_first, "init")

    def body4(g):
        mb = (g * upi) // d4
        bias = jnp.where(mb == 0, band_first, band)
        for u in range(upi):
            unit(d4, (g * upi) % d4 + u + d4 * tq * mb, bias, "merge")

    def body1(g):
        for u in range(upi):
            bias = jnp.where(g == 0, band_first, band) if u == 0 else band
            unit(1, (g * upi + u) * tq, bias, "final")

    counts = (nblk // upi,) * 3
    assert DIL_SHARE * DIL_PARTS == sum(counts)
    first = 0
    for body, cnt in zip((body16, body4, body1), counts):
        lo = jnp.clip(part * DIL_SHARE - first, 0, cnt)
        hi = jnp.clip((part + 1) * DIL_SHARE - first, 0, cnt)
        off = first - part * DIL_SHARE

        def wrapped(g, c, body=body, off=off):
            side_work(g + off)
            body(g)
            return c

        lax.fori_loop(lo, hi, wrapped, 0)
        first += cnt


def _sample_bias():
    w = WIN_MAX
    out = []
    i = (np.arange(N_HEADS * CHUNK) % CHUNK)[:, None]
    for d in DILATIONS:
        span = WIN_STEPS * d
        c = np.arange(w - span, w)[None, :]
        dist = w + i - c
        ok_buf = (dist % d == 0) & (dist <= span)
        cn = np.arange(LANES)[None, :] - (LANES - CHUNK)
        dn = i - cn
        ok_new = (cn >= 0) & (dn >= 0) & (dn % d == 0) & (dn <= span)
        ok = np.concatenate([ok_buf, ok_new], axis=1)
        out.append(jnp.asarray(np.where(ok, 0.0, NEG), F32))
    return out


SHIFT_HEAD_ROWS = 200
SHIFT_ROWS = -(-(D_ATT - SHIFT_HEAD_ROWS) // (8 * DIL_SHARE)) * 8


def _shift_rows(start, nrows, kc_ref, vc_ref, ko_ref, vo_ref, knt, vnt):
    w = kc_ref.shape[1]
    nq = CHUNK
    rows = pl.ds(start, nrows)
    lane = lax.broadcasted_iota(jnp.int32, (nrows, LANES), 1)
    for buf_ref, out_ref, new_ref in ((kc_ref, ko_ref, knt), (vc_ref, vo_ref, vnt)):
        rolled = pltpu.roll(buf_ref[rows, :], w - nq, 1)
        out_ref[0, rows, 0:w - LANES] = rolled[:, 0:w - LANES]
        out_ref[0, rows, w - LANES:] = jnp.where(lane >= LANES - nq, new_ref[rows, :],
                                                 rolled[:, w - LANES:])


def _sample_scores(q_ref, kn_ref, vn_ref, kc_ref, vc_ref, b1_ref, b2_ref, b3_ref, o_ref, knt, vnt):
    w = kc_ref.shape[1]
    nq = q_ref.shape[1]
    kc = kc_ref[...]
    vc = vc_ref[...]
    pad = jnp.zeros((LANES - nq, D_ATT), F32)
    knt[...] = jnp.concatenate([pad, kn_ref[0]], axis=0).T
    vnt[...] = jnp.concatenate([pad, vn_ref[0]], axis=0).T
    knb = knt[...].astype(BF16)
    vnb = vnt[...].astype(BF16)

    rows = N_HEADS * nq
    q = q_ref[0] * (HEAD_DIM ** -0.5)
    qt = jnp.concatenate([q] * N_HEADS, axis=0)
    rhead = lax.broadcasted_iota(jnp.int32, (rows, D_ATT), 0) // nq
    lhead = lax.broadcasted_iota(jnp.int32, (rows, D_ATT), 1) // HEAD_DIM
    qe = jnp.where(rhead == lhead, qt, 0.0).astype(BF16)
    kcb = kc.astype(BF16)
    vcb = vc.astype(BF16)
    s_buf = _dot(qe, kcb)
    s_new = _dot(qe, knb)

    outs, lses = [], []
    for d, b_ref in zip(DILATIONS, (b1_ref, b2_ref, b3_ref)):
        span = WIN_STEPS * d
        s = jnp.concatenate([s_buf[:, w - span:], s_new], axis=1) + b_ref[...]
        mx = jnp.max(s, axis=1, keepdims=True)
        p = jnp.exp(s - mx)
        l = jnp.sum(p, axis=1, keepdims=True)
        vcat = jnp.concatenate([vcb[:, w - span:], vnb], axis=1)
        o = lax.dot_general(p.astype(BF16), vcat, _NT, preferred_element_type=F32)
        outs.append(o * (1.0 / l))
        lses.append(mx + jnp.log(l))
    lmax = jnp.maximum(jnp.maximum(lses[0], lses[1]), lses[2])
    es = [jnp.exp(l - lmax) for l in lses]
    o = (es[0] * outs[0] + es[1] * outs[1] + es[2] * outs[2]) * (1.0 / (es[0] + es[1] + es[2]))
    o = jnp.where(rhead == lhead, o, 0.0)
    acc = o[0:nq]
    for h in range(1, N_HEADS):
        acc = acc + o[h * nq:(h + 1) * nq]
    o_ref[0] = acc


RING = 3


def _attention_kernel(nseq, nsteps, nsb, q_ref, kn_ref, vn_ref, kc_hbm, vc_hbm, b1_ref, b2_ref, b3_ref,
                      qp_ref, kp_ref, vp_ref, o_ref, ko_ref, vo_ref, op_ref,
                      kbuf, vbuf, acc, mrow, lrow, knt, vnt, kring, vring, sems):
    j = pl.program_id(0)

    def fetch(step):
        seq = jnp.minimum(step, nseq - 1)
        slot = step % RING
        return (pltpu.make_async_copy(kc_hbm.at[seq], kring.at[slot], sems.at[0, slot]),
                pltpu.make_async_copy(vc_hbm.at[seq], vring.at[slot], sems.at[1, slot]))

    @pl.when(j == 0)
    def _():
        for step in range(min(RING - 1, nsteps)):
            for cp in fetch(step):
                cp.start()

    @pl.when(j + RING - 1 < nsteps)
    def _():
        for cp in fetch(j + RING - 1):
            cp.start()

    for cp in fetch(j):
        cp.wait()
    kc_ref, vc_ref = kring.at[j % RING], vring.at[j % RING]
    _sample_scores(q_ref, kn_ref, vn_ref, kc_ref, vc_ref, b1_ref, b2_ref, b3_ref, o_ref, knt, vnt)
    shift = functools.partial(_shift_rows, kc_ref=kc_ref, vc_ref=vc_ref, ko_ref=ko_ref,
                              vo_ref=vo_ref, knt=knt, vnt=vnt)
    shift(0, SHIFT_HEAD_ROWS)

    def side_work(it):
        start = jnp.minimum(SHIFT_HEAD_ROWS + it * SHIFT_ROWS, D_ATT - SHIFT_ROWS)
        shift(pl.multiple_of(start, 8), SHIFT_ROWS)

    _dil_attn_part((j // DIL_PARTS) % nsb, j % DIL_PARTS, side_work, qp_ref, kp_ref, vp_ref, op_ref,
                   kbuf, vbuf, acc, mrow, lrow)


def _attention(q, kn, vn, kc, vc, qp, kp, vp):
    b, nq, _ = q.shape
    w = kc.shape[2]
    npair, s, _ = qp.shape
    sb = WIN_STEPS * DILATIONS[-1]
    assert w == WIN_MAX and nq == CHUNK and s % sb == 0 and DILATIONS[0] == 1
    nsb = s // sb
    ndil = npair * nsb * DIL_PARTS
    assert b <= ndil, "every sample sequence needs a grid step of the prompt attention"
    b1, b2, b3 = _sample_bias()
    seq = lambda j: (jnp.minimum(j, b - 1), 0, 0)
    new = pl.BlockSpec((1, nq, D_ATT), seq)
    buf = pl.BlockSpec((1, D_ATT, w), seq)
    blk = pl.BlockSpec((1, sb, LANES), lambda j: (j // (DIL_PARTS * nsb), (j // DIL_PARTS) % nsb, 0))
    hbm = pl.BlockSpec(memory_space=pl.ANY)
    return pl.pallas_call(
        functools.partial(_attention_kernel, b, ndil, nsb),
        out_shape=(jax.ShapeDtypeStruct(q.shape, F32), jax.ShapeDtypeStruct(kc.shape, F32),
                   jax.ShapeDtypeStruct(vc.shape, F32), jax.ShapeDtypeStruct(qp.shape, F32)),
        grid=(ndil,),
        in_specs=[new, new, new, hbm, hbm, _const_spec(b1.shape), _const_spec(b2.shape),
                  _const_spec(b3.shape), blk, blk, blk],
        out_specs=(new, buf, buf, blk),
        scratch_shapes=[pltpu.VMEM((2 * sb, LANES), F32), pltpu.VMEM((2 * sb, LANES), F32),
                        pltpu.VMEM((sb, LANES), F32), pltpu.VMEM((sb, LANES), F32),
                        pltpu.VMEM((sb, LANES), F32), pltpu.VMEM((D_ATT, LANES), F32),
                        pltpu.VMEM((D_ATT, LANES), F32), pltpu.VMEM((RING, D_ATT, w), F32),
                        pltpu.VMEM((RING, D_ATT, w), F32), pltpu.SemaphoreType.DMA((2, RING))],
        compiler_params=_params(("arbitrary",), RING_VMEM_LIMIT),
        name="attention",
    )(q, kn, vn, kc, vc, b1, b2, b3, qp, kp, vp)


def _split(a):
    hi = a.astype(BF16)
    return hi, (a - hi.astype(F32)).astype(BF16)


def _s5_prep_kernel(vec_ref, mat_ref, p_ref, qt_ref, t_ref, a_ref, pf_scr):
    ar, ai = vec_ref[0, 0:1, :], vec_ref[0, 1:2, :]
    dt = jnp.exp(vec_ref[0, 2:3, :])
    bre_ref, bim_ref, cre_ref, cim_ref = (mat_ref.at[m] for m in range(4))
    mag = jnp.exp(ar * dt)
    lr, li = mag * jnp.cos(ai * dt), mag * jnp.sin(ai * dt)
    den = 1.0 / (ar * ar + ai * ai)
    kr = ((lr - 1.0) * ar + li * ai) * den
    ki = (li * ar - (lr - 1.0) * ai) * den
    bre, bim = bre_ref[0], bim_ref[0]
    bbr, bbi = kr * bre - ki * bim, kr * bim + ki * bre
    cre, cim = cre_ref[0], cim_ref[0]
    pr, pi = [jnp.ones_like(lr)], [jnp.zeros_like(lr)]
    for _ in range(CHUNK):
        pr.append(pr[-1] * lr - pi[-1] * li)
        pi.append(pr[-2] * li + pi[-1] * lr)
    for s in range(CHUNK):
        n = CHUNK - 1 - s
        rows = slice(s * LANES, (s + 1) * LANES)
        pf_scr[rows, 0:SG_STATE] = bbr * pr[n] - bbi * pi[n]
        pf_scr[rows, SG_STATE:] = bbr * pi[n] + bbi * pr[n]
        n = s + 1
        re, im = cre * pr[n] - cim * pi[n], cre * pi[n] + cim * pr[n]
        qt_ref[0, rows, 0:SG_STATE] = re.astype(BF16)
        qt_ref[0, rows, SG_STATE:] = (-im).astype(BF16)
    pf = pf_scr[...]
    p_ref[0] = pf.astype(BF16)
    a_ref[0] = jnp.concatenate([pr[CHUNK], pi[CHUNK]], axis=-1)
    ph, plo = _split(pf)
    ch, clo = _split(jnp.concatenate([cre, -cim], axis=-1))
    lags = (lax.dot_general(ph, ch, _NT, preferred_element_type=F32)
            + lax.dot_general(ph, clo, _NT, preferred_element_type=F32)
            + lax.dot_general(plo, ch, _NT, preferred_element_type=F32)).astype(BF16)
    for tau in range(CHUNK):
        cols = slice(tau * LANES, (tau + 1) * LANES)
        nr = (tau + 1) * LANES
        t_ref[0, 0:nr, cols] = lags[CW - nr:, :]
        if nr < CW:
            t_ref[0, nr:, cols] = jnp.zeros((CW - nr, LANES), BF16)


def _block_diag(a):
    g = LANES // SSM_GROUP
    m, _, r, c = a.shape
    a = a.reshape(m, SUPER, g, r, c)
    eye = jnp.eye(g, dtype=a.dtype)
    return jnp.einsum("msgrc,gh->msgrhc", a, eye).reshape(m, SUPER, g * r, g * c)


def _s5_prep(a_re, a_im, log_dt, b_re, b_im, c_re, c_im):
    ldt = jnp.broadcast_to(log_dt[:, None], (N_SSM_GROUPS, SSM_STATE))
    vecs = jnp.stack([a.reshape(SUPER, SG_STATE) for a in (a_re, a_im, ldt)], axis=1)
    mats = _block_diag(jnp.stack([jnp.swapaxes(b_re, 1, 2), jnp.swapaxes(b_im, 1, 2), c_re, c_im]))
    vec = pl.BlockSpec((1, 3, SG_STATE), lambda g: (g, 0, 0))
    mat = pl.BlockSpec((4, 1, LANES, SG_STATE), lambda g: (0, g, 0, 0))
    big = pl.BlockSpec((1, CW, CW), lambda g: (g, 0, 0))
    big_shape = jax.ShapeDtypeStruct((SUPER, CW, CW), BF16)
    return pl.pallas_call(
        _s5_prep_kernel,
        out_shape=(big_shape, big_shape, big_shape, jax.ShapeDtypeStruct((SUPER, 1, CW), F32)),
        grid=(SUPER,),
        in_specs=[vec, mat],
        out_specs=(big, big, big, pl.BlockSpec((1, 1, CW), lambda g: (g, 0, 0))),
        scratch_shapes=[pltpu.VMEM((CW, CW), F32)],
        compiler_params=_params(("arbitrary",)),
        name="s5_prep",
    )(vecs, mats)


def _gelu_tanh(y):
    return 0.5 * y * (1.0 + jnp.tanh(math.sqrt(2.0 / math.pi) * (y + 0.044715 * (y * y * y))))


def _s5_out(u, ub, hin, qt, t, d):
    y = _dot(ub, t) + lax.dot_general(hin.astype(BF16), qt, _NT, preferred_element_type=F32) + d * u
    return _gelu_tanh(y)


def _s5_scan_kernel(u_ref, p_ref, qt_ref, t_ref, a_ref, d_ref, y_ref, hout_ref,
                    h_scr, gs_scr, hin_scr):
    tc = u_ref.shape[1]

    @pl.when(pl.program_id(0) == 0)
    def _():
        h_scr[...] = jnp.zeros_like(h_scr)

    ubs = [u_ref[sg].astype(BF16) for sg in range(SUPER)]
    for sg in range(SUPER):
        gs_scr[sg] = _dot(ubs[sg], p_ref[sg])
    ar = [a_ref[sg][:, 0:SG_STATE] for sg in range(SUPER)]
    ai = [a_ref[sg][:, SG_STATE:] for sg in range(SUPER)]
    hr = [h_scr[sg][:, 0:SG_STATE] for sg in range(SUPER)]
    hi = [h_scr[sg][:, SG_STATE:] for sg in range(SUPER)]
    for k in range(tc):
        for sg in range(SUPER):
            hin_scr[sg, k:k + 1, 0:SG_STATE] = hr[sg]
            hin_scr[sg, k:k + 1, SG_STATE:] = hi[sg]
            g = gs_scr[sg, k:k + 1, :]
            hr[sg], hi[sg] = (ar[sg] * hr[sg] - ai[sg] * hi[sg] + g[:, 0:SG_STATE],
                              ar[sg] * hi[sg] + ai[sg] * hr[sg] + g[:, SG_STATE:])
    for sg in range(SUPER):
        h_new = jnp.concatenate([hr[sg], hi[sg]], axis=-1)
        h_scr[sg] = h_new
        hout_ref[sg] = h_new
        y_ref[sg] = _s5_out(u_ref[sg], ubs[sg], hin_scr[sg], qt_ref[sg], t_ref[sg], d_ref[sg])


def _s5_scan(u, mats, dvec, tc):
    p, qt, t, a = mats
    nch = u.shape[1]
    tc = min(tc, nch)
    tile = pl.BlockSpec((SUPER, tc, CW), lambda i: (0, i, 0))
    return pl.pallas_call(
        _s5_scan_kernel,
        out_shape=(jax.ShapeDtypeStruct(u.shape, F32), jax.ShapeDtypeStruct((SUPER, 1, CW), F32)),
        grid=(nch // tc,),
        in_specs=[tile, _const_spec(p.shape), _const_spec(qt.shape), _const_spec(t.shape),
                  _const_spec(a.shape), _const_spec(dvec.shape)],
        out_specs=(tile, _const_spec((SUPER, 1, CW))),
        scratch_shapes=[pltpu.VMEM((SUPER, 1, CW), F32), pltpu.VMEM((SUPER, tc, CW), F32),
                        pltpu.VMEM((SUPER, tc, CW), F32)],
        compiler_params=_params(("arbitrary",)),
        name="s5_scan",
    )(u, p, qt, t, a, dvec)


def _s5_step_kernel(u_ref, hr_ref, hi_ref, p_ref, qt_ref, t_ref, a_ref, d_ref,
                    y_ref, hro_ref, hio_ref):
    u = u_ref[0]
    ub = u.astype(BF16)
    hr, hi = hr_ref[...], hi_ref[...]
    a = a_ref[0]
    ar, ai = a[:, 0:SG_STATE], a[:, SG_STATE:]
    gs = _dot(ub, p_ref[0])
    hro_ref[...] = ar * hr - ai * hi + gs[:, 0:SG_STATE]
    hio_ref[...] = ar * hi + ai * hr + gs[:, SG_STATE:]
    y_ref[0] = _s5_out(u, ub, jnp.concatenate([hr, hi], axis=-1), qt_ref[0], t_ref[0], d_ref[0])


def _s5_step(u, h_re, h_im, mats, dvec):
    p, qt, t, a = mats
    b = u.shape[1]
    tile = pl.BlockSpec((1, b, CW), lambda g: (g, 0, 0))
    st = pl.BlockSpec((b, SG_STATE), lambda g: (0, g))
    big = pl.BlockSpec((1, CW, CW), lambda g: (g, 0, 0))
    vec = pl.BlockSpec((1, 1, CW), lambda g: (g, 0, 0))
    st_shape = jax.ShapeDtypeStruct(h_re.shape, F32)
    return pl.pallas_call(
        _s5_step_kernel,
        out_shape=(jax.ShapeDtypeStruct(u.shape, F32), st_shape, st_shape),
        grid=(SUPER,),
        in_specs=[tile, st, st, big, big, big, vec, vec],
        out_specs=(tile, st, st),
        compiler_params=_params(("arbitrary",)),
        name="s5_step",
    )(u, h_re, h_im, p, qt, t, a, dvec)


def _stage_b_kernel(pair_major, x_ref, ada_ref, att_ref, gs_ref, g2_ref, wglu_ref, wout_ref,
                    wg_ref, wu_ref, wd_ref, y_ref, g_scr):
    kb, r, _ = x_ref.shape
    n = kb * r
    x = x_ref[...]
    ada = ada_ref[...]
    gt2 = ada[:, 5:6]
    sh3, sc3, gt3 = ada[:, 6:7], ada[:, 7:8], ada[:, 8:9]

    if pair_major:
        o_att = jnp.concatenate([att_ref[hp] for hp in range(D_ATT // LANES)], axis=-1)
    else:
        o_att = att_ref[...].reshape(n, D_ATT)

    for sg in range(SUPER):
        for s in range(CHUNK):
            g_scr[sg, pl.ds(s, n // CHUNK, stride=CHUNK), :] = gs_ref[sg, :, s * LANES:(s + 1) * LANES]
    g = jnp.concatenate([g_scr[sg] for sg in range(SUPER)], axis=-1)
    gl = _dot(g.astype(BF16), wglu_ref[...])
    o_ssm = gl[:, 0:D_SSM] * (1.0 / (1.0 + jnp.exp(-gl[:, D_SSM:])))
    mix = _dot(jnp.concatenate([o_att, o_ssm], axis=-1).astype(BF16), wout_ref[...])
    x2 = x + gt2 * mix.reshape(kb, r, D_MODEL)

    h = _rms(x2, g2_ref[...]) * (1.0 + sc3) + sh3
    f = _swiglu(h.reshape(n, D_MODEL).astype(BF16), wg_ref, wu_ref, wd_ref)
    y_ref[...] = x2 + 0.5 * gt3 * f.reshape(kb, r, D_MODEL)


def _stage_b(x3, ada3, att, gs, per_seq, wts, tm):
    b3, r3, _ = x3.shape
    if per_seq:
        kb, r = min(tm // r3, b3), r3
        grid = (b3 // kb,)
        tok = lambda i: (i, 0, 0)
        ada_spec = pl.BlockSpec((kb, N_ADA, D_MODEL), lambda i: (i, 0, 0))
    else:
        kb, r = 1, min(tm, r3)
        grid = (r3 // r,)
        tok = lambda i: (0, i, 0)
        ada_spec = pl.BlockSpec((1, N_ADA, D_MODEL), lambda i: (0, 0, 0))
    n = kb * r
    g2, wglu, wout, wg, wu, wd = wts
    if per_seq:
        att_spec = pl.BlockSpec((kb, r, D_ATT), tok)
    else:
        att_spec = pl.BlockSpec((D_ATT // LANES, n, LANES), lambda i: (0, i, 0))
    return pl.pallas_call(
        functools.partial(_stage_b_kernel, not per_seq),
        out_shape=jax.ShapeDtypeStruct(x3.shape, F32),
        grid=grid,
        in_specs=[pl.BlockSpec((kb, r, D_MODEL), tok), ada_spec, att_spec,
                  pl.BlockSpec((SUPER, n // CHUNK, CW), lambda i: (0, i, 0)),
                  _const_spec(g2.shape), _const_spec(wglu.shape), _const_spec(wout.shape),
                  _const_spec(wg.shape), _const_spec(wu.shape), _const_spec(wd.shape)],
        out_specs=pl.BlockSpec((kb, r, D_MODEL), tok),
        scratch_shapes=[pltpu.VMEM((SUPER, n, LANES), F32)],
        compiler_params=_params(("arbitrary",)),
        name="stage_b",
    )(x3, ada3, att, gs, g2, wglu, wout, wg, wu, wd)


def _layer(x_p, x_s, c_p, c_s, cache_k, cache_v, st_re, st_im, p):
    (w_ada, b_ada, g_ffn1, w1_gate, w1_up, w1_down, g_mix, w_in, g_q, g_k,
     a_re, a_im, log_dt, b_re, b_im, c_re, c_im, d_skip, w_glu, w_out,
     g_ffn2, w2_gate, w2_up, w2_down) = p
    bp, s, _ = x_p.shape
    bs, ns, _ = x_s.shape
    assert bp == 1 and ns == CHUNK and s % (WIN_STEPS * DILATIONS[-1]) == 0
    w_buf = cache_k.shape[2]
    tm = 512

    rows = bp + bs
    pad = (-rows) % 8
    c_all = jnp.concatenate([c_p, c_s, jnp.zeros((pad, D_MODEL), F32)], axis=0)
    ada = _ada(c_all, w_ada, b_ada)
    ada_p = ada[0:bp].reshape(bp, N_ADA, D_MODEL)
    ada_s = ada[bp:rows].reshape(bs, N_ADA, D_MODEL)

    vec = lambda g: g.reshape(1, 1, -1)
    head_gain = lambda g: jnp.tile(g, N_HEADS).reshape(1, D_ATT)
    seg = jnp.asarray(np.kron(np.eye(N_HEADS), np.full((HEAD_DIM, HEAD_DIM), 1.0 / HEAD_DIM)), BF16)
    bf = lambda w: w.astype(BF16)
    wts_a = (vec(g_ffn1), vec(g_mix), head_gain(g_q), head_gain(g_k), seg,
             bf(w1_gate), bf(w1_up), bf(w1_down), bf(w_in))
    wts_b = (vec(g_ffn2), bf(w_glu), bf(w_out), bf(w2_gate), bf(w2_up), bf(w2_down))

    x1_p, q_p, k_p, v_p, u_p = _stage_a(x_p, ada_p, 0, False, wts_a, tm)
    x1_s, q_s, k_s, v_s, u_s = _stage_a(x_s, ada_s, PAST_LEN, True, wts_a, tm)

    o_s, kwin_s, vwin_s, o_p = _attention(q_s, k_s, v_s, cache_k, cache_v, q_p, k_p, v_p)

    mats = _s5_prep(a_re, a_im, log_dt, b_re, b_im, c_re, c_im)
    dvec = jnp.tile(d_skip.reshape(SUPER, 1, LANES), (1, 1, CHUNK))
    gs_p, hfin_p = _s5_scan(u_p, mats, dvec, 256)
    gs_s, hre_s, him_s = _s5_step(u_s, st_re, st_im, mats, dvec)

    y_p = _stage_b(x1_p, ada_p, o_p, gs_p, False, wts_b, tm)
    y_s = _stage_b(x1_s, ada_s, o_s, gs_s, True, wts_b, tm)

    keep = min(WIN_MAX, s)
    tail = lambda a: jnp.transpose(a[:, s - keep:], (1, 0, 2)).reshape(bp, keep, N_HEADS, HEAD_DIM)
    kwin_p, vwin_p = tail(k_p), tail(v_p)
    hfin_p = hfin_p.reshape(SUPER, 2, SG_STATE)
    states = lambda h: h.reshape(-1, N_SSM_GROUPS, SSM_STATE)
    unflip = lambda a: jnp.transpose(a.reshape(bs, N_HEADS, HEAD_DIM, w_buf), (0, 3, 1, 2))
    return (y_p, y_s, kwin_p, vwin_p, states(hfin_p[:, 0]), states(hfin_p[:, 1]),
            unflip(kwin_s), unflip(vwin_s), states(hre_s), states(him_s))


def kernel(x_prompt, x_sample, c_prompt, c_sample, cache_k_win, cache_v_win, state_ssm_re, state_ssm_im, w_ada, b_ada, g_ffn1, w1_gate, w1_up, w1_down, g_mix, w_in, g_q, g_k, ssm_a_re, ssm_a_im, ssm_log_dt, ssm_b_re, ssm_b_im, ssm_c_re, ssm_c_im, ssm_d, w_glu, w_out, g_ffn2, w2_gate, w2_up, w2_down):
    depth = w_ada.shape[0]
    assert depth == 1
    bs = x_sample.shape[0]
    w_buf = cache_k_win.shape[2]
    p = tuple(a[0] for a in (w_ada, b_ada, g_ffn1, w1_gate, w1_up, w1_down, g_mix, w_in, g_q, g_k,
                             ssm_a_re, ssm_a_im, ssm_log_dt, ssm_b_re, ssm_b_im, ssm_c_re, ssm_c_im,
                             ssm_d, w_glu, w_out, g_ffn2, w2_gate, w2_up, w2_down))
    flip = lambda a: jnp.transpose(a[0], (0, 2, 3, 1)).reshape(bs, D_ATT, w_buf)
    outs = _layer(x_prompt, x_sample, c_prompt, c_sample, flip(cache_k_win), flip(cache_v_win),
                  state_ssm_re[0].reshape(bs, N_SSM_GROUPS * SSM_STATE),
                  state_ssm_im[0].reshape(bs, N_SSM_GROUPS * SSM_STATE), p)
    return tuple(o[None] if i >= 2 else o for i, o in enumerate(outs))
```

```python
import functools
import math

import numpy as np
import jax
import jax.numpy as jnp
from jax import lax
from jax.experimental import pallas as pl
from jax.experimental.pallas import tpu as pltpu

F32 = jnp.float32
BF16 = jnp.bfloat16

D_MODEL = 1024
D_ATT = 512
D_SSM = 512
HEAD_DIM = 64
N_HEADS = 8
ROT_DIM = 16
ROPE_THETA = 500000.0
DILATIONS = (1, 4, 16)
WIN_STEPS = 128
WIN_MAX = 2048
PAST_LEN = 8192
SSM_GROUP = 16
N_SSM_GROUPS = 32
SSM_STATE = 64
D_FF = 2816
N_ADA = 9
EPS = 1e-6

LANES = 128
CHUNK = 8
SUPER = D_SSM // LANES
SG_STATE = (LANES // SSM_GROUP) * SSM_STATE
CW = CHUNK * LANES
NEG = -1e30
VMEM_LIMIT = 56 * 1024 * 1024
RING_VMEM_LIMIT = 60 * 1024 * 1024

_NT = (((1,), (1,)), ((), ()))


def _params(sem, vmem=VMEM_LIMIT):
    return pltpu.CompilerParams(dimension_semantics=sem, vmem_limit_bytes=vmem)


def _const_spec(shape):
    nd = len(shape)
    return pl.BlockSpec(shape, lambda *_: (0,) * nd, pipeline_mode=pl.Buffered(1))


def _dot(a, b):
    return jnp.dot(a, b, preferred_element_type=F32)


def _rms(x, g):
    ms = jnp.mean(x * x, axis=-1, keepdims=True)
    return x * lax.rsqrt(ms + EPS) * g


def _swiglu(h, wg_ref, wu_ref, wd_ref):
    a = _dot(h, wg_ref[...])
    b = _dot(h, wu_ref[...])
    t = (a * (1.0 / (1.0 + jnp.exp(-a))) * b).astype(BF16)
    return _dot(t, wd_ref[...])


def _ada_kernel(c_ref, w_ref, b_ref, o_ref):
    c = c_ref[...]
    s = (c * (1.0 / (1.0 + jnp.exp(-c)))).astype(BF16)
    o_ref[...] = _dot(s, w_ref[...].astype(BF16)) + b_ref[...]


def _ada(c, w_ada, b_ada):
    m = c.shape[0]
    n = w_ada.shape[1]
    tn = n // 3
    return pl.pallas_call(
        _ada_kernel,
        out_shape=jax.ShapeDtypeStruct((m, n), F32),
        grid=(n // tn,),
        in_specs=[pl.BlockSpec((m, D_MODEL), lambda j: (0, 0)),
                  pl.BlockSpec((D_MODEL, tn), lambda j: (0, j)),
                  pl.BlockSpec((1, tn), lambda j: (0, j))],
        out_specs=pl.BlockSpec((m, tn), lambda j: (0, j)),
        compiler_params=_params(("arbitrary",)),
        name="ada",
    )(c, w_ada, b_ada.reshape(1, n))


def _stage_a_kernel(pair_major, keep_tiles, x_ref, ada_ref, base_ref, off_ref, lanes_ref, gf_ref,
                    gm_ref, gq_ref, gk_ref, seg_ref, wg_ref, wu_ref, wd_ref, win_ref,
                    x1_ref, q_ref, k_ref, v_ref, u_ref, *rest):
    kt_ref, vt_ref, u_scr = rest if pair_major else (None, None) + rest
    kb, r, _ = x_ref.shape
    n = kb * r
    x = x_ref[...]
    ada = ada_ref[...]
    sh1, sc1, gt1 = ada[:, 0:1], ada[:, 1:2], ada[:, 2:3]
    sh2, sc2 = ada[:, 3:4], ada[:, 4:5]

    h = _rms(x, gf_ref[...]) * (1.0 + sc1) + sh1
    f = _swiglu(h.reshape(n, D_MODEL).astype(BF16), wg_ref, wu_ref, wd_ref)
    x1 = x + 0.5 * gt1 * f.reshape(kb, r, D_MODEL)
    x1_ref[...] = x1

    h = _rms(x1, gm_ref[...]) * (1.0 + sc2) + sh2
    proj = _dot(h.reshape(n, D_MODEL).astype(BF16), win_ref[...])

    ca, sa = base_ref[0, 0:1, :], base_ref[0, 1:2, :]
    cb, sb = off_ref[0], off_ref[1]
    c, s = ca * cb - sa * sb, sa * cb + ca * sb
    rot, neg_lo, pos_hi = lanes_ref[0], lanes_ref[1], lanes_ref[2]
    tile4 = lambda t: jnp.concatenate([t] * (D_ATT // LANES), axis=-1)[None]
    cos, sina, sinb = tile4(c * rot + (1.0 - rot)), tile4(s * neg_lo), tile4(s * pos_hi)

    def head_norm_rope(z, g):
        z2 = (z * z).astype(BF16)
        sw = seg_ref.shape[0]
        ms = jnp.concatenate([_dot(z2[:, c:c + sw], seg_ref[...]) for c in range(0, D_ATT, sw)], axis=-1)
        zn = z * lax.rsqrt(ms + EPS) * g
        up = pltpu.roll(zn, D_ATT - ROT_DIM // 2, 1).reshape(kb, r, D_ATT)
        dn = pltpu.roll(zn, ROT_DIM // 2, 1).reshape(kb, r, D_ATT)
        return zn.reshape(kb, r, D_ATT) * cos + up * sina + dn * sinb

    q = head_norm_rope(proj[:, 0:D_ATT], gq_ref[...])
    k = head_norm_rope(proj[:, D_ATT:2 * D_ATT], gk_ref[...])
    v = proj[:, 2 * D_ATT:3 * D_ATT]
    if pair_major:
        k2 = k.reshape(n, D_ATT)
        for z, z_ref in ((q.reshape(n, D_ATT), q_ref), (k2, k_ref), (v, v_ref)):
            for hp in range(D_ATT // LANES):
                z_ref[hp] = z[:, hp * LANES:(hp + 1) * LANES]

        @pl.when(pl.program_id(0) >= pl.num_programs(0) - keep_tiles)
        def _():
            kt_ref[...] = k2.T
            vt_ref[...] = v.T
    else:
        q_ref[...] = q
        k_ref[...] = k
        v_ref[...] = v.reshape(kb, r, D_ATT)
    u = proj[:, 3 * D_ATT:]
    for sg in range(SUPER):
        u_scr[sg] = u[:, sg * LANES:(sg + 1) * LANES]
    for sg in range(SUPER):
        for s in range(CHUNK):
            u_ref[sg, :, s * LANES:(s + 1) * LANES] = u_scr[sg, pl.ds(s, n // CHUNK, stride=CHUNK), :]


def _rope_inputs(start, n_tiles, tile):
    half = ROT_DIM // 2
    j = np.arange(LANES) % HEAD_DIM
    inv = ROPE_THETA ** (-(j % half).astype(np.float64) / half)
    a = (start + tile * np.arange(n_tiles))[:, None] * inv[None, :]
    b = np.arange(tile)[:, None] * inv[None, :]
    lo = (j < half).astype(np.float32)
    hi = ((j >= half) & (j < ROT_DIM)).astype(np.float32)
    lanes = np.stack([lo + hi, -lo, hi])[:, None, :]
    base = np.stack([np.cos(a), np.sin(a)], axis=1).astype(np.float32)
    off = np.stack([np.cos(b), np.sin(b)]).astype(np.float32)
    return jnp.asarray(base), jnp.asarray(off), jnp.asarray(lanes)


def _stage_a(x3, ada3, pos0, per_seq, wts, tm, keep=0):
    b3, r3, _ = x3.shape
    if per_seq:
        kb, r = min(tm // r3, b3), r3
        grid = (b3 // kb,)
        tok = lambda i: (i, 0, 0)
        tab = lambda i: (0, 0, 0)
        ada_spec = pl.BlockSpec((kb, N_ADA, D_MODEL), lambda i: (i, 0, 0))
        base, off, lanes = _rope_inputs(pos0, 1, r)
    else:
        kb, r = 1, min(tm, r3)
        grid = (r3 // r,)
        tok = lambda i: (0, i, 0)
        tab = lambda i: (i, 0, 0)
        ada_spec = pl.BlockSpec((1, N_ADA, D_MODEL), lambda i: (0, 0, 0))
        base, off, lanes = _rope_inputs(pos0, r3 // r, r)
    n = kb * r
    ntok = b3 * r3
    gf, gm, gq, gk, seg, wg, wu, wd, win = wts
    base_spec = pl.BlockSpec((1, 2, LANES), tab)
    if per_seq:
        att_spec = pl.BlockSpec((kb, r, D_ATT), tok)
        att_shape = jax.ShapeDtypeStruct((b3, r3, D_ATT), F32)
    else:
        att_spec = pl.BlockSpec((D_ATT // LANES, n, LANES), lambda i: (0, i, 0))
        att_shape = jax.ShapeDtypeStruct((D_ATT // LANES, ntok, LANES), F32)
    out_shape = (jax.ShapeDtypeStruct(x3.shape, F32), att_shape, att_shape, att_shape,
                 jax.ShapeDtypeStruct((SUPER, ntok // CHUNK, CW), F32))
    out_specs = (pl.BlockSpec((kb, r, D_MODEL), tok), att_spec, att_spec, att_spec,
                 pl.BlockSpec((SUPER, n // CHUNK, CW), lambda i: (0, i, 0)))
    keep_tiles = 0
    if not per_seq:
        assert keep % r == 0
        keep_tiles = keep // r
        first = grid[0] - keep_tiles
        tail_spec = pl.BlockSpec((D_ATT, r), lambda i: (0, jnp.maximum(i - first, 0)))
        out_shape += (jax.ShapeDtypeStruct((D_ATT, keep), F32),) * 2
        out_specs += (tail_spec, tail_spec)
    return pl.pallas_call(
        functools.partial(_stage_a_kernel, not per_seq, keep_tiles),
        out_shape=out_shape,
        grid=grid,
        in_specs=[pl.BlockSpec((kb, r, D_MODEL), tok), ada_spec, base_spec,
                  _const_spec(off.shape), _const_spec(lanes.shape),
                  _const_spec(gf.shape), _const_spec(gm.shape), _const_spec(gq.shape),
                  _const_spec(gk.shape), _const_spec(seg.shape), _const_spec(wg.shape),
                  _const_spec(wu.shape), _const_spec(wd.shape), _const_spec(win.shape)],
        out_specs=out_specs,
        scratch_shapes=[pltpu.VMEM((SUPER, n, LANES), F32)],
        compiler_params=_params(("arbitrary",)),
        name="stage_a",
    )(x3, ada3, base, off, lanes, gf, gm, gq, gk, seg, wg, wu, wd, win)


UNITS_PER_ITER = 4


def _rows(start, size, stride):
    return pl.ds(start, size) if stride == 1 else pl.ds(start, size, stride=stride)


DIL_PARTS = 4
DIL_SHARE = len(DILATIONS) * DILATIONS[-1] // (UNITS_PER_ITER * DIL_PARTS)


def _dil_attn_part(i, part, side_work, q_ref, k_ref, v_ref, o_ref, kbuf, vbuf, acc, mrow, lrow):
    sb = q_ref.shape[1]
    tq = WIN_STEPS

    @pl.when(jnp.logical_and(part == 0, i == 0))
    def _():
        kbuf[0:sb, :] = jnp.zeros((sb, LANES), F32)
        vbuf[0:sb, :] = jnp.zeros((sb, LANES), F32)

    @pl.when(jnp.logical_and(part == 0, i > 0))
    def _():
        kbuf[0:sb, :] = kbuf[sb:2 * sb, :]
        vbuf[0:sb, :] = vbuf[sb:2 * sb, :]

    @pl.when(part == 0)
    def _():
        kbuf[sb:2 * sb, :] = k_ref[0]
        vbuf[sb:2 * sb, :] = v_ref[0]

    row = lax.broadcasted_iota(jnp.int32, (tq, 2 * tq), 0)
    col = lax.broadcasted_iota(jnp.int32, (tq, 2 * tq), 1)
    band = jnp.where(col >= row, 0.0, NEG)
    band = jnp.where(col <= row + tq, band, NEG)
    band0 = jnp.where(col >= tq, band, NEG)
    band_first = jnp.where(i == 0, band0, band)
    low = lax.broadcasted_iota(jnp.int32, (tq, LANES), 1) < HEAD_DIM
    high = jnp.logical_not(low)
    klow = lax.broadcasted_iota(jnp.int32, (2 * tq, LANES), 1) < HEAD_DIM
    ones_sel = jnp.concatenate([jnp.where(klow, 1.0, 0.0), jnp.where(klow, 0.0, 1.0)], axis=0).astype(BF16)

    def unit(d, qs, bias, mode):
        qp = (q_ref[0, _rows(qs, tq, d), :] * (HEAD_DIM ** -0.5)).astype(BF16)
        kp = kbuf[_rows(sb + qs - d * tq, 2 * tq, d), :].astype(BF16)
        vp = vbuf[_rows(sb + qs - d * tq, 2 * tq, d), :].astype(BF16)
        ps, mxs = [], []
        for sel in (low, high):
            qe = jnp.where(sel, qp, jnp.zeros_like(qp))
            s = lax.dot_general(qe, kp, _NT, preferred_element_type=F32) + bias
            mx = jnp.max(s, axis=1, keepdims=True)
            ps.append(jnp.exp(s - mx).astype(BF16))
            mxs.append(mx)
        zero = jnp.zeros_like(vp)
        vsel = jnp.concatenate([jnp.where(klow, vp, zero), jnp.where(klow, zero, vp)], axis=0)
        ol = _dot(jnp.concatenate(ps, axis=1), jnp.concatenate([vsel, ones_sel], axis=1))
        o, l = ol[:, 0:LANES], ol[:, LANES:]
        mx = jnp.where(low, mxs[0], mxs[1])
        rows = _rows(qs, tq, d)
        if mode != "init":
            m_old = mrow[rows, :]
            m_new = jnp.maximum(m_old, mx)
            a_old = jnp.exp(m_old - m_new)
            a_new = jnp.exp(mx - m_new)
            o = acc[rows, :] * a_old + o * a_new
            l = lrow[rows, :] * a_old + l * a_new
            mx = m_new
        if mode == "final":
            o_ref[0, rows, :] = o * (1.0 / l)
        else:
            acc[rows, :] = o
            mrow[rows, :] = mx
            lrow[rows, :] = l

    upi = UNITS_PER_ITER
    d16, d4 = DILATIONS[2], DILATIONS[1]
    nblk = sb // tq
    assert d16 == nblk and d4 % upi == 0

    def body16(g):
        for u in range(upi):
            unit(d16, g * upi + u, band_first, "init")

    def body4(g):
        mb = (g * upi) // d4
        bias = jnp.where(mb == 0, band_first, band)
        for u in range(upi):
            unit(d4, (g * upi) % d4 + u + d4 * tq * mb, bias, "merge")

    def body1(g):
        for u in range(upi):
            bias = jnp.where(g == 0, band_first, band) if u == 0 else band
            unit(1, (g * upi + u) * tq, bias, "final")

    counts = (nblk // upi,) * 3
    assert DIL_SHARE * DIL_PARTS == sum(counts)
    first = 0
    for body, cnt in zip((body16, body4, body1), counts):
        lo = jnp.clip(part * DIL_SHARE - first, 0, cnt)
        hi = jnp.clip((part + 1) * DIL_SHARE - first, 0, cnt)
        off = first - part * DIL_SHARE

        def wrapped(g, c, body=body, off=off):
            side_work(g + off)
            body(g)
            return c

        lax.fori_loop(lo, hi, wrapped, 0)
        first += cnt


def _sample_bias():
    w = WIN_MAX
    out = []
    i = (np.arange(N_HEADS * CHUNK) % CHUNK)[:, None]
    for d in DILATIONS:
        span = WIN_STEPS * d
        c = np.arange(w - span, w)[None, :]
        dist = w + i - c
        ok_buf = (dist % d == 0) & (dist <= span)
        cn = np.arange(LANES)[None, :] - (LANES - CHUNK)
        dn = i - cn
        ok_new = (cn >= 0) & (dn >= 0) & (dn % d == 0) & (dn <= span)
        ok = np.concatenate([ok_buf, ok_new], axis=1)
        out.append(jnp.asarray(np.where(ok, 0.0, NEG), F32))
    return out


SHIFT_HEAD_ROWS = 200
SHIFT_ROWS = -(-(D_ATT - SHIFT_HEAD_ROWS) // (8 * DIL_SHARE)) * 8


def _shift_rows(start, nrows, kc_ref, vc_ref, ko_ref, vo_ref, knt, vnt):
    w = kc_ref.shape[1]
    nq = CHUNK
    rows = pl.ds(start, nrows)
    lane = lax.broadcasted_iota(jnp.int32, (nrows, LANES), 1)
    for buf_ref, out_ref, new_ref in ((kc_ref, ko_ref, knt), (vc_ref, vo_ref, vnt)):
        rolled = pltpu.roll(buf_ref[rows, :], w - nq, 1)
        out_ref[0, rows, 0:w - LANES] = rolled[:, 0:w - LANES]
        out_ref[0, rows, w - LANES:] = jnp.where(lane >= LANES - nq, new_ref[rows, :],
                                                 rolled[:, w - LANES:])


def _sample_scores(q_ref, kn_ref, vn_ref, kc_ref, vc_ref, b1_ref, b2_ref, b3_ref, o_ref, knt, vnt):
    w = kc_ref.shape[1]
    nq = q_ref.shape[1]
    kc = kc_ref[...]
    vc = vc_ref[...]
    pad = jnp.zeros((LANES - nq, D_ATT), F32)
    knt[...] = jnp.concatenate([pad, kn_ref[0]], axis=0).T
    vnt[...] = jnp.concatenate([pad, vn_ref[0]], axis=0).T
    knb = knt[...].astype(BF16)
    vnb = vnt[...].astype(BF16)

    rows = N_HEADS * nq
    q = q_ref[0] * (HEAD_DIM ** -0.5)
    qt = jnp.concatenate([q] * N_HEADS, axis=0)
    rhead = lax.broadcasted_iota(jnp.int32, (rows, D_ATT), 0) // nq
    lhead = lax.broadcasted_iota(jnp.int32, (rows, D_ATT), 1) // HEAD_DIM
    qe = jnp.where(rhead == lhead, qt, 0.0).astype(BF16)
    kcb = kc.astype(BF16)
    vcb = vc.astype(BF16)
    s_buf = _dot(qe, kcb)
    s_new = _dot(qe, knb)

    outs, lses = [], []
    for d, b_ref in zip(DILATIONS, (b1_ref, b2_ref, b3_ref)):
        span = WIN_STEPS * d
        s = jnp.concatenate([s_buf[:, w - span:], s_new], axis=1) + b_ref[...]
        mx = jnp.max(s, axis=1, keepdims=True)
        p = jnp.exp(s - mx)
        l = jnp.sum(p, axis=1, keepdims=True)
        vcat = jnp.concatenate([vcb[:, w - span:], vnb], axis=1)
        o = lax.dot_general(p.astype(BF16), vcat, _NT, preferred_element_type=F32)
        outs.append(o * (1.0 / l))
        lses.append(mx + jnp.log(l))
    lmax = jnp.maximum(jnp.maximum(lses[0], lses[1]), lses[2])
    es = [jnp.exp(l - lmax) for l in lses]
    o = (es[0] * outs[0] + es[1] * outs[1] + es[2] * outs[2]) * (1.0 / (es[0] + es[1] + es[2]))
    o = jnp.where(rhead == lhead, o, 0.0)
    acc = o[0:nq]
    for h in range(1, N_HEADS):
        acc = acc + o[h * nq:(h + 1) * nq]
    o_ref[0] = acc


RING = 3


def _attention_kernel(nseq, nsteps, nsb, q_ref, kn_ref, vn_ref, kc_hbm, vc_hbm, b1_ref, b2_ref, b3_ref,
                      qp_ref, kp_ref, vp_ref, o_ref, ko_ref, vo_ref, op_ref,
                      kbuf, vbuf, acc, mrow, lrow, knt, vnt, kring, vring, sems):
    j = pl.program_id(0)

    def fetch(step):
        seq = jnp.minimum(step, nseq - 1)
        slot = step % RING
        return (pltpu.make_async_copy(kc_hbm.at[seq], kring.at[slot], sems.at[0, slot]),
                pltpu.make_async_copy(vc_hbm.at[seq], vring.at[slot], sems.at[1, slot]))

    @pl.when(j == 0)
    def _():
        for step in range(min(RING - 1, nsteps)):
            for cp in fetch(step):
                cp.start()

    @pl.when(j + RING - 1 < nsteps)
    def _():
        for cp in fetch(j + RING - 1):
            cp.start()

    for cp in fetch(j):
        cp.wait()
    kc_ref, vc_ref = kring.at[j % RING], vring.at[j % RING]
    _sample_scores(q_ref, kn_ref, vn_ref, kc_ref, vc_ref, b1_ref, b2_ref, b3_ref, o_ref, knt, vnt)
    shift = functools.partial(_shift_rows, kc_ref=kc_ref, vc_ref=vc_ref, ko_ref=ko_ref,
                              vo_ref=vo_ref, knt=knt, vnt=vnt)
    shift(0, SHIFT_HEAD_ROWS)

    def side_work(it):
        start = jnp.minimum(SHIFT_HEAD_ROWS + it * SHIFT_ROWS, D_ATT - SHIFT_ROWS)
        shift(pl.multiple_of(start, 8), SHIFT_ROWS)

    _dil_attn_part((j // DIL_PARTS) % nsb, j % DIL_PARTS, side_work, qp_ref, kp_ref, vp_ref, op_ref,
                   kbuf, vbuf, acc, mrow, lrow)


def _attention(q, kn, vn, kc, vc, qp, kp, vp):
    b, nq, _ = q.shape
    w = kc.shape[2]
    npair, s, _ = qp.shape
    sb = WIN_STEPS * DILATIONS[-1]
    assert w == WIN_MAX and nq == CHUNK and s % sb == 0 and DILATIONS[0] == 1
    nsb = s // sb
    ndil = npair * nsb * DIL_PARTS
    assert b <= ndil, "every sample sequence needs a grid step of the prompt attention"
    b1, b2, b3 = _sample_bias()
    seq = lambda j: (jnp.minimum(j, b - 1), 0, 0)
    new = pl.BlockSpec((1, nq, D_ATT), seq)
    buf = pl.BlockSpec((1, D_ATT, w), seq)
    blk = pl.BlockSpec((1, sb, LANES), lambda j: (j // (DIL_PARTS * nsb), (j // DIL_PARTS) % nsb, 0))
    hbm = pl.BlockSpec(memory_space=pl.ANY)
    return pl.pallas_call(
        functools.partial(_attention_kernel, b, ndil, nsb),
        out_shape=(jax.ShapeDtypeStruct(q.shape, F32), jax.ShapeDtypeStruct(kc.shape, F32),
                   jax.ShapeDtypeStruct(vc.shape, F32), jax.ShapeDtypeStruct(qp.shape, F32)),
        grid=(ndil,),
        in_specs=[new, new, new, hbm, hbm, _const_spec(b1.shape), _const_spec(b2.shape),
                  _const_spec(b3.shape), blk, blk, blk],
        out_specs=(new, buf, buf, blk),
        scratch_shapes=[pltpu.VMEM((2 * sb, LANES), F32), pltpu.VMEM((2 * sb, LANES), F32),
                        pltpu.VMEM((sb, LANES), F32), pltpu.VMEM((sb, LANES), F32),
                        pltpu.VMEM((sb, LANES), F32), pltpu.VMEM((D_ATT, LANES), F32),
                        pltpu.VMEM((D_ATT, LANES), F32), pltpu.VMEM((RING, D_ATT, w), F32),
                        pltpu.VMEM((RING, D_ATT, w), F32), pltpu.SemaphoreType.DMA((2, RING))],
        compiler_params=_params(("arbitrary",), RING_VMEM_LIMIT),
        name="attention",
    )(q, kn, vn, kc, vc, b1, b2, b3, qp, kp, vp)


def _split(a):
    hi = a.astype(BF16)
    return hi, (a - hi.astype(F32)).astype(BF16)


def _s5_prep_kernel(vec_ref, mat_ref, p_ref, qt_ref, t_ref, a_ref, pf_scr):
    ar, ai = vec_ref[0, 0:1, :], vec_ref[0, 1:2, :]
    dt = jnp.exp(vec_ref[0, 2:3, :])
    bre_ref, bim_ref, cre_ref, cim_ref = (mat_ref.at[m] for m in range(4))
    mag = jnp.exp(ar * dt)
    lr, li = mag * jnp.cos(ai * dt), mag * jnp.sin(ai * dt)
    den = 1.0 / (ar * ar + ai * ai)
    kr = ((lr - 1.0) * ar + li * ai) * den
    ki = (li * ar - (lr - 1.0) * ai) * den
    bre, bim = bre_ref[0], bim_ref[0]
    bbr, bbi = kr * bre - ki * bim, kr * bim + ki * bre
    cre, cim = cre_ref[0], cim_ref[0]
    pr, pi = [jnp.ones_like(lr)], [jnp.zeros_like(lr)]
    for _ in range(CHUNK):
        pr.append(pr[-1] * lr - pi[-1] * li)
        pi.append(pr[-2] * li + pi[-1] * lr)
    for s in range(CHUNK):
        n = CHUNK - 1 - s
        rows = slice(s * LANES, (s + 1) * LANES)
        pf_scr[rows, 0:SG_STATE] = bbr * pr[n] - bbi * pi[n]
        pf_scr[rows, SG_STATE:] = bbr * pi[n] + bbi * pr[n]
        n = s + 1
        re, im = cre * pr[n] - cim * pi[n], cre * pi[n] + cim * pr[n]
        qt_ref[0, rows, 0:SG_STATE] = re.astype(BF16)
        qt_ref[0, rows, SG_STATE:] = (-im).astype(BF16)
    pf = pf_scr[...]
    p_ref[0] = pf.astype(BF16)
    a_ref[0] = jnp.concatenate([pr[CHUNK], pi[CHUNK]], axis=-1)
    ph, plo = _split(pf)
    ch, clo = _split(jnp.concatenate([cre, -cim], axis=-1))
    lags = (lax.dot_general(ph, ch, _NT, preferred_element_type=F32)
            + lax.dot_general(ph, clo, _NT, preferred_element_type=F32)
            + lax.dot_general(plo, ch, _NT, preferred_element_type=F32)).astype(BF16)
    for tau in range(CHUNK):
        cols = slice(tau * LANES, (tau + 1) * LANES)
        nr = (tau + 1) * LANES
        t_ref[0, 0:nr, cols] = lags[CW - nr:, :]
        if nr < CW:
            t_ref[0, nr:, cols] = jnp.zeros((CW - nr, LANES), BF16)


def _block_diag(a):
    g = LANES // SSM_GROUP
    m, _, r, c = a.shape
    a = a.reshape(m, SUPER, g, r, c)
    eye = jnp.eye(g, dtype=a.dtype)
    return jnp.einsum("msgrc,gh->msgrhc", a, eye).reshape(m, SUPER, g * r, g * c)


def _s5_prep(a_re, a_im, log_dt, b_re, b_im, c_re, c_im):
    ldt = jnp.broadcast_to(log_dt[:, None], (N_SSM_GROUPS, SSM_STATE))
    vecs = jnp.stack([a.reshape(SUPER, SG_STATE) for a in (a_re, a_im, ldt)], axis=1)
    mats = _block_diag(jnp.stack([jnp.swapaxes(b_re, 1, 2), jnp.swapaxes(b_im, 1, 2), c_re, c_im]))
    vec = pl.BlockSpec((1, 3, SG_STATE), lambda g: (g, 0, 0))
    mat = pl.BlockSpec((4, 1, LANES, SG_STATE), lambda g: (0, g, 0, 0))
    big = pl.BlockSpec((1, CW, CW), lambda g: (g, 0, 0))
    big_shape = jax.ShapeDtypeStruct((SUPER, CW, CW), BF16)
    return pl.pallas_call(
        _s5_prep_kernel,
        out_shape=(big_shape, big_shape, big_shape, jax.ShapeDtypeStruct((SUPER, 1, CW), F32)),
        grid=(SUPER,),
        in_specs=[vec, mat],
        out_specs=(big, big, big, pl.BlockSpec((1, 1, CW), lambda g: (g, 0, 0))),
        scratch_shapes=[pltpu.VMEM((CW, CW), F32)],
        compiler_params=_params(("arbitrary",)),
        name="s5_prep",
    )(vecs, mats)


def _gelu_tanh(y):
    return 0.5 * y * (1.0 + jnp.tanh(math.sqrt(2.0 / math.pi) * (y + 0.044715 * (y * y * y))))


def _s5_out(u, ub, hin, qt, t, d):
    y = _dot(ub, t) + lax.dot_general(hin.astype(BF16), qt, _NT, preferred_element_type=F32) + d * u
    return _gelu_tanh(y)


def _s5_scan_kernel(u_ref, p_ref, qt_ref, t_ref, a_ref, d_ref, y_ref, hout_ref,
                    h_scr, gs_scr, hin_scr):
    tc = u_ref.shape[1]

    @pl.when(pl.program_id(0) == 0)
    def _():
        h_scr[...] = jnp.zeros_like(h_scr)

    ubs = [u_ref[sg].astype(BF16) for sg in range(SUPER)]
    for sg in range(SUPER):
        gs_scr[sg] = _dot(ubs[sg], p_ref[sg])
    ar = [a_ref[sg][:, 0:SG_STATE] for sg in range(SUPER)]
    ai = [a_ref[sg][:, SG_STATE:] for sg in range(SUPER)]
    hr = [h_scr[sg][:, 0:SG_STATE] for sg in range(SUPER)]
    hi = [h_scr[sg][:, SG_STATE:] for sg in range(SUPER)]
    for k in range(tc):
        for sg in range(SUPER):
            hin_scr[sg, k:k + 1, 0:SG_STATE] = hr[sg]
            hin_scr[sg, k:k + 1, SG_STATE:] = hi[sg]
            g = gs_scr[sg, k:k + 1, :]
            hr[sg], hi[sg] = (ar[sg] * hr[sg] - ai[sg] * hi[sg] + g[:, 0:SG_STATE],
                              ar[sg] * hi[sg] + ai[sg] * hr[sg] + g[:, SG_STATE:])
    for sg in range(SUPER):
        h_new = jnp.concatenate([hr[sg], hi[sg]], axis=-1)
        h_scr[sg] = h_new
        hout_ref[sg] = h_new
        y_ref[sg] = _s5_out(u_ref[sg], ubs[sg], hin_scr[sg], qt_ref[sg], t_ref[sg], d_ref[sg])


def _s5_scan(u, mats, dvec, tc):
    p, qt, t, a = mats
    nch = u.shape[1]
    tc = min(tc, nch)
    tile = pl.BlockSpec((SUPER, tc, CW), lambda i: (0, i, 0))
    return pl.pallas_call(
        _s5_scan_kernel,
        out_shape=(jax.ShapeDtypeStruct(u.shape, F32), jax.ShapeDtypeStruct((SUPER, 1, CW), F32)),
        grid=(nch // tc,),
        in_specs=[tile, _const_spec(p.shape), _const_spec(qt.shape), _const_spec(t.shape),
                  _const_spec(a.shape), _const_spec(dvec.shape)],
        out_specs=(tile, _const_spec((SUPER, 1, CW))),
        scratch_shapes=[pltpu.VMEM((SUPER, 1, CW), F32), pltpu.VMEM((SUPER, tc, CW), F32),
                        pltpu.VMEM((SUPER, tc, CW), F32)],
        compiler_params=_params(("arbitrary",)),
        name="s5_scan",
    )(u, p, qt, t, a, dvec)


def _s5_step_kernel(u_ref, hr_ref, hi_ref, p_ref, qt_ref, t_ref, a_ref, d_ref,
                    y_ref, hro_ref, hio_ref):
    u = u_ref[0]
    ub = u.astype(BF16)
    hr, hi = hr_ref[...].T, hi_ref[...].T
    a = a_ref[0]
    ar, ai = a[:, 0:SG_STATE], a[:, SG_STATE:]
    gs = _dot(ub, p_ref[0])
    hro_ref[...] = (ar * hr - ai * hi + gs[:, 0:SG_STATE]).T
    hio_ref[...] = (ar * hi + ai * hr + gs[:, SG_STATE:]).T
    y_ref[0] = _s5_out(u, ub, jnp.concatenate([hr, hi], axis=-1), qt_ref[0], t_ref[0], d_ref[0])


def _s5_step(u, h_re, h_im, mats, dvec):
    p, qt, t, a = mats
    b = u.shape[1]
    tile = pl.BlockSpec((1, b, CW), lambda g: (g, 0, 0))
    st = pl.BlockSpec((SG_STATE, b), lambda g: (g, 0))
    big = pl.BlockSpec((1, CW, CW), lambda g: (g, 0, 0))
    vec = pl.BlockSpec((1, 1, CW), lambda g: (g, 0, 0))
    st_shape = jax.ShapeDtypeStruct(h_re.shape, F32)
    return pl.pallas_call(
        _s5_step_kernel,
        out_shape=(jax.ShapeDtypeStruct(u.shape, F32), st_shape, st_shape),
        grid=(SUPER,),
        in_specs=[tile, st, st, big, big, big, vec, vec],
        out_specs=(tile, st, st),
        compiler_params=_params(("arbitrary",)),
        name="s5_step",
    )(u, h_re, h_im, p, qt, t, a, dvec)


def _stage_b_kernel(pair_major, x_ref, ada_ref, att_ref, gs_ref, g2_ref, wglu_ref, wout_ref,
                    wg_ref, wu_ref, wd_ref, y_ref, g_scr):
    kb, r, _ = x_ref.shape
    n = kb * r
    x = x_ref[...]
    ada = ada_ref[...]
    gt2 = ada[:, 5:6]
    sh3, sc3, gt3 = ada[:, 6:7], ada[:, 7:8], ada[:, 8:9]

    if pair_major:
        o_att = jnp.concatenate([att_ref[hp] for hp in range(D_ATT // LANES)], axis=-1)
    else:
        o_att = att_ref[...].reshape(n, D_ATT)

    for sg in range(SUPER):
        for s in range(CHUNK):
            g_scr[sg, pl.ds(s, n // CHUNK, stride=CHUNK), :] = gs_ref[sg, :, s * LANES:(s + 1) * LANES]
    g = jnp.concatenate([g_scr[sg] for sg in range(SUPER)], axis=-1)
    gl = _dot(g.astype(BF16), wglu_ref[...])
    o_ssm = gl[:, 0:D_SSM] * (1.0 / (1.0 + jnp.exp(-gl[:, D_SSM:])))
    mix = _dot(jnp.concatenate([o_att, o_ssm], axis=-1).astype(BF16), wout_ref[...])
    x2 = x + gt2 * mix.reshape(kb, r, D_MODEL)

    h = _rms(x2, g2_ref[...]) * (1.0 + sc3) + sh3
    f = _swiglu(h.reshape(n, D_MODEL).astype(BF16), wg_ref, wu_ref, wd_ref)
    y_ref[...] = x2 + 0.5 * gt3 * f.reshape(kb, r, D_MODEL)


def _stage_b(x3, ada3, att, gs, per_seq, wts, tm):
    b3, r3, _ = x3.shape
    if per_seq:
        kb, r = min(tm // r3, b3), r3
        grid = (b3 // kb,)
        tok = lambda i: (i, 0, 0)
        ada_spec = pl.BlockSpec((kb, N_ADA, D_MODEL), lambda i: (i, 0, 0))
    else:
        kb, r = 1, min(tm, r3)
        grid = (r3 // r,)
        tok = lambda i: (0, i, 0)
        ada_spec = pl.BlockSpec((1, N_ADA, D_MODEL), lambda i: (0, 0, 0))
    n = kb * r
    g2, wglu, wout, wg, wu, wd = wts
    if per_seq:
        att_spec = pl.BlockSpec((kb, r, D_ATT), tok)
    else:
        att_spec = pl.BlockSpec((D_ATT // LANES, n, LANES), lambda i: (0, i, 0))
    return pl.pallas_call(
        functools.partial(_stage_b_kernel, not per_seq),
        out_shape=jax.ShapeDtypeStruct(x3.shape, F32),
        grid=grid,
        in_specs=[pl.BlockSpec((kb, r, D_MODEL), tok), ada_spec, att_spec,
                  pl.BlockSpec((SUPER, n // CHUNK, CW), lambda i: (0, i, 0)),
                  _const_spec(g2.shape), _const_spec(wglu.shape), _const_spec(wout.shape),
                  _const_spec(wg.shape), _const_spec(wu.shape), _const_spec(wd.shape)],
        out_specs=pl.BlockSpec((kb, r, D_MODEL), tok),
        scratch_shapes=[pltpu.VMEM((SUPER, n, LANES), F32)],
        compiler_params=_params(("arbitrary",)),
        name="stage_b",
    )(x3, ada3, att, gs, g2, wglu, wout, wg, wu, wd)


def _layer(x_p, x_s, c_p, c_s, cache_k, cache_v, st_re, st_im, p):
    (w_ada, b_ada, g_ffn1, w1_gate, w1_up, w1_down, g_mix, w_in, g_q, g_k,
     a_re, a_im, log_dt, b_re, b_im, c_re, c_im, d_skip, w_glu, w_out,
     g_ffn2, w2_gate, w2_up, w2_down) = p
    bp, s, _ = x_p.shape
    bs, ns, _ = x_s.shape
    assert bp == 1 and ns == CHUNK and s % (WIN_STEPS * DILATIONS[-1]) == 0
    w_buf = cache_k.shape[2]
    tm = 512

    rows = bp + bs
    pad = (-rows) % 8
    c_all = jnp.concatenate([c_p, c_s, jnp.zeros((pad, D_MODEL), F32)], axis=0)
    ada = _ada(c_all, w_ada, b_ada)
    ada_p = ada[0:bp].reshape(bp, N_ADA, D_MODEL)
    ada_s = ada[bp:rows].reshape(bs, N_ADA, D_MODEL)

    vec = lambda g: g.reshape(1, 1, -1)
    head_gain = lambda g: jnp.tile(g, N_HEADS).reshape(1, D_ATT)
    seg = jnp.asarray(np.kron(np.eye(2 * LANES // HEAD_DIM),
                              np.full((HEAD_DIM, HEAD_DIM), 1.0 / HEAD_DIM)), BF16)
    bf = lambda w: w.astype(BF16)
    wts_a = (vec(g_ffn1), vec(g_mix), head_gain(g_q), head_gain(g_k), seg,
             bf(w1_gate), bf(w1_up), bf(w1_down), bf(w_in))
    wts_b = (vec(g_ffn2), bf(w_glu), bf(w_out), bf(w2_gate), bf(w2_up), bf(w2_down))

    keep = min(WIN_MAX, s)
    x1_p, q_p, k_p, v_p, u_p, kt_p, vt_p = _stage_a(x_p, ada_p, 0, False, wts_a, tm, keep)
    x1_s, q_s, k_s, v_s, u_s = _stage_a(x_s, ada_s, PAST_LEN, True, wts_a, tm)

    o_s, kwin_s, vwin_s, o_p = _attention(q_s, k_s, v_s, cache_k, cache_v, q_p, k_p, v_p)

    mats = _s5_prep(a_re, a_im, log_dt, b_re, b_im, c_re, c_im)
    dvec = jnp.tile(d_skip.reshape(SUPER, 1, LANES), (1, 1, CHUNK))
    gs_p, hfin_p = _s5_scan(u_p, mats, dvec, 256)
    gs_s, hre_s, him_s = _s5_step(u_s, st_re, st_im, mats, dvec)

    y_p = _stage_b(x1_p, ada_p, o_p, gs_p, False, wts_b, tm)
    y_s = _stage_b(x1_s, ada_s, o_s, gs_s, True, wts_b, tm)

    unflip = lambda a: jnp.transpose(a.reshape(-1, N_HEADS, HEAD_DIM, a.shape[-1]), (0, 3, 1, 2))
    hfin_p = hfin_p.reshape(SUPER, 2, SG_STATE)
    states = lambda h: h.reshape(-1, N_SSM_GROUPS, SSM_STATE)
    states_t = lambda h: jnp.transpose(h.reshape(N_SSM_GROUPS, SSM_STATE, bs), (2, 0, 1))
    return (y_p, y_s, unflip(kt_p[None]), unflip(vt_p[None]), states(hfin_p[:, 0]), states(hfin_p[:, 1]),
            unflip(kwin_s), unflip(vwin_s), states_t(hre_s), states_t(him_s))


def kernel(x_prompt, x_sample, c_prompt, c_sample, cache_k_win, cache_v_win, state_ssm_re, state_ssm_im, w_ada, b_ada, g_ffn1, w1_gate, w1_up, w1_down, g_mix, w_in, g_q, g_k, ssm_a_re, ssm_a_im, ssm_log_dt, ssm_b_re, ssm_b_im, ssm_c_re, ssm_c_im, ssm_d, w_glu, w_out, g_ffn2, w2_gate, w2_up, w2_down):
    depth = w_ada.shape[0]
    assert depth == 1
    bs = x_sample.shape[0]
    w_buf = cache_k_win.shape[2]
    p = tuple(a[0] for a in (w_ada, b_ada, g_ffn1, w1_gate, w1_up, w1_down, g_mix, w_in, g_q, g_k,
                             ssm_a_re, ssm_a_im, ssm_log_dt, ssm_b_re, ssm_b_im, ssm_c_re, ssm_c_im,
                             ssm_d, w_glu, w_out, g_ffn2, w2_gate, w2_up, w2_down))
    flip = lambda a: jnp.transpose(a[0], (0, 2, 3, 1)).reshape(bs, D_ATT, w_buf)
    state_t = lambda a: jnp.transpose(a[0], (1, 2, 0)).reshape(N_SSM_GROUPS * SSM_STATE, bs)
    outs = _layer(x_prompt, x_sample, c_prompt, c_sample, flip(cache_k_win), flip(cache_v_win),
                  state_t(state_ssm_re), state_t(state_ssm_im), p)
    return tuple(o[None] if i >= 2 else o for i, o in enumerate(outs))
```

```python
import functools
import math

import numpy as np
import jax
import jax.numpy as jnp
from jax import lax
from jax.experimental import pallas as pl
from jax.experimental.pallas import tpu as pltpu

F32 = jnp.float32
BF16 = jnp.bfloat16

D_MODEL = 1024
D_ATT = 512
D_SSM = 512
HEAD_DIM = 64
N_HEADS = 8
ROT_DIM = 16
ROPE_THETA = 500000.0
DILATIONS = (1, 4, 16)
WIN_STEPS = 128
WIN_MAX = 2048
PAST_LEN = 8192
SSM_GROUP = 16
N_SSM_GROUPS = 32
SSM_STATE = 64
D_FF = 2816
N_ADA = 9
EPS = 1e-6

LANES = 128
CHUNK = 8
SUPER = D_SSM // LANES
SG_STATE = (LANES // SSM_GROUP) * SSM_STATE
CW = CHUNK * LANES
NEG = -1e30
VMEM_LIMIT = 56 * 1024 * 1024
RING_VMEM_LIMIT = 60 * 1024 * 1024

_NT = (((1,), (1,)), ((), ()))


def _params(sem, vmem=VMEM_LIMIT):
    return pltpu.CompilerParams(dimension_semantics=sem, vmem_limit_bytes=vmem)


def _const_spec(shape):
    nd = len(shape)
    return pl.BlockSpec(shape, lambda *_: (0,) * nd, pipeline_mode=pl.Buffered(1))


def _dot(a, b):
    return jnp.dot(a, b, preferred_element_type=F32)


def _rms(x, g):
    ms = jnp.mean(x * x, axis=-1, keepdims=True)
    return x * lax.rsqrt(ms + EPS) * g


def _swiglu(h, wg_ref, wu_ref, wd_ref):
    a = _dot(h, wg_ref[...])
    b = _dot(h, wu_ref[...])
    t = (a * (1.0 / (1.0 + jnp.exp(-a))) * b).astype(BF16)
    return _dot(t, wd_ref[...])


def _cast_kernel(*refs):
    n = len(refs) // 2
    for src, dst in zip(refs[:n], refs[n:]):
        dst[...] = src[...].astype(BF16)


def _cast_bf16(ws, steps=8):
    specs = [pl.BlockSpec((w.shape[0] // steps, w.shape[1]), lambda i: (i, 0)) for w in ws]
    assert all(w.shape[0] % (16 * steps) == 0 for w in ws)
    return pl.pallas_call(
        _cast_kernel,
        out_shape=[jax.ShapeDtypeStruct(w.shape, BF16) for w in ws],
        grid=(steps,),
        in_specs=specs,
        out_specs=specs,
        compiler_params=_params(("arbitrary",)),
        name="cast_weights",
    )(*ws)


def _ada_kernel(c_ref, w_ref, b_ref, o_ref):
    c = c_ref[...]
    s = (c * (1.0 / (1.0 + jnp.exp(-c)))).astype(BF16)
    o_ref[...] = _dot(s, w_ref[...].astype(BF16)) + b_ref[...]


def _ada(c, w_ada, b_ada):
    m = c.shape[0]
    n = w_ada.shape[1]
    tn = n // 3
    return pl.pallas_call(
        _ada_kernel,
        out_shape=jax.ShapeDtypeStruct((m, n), F32),
        grid=(n // tn,),
        in_specs=[pl.BlockSpec((m, D_MODEL), lambda j: (0, 0)),
                  pl.BlockSpec((D_MODEL, tn), lambda j: (0, j)),
                  pl.BlockSpec((1, tn), lambda j: (0, j))],
        out_specs=pl.BlockSpec((m, tn), lambda j: (0, j)),
        compiler_params=_params(("arbitrary",)),
        name="ada",
    )(c, w_ada, b_ada.reshape(1, n))


def _stage_a_kernel(pair_major, keep_tiles, x_ref, ada_ref, base_ref, off_ref, lanes_ref, gf_ref,
                    gm_ref, gq_ref, gk_ref, seg_ref, wg_ref, wu_ref, wd_ref, win_ref,
                    x1_ref, q_ref, k_ref, v_ref, u_ref, *rest):
    kt_ref, vt_ref, u_scr = rest if pair_major else (None, None) + rest
    kb, r, _ = x_ref.shape
    n = kb * r
    x = x_ref[...]
    ada = ada_ref[...]
    sh1, sc1, gt1 = ada[:, 0:1], ada[:, 1:2], ada[:, 2:3]
    sh2, sc2 = ada[:, 3:4], ada[:, 4:5]

    h = _rms(x, gf_ref[...]) * (1.0 + sc1) + sh1
    f = _swiglu(h.reshape(n, D_MODEL).astype(BF16), wg_ref, wu_ref, wd_ref)
    x1 = x + 0.5 * gt1 * f.reshape(kb, r, D_MODEL)
    x1_ref[...] = x1

    h = _rms(x1, gm_ref[...]) * (1.0 + sc2) + sh2
    proj = _dot(h.reshape(n, D_MODEL).astype(BF16), win_ref[...])

    ca, sa = base_ref[0, 0:1, :], base_ref[0, 1:2, :]
    cb, sb = off_ref[0], off_ref[1]
    c, s = ca * cb - sa * sb, sa * cb + ca * sb
    rot, neg_lo, pos_hi = lanes_ref[0], lanes_ref[1], lanes_ref[2]
    tile4 = lambda t: jnp.concatenate([t] * (D_ATT // LANES), axis=-1)[None]
    cos, sina, sinb = tile4(c * rot + (1.0 - rot)), tile4(s * neg_lo), tile4(s * pos_hi)

    def head_norm_rope(z, g):
        z2 = (z * z).astype(BF16)
        sw = seg_ref.shape[0]
        ms = jnp.concatenate([_dot(z2[:, c:c + sw], seg_ref[...]) for c in range(0, D_ATT, sw)], axis=-1)
        zn = z * lax.rsqrt(ms + EPS) * g
        up = pltpu.roll(zn, D_ATT - ROT_DIM // 2, 1).reshape(kb, r, D_ATT)
        dn = pltpu.roll(zn, ROT_DIM // 2, 1).reshape(kb, r, D_ATT)
        return zn.reshape(kb, r, D_ATT) * cos + up * sina + dn * sinb

    q = head_norm_rope(proj[:, 0:D_ATT], gq_ref[...])
    k = head_norm_rope(proj[:, D_ATT:2 * D_ATT], gk_ref[...])
    v = proj[:, 2 * D_ATT:3 * D_ATT]
    if pair_major:
        k2 = k.reshape(n, D_ATT)
        for z, z_ref in ((q.reshape(n, D_ATT), q_ref), (k2, k_ref), (v, v_ref)):
            for hp in range(D_ATT // LANES):
                z_ref[hp] = z[:, hp * LANES:(hp + 1) * LANES]

        @pl.when(pl.program_id(0) >= pl.num_programs(0) - keep_tiles)
        def _():
            kt_ref[...] = k2.T
            vt_ref[...] = v.T
    else:
        q_ref[...] = q
        k_ref[...] = k
        v_ref[...] = v.reshape(kb, r, D_ATT)
    u = proj[:, 3 * D_ATT:]
    for sg in range(SUPER):
        u_scr[sg] = u[:, sg * LANES:(sg + 1) * LANES]
    for sg in range(SUPER):
        for s in range(CHUNK):
            u_ref[sg, :, s * LANES:(s + 1) * LANES] = u_scr[sg, pl.ds(s, n // CHUNK, stride=CHUNK), :]


def _rope_inputs(start, n_tiles, tile):
    half = ROT_DIM // 2
    j = np.arange(LANES) % HEAD_DIM
    inv = ROPE_THETA ** (-(j % half).astype(np.float64) / half)
    a = (start + tile * np.arange(n_tiles))[:, None] * inv[None, :]
    b = np.arange(tile)[:, None] * inv[None, :]
    lo = (j < half).astype(np.float32)
    hi = ((j >= half) & (j < ROT_DIM)).astype(np.float32)
    lanes = np.stack([lo + hi, -lo, hi])[:, None, :]
    base = np.stack([np.cos(a), np.sin(a)], axis=1).astype(np.float32)
    off = np.stack([np.cos(b), np.sin(b)]).astype(np.float32)
    return jnp.asarray(base), jnp.asarray(off), jnp.asarray(lanes)


def _stage_a(x3, ada3, pos0, per_seq, wts, tm, keep=0):
    b3, r3, _ = x3.shape
    if per_seq:
        kb, r = min(tm // r3, b3), r3
        grid = (b3 // kb,)
        tok = lambda i: (i, 0, 0)
        tab = lambda i: (0, 0, 0)
        ada_spec = pl.BlockSpec((kb, N_ADA, D_MODEL), lambda i: (i, 0, 0))
        base, off, lanes = _rope_inputs(pos0, 1, r)
    else:
        kb, r = 1, min(tm, r3)
        grid = (r3 // r,)
        tok = lambda i: (0, i, 0)
        tab = lambda i: (i, 0, 0)
        ada_spec = pl.BlockSpec((1, N_ADA, D_MODEL), lambda i: (0, 0, 0))
        base, off, lanes = _rope_inputs(pos0, r3 // r, r)
    n = kb * r
    ntok = b3 * r3
    gf, gm, gq, gk, seg, wg, wu, wd, win = wts
    base_spec = pl.BlockSpec((1, 2, LANES), tab)
    if per_seq:
        att_spec = pl.BlockSpec((kb, r, D_ATT), tok)
        att_shape = jax.ShapeDtypeStruct((b3, r3, D_ATT), F32)
    else:
        att_spec = pl.BlockSpec((D_ATT // LANES, n, LANES), lambda i: (0, i, 0))
        att_shape = jax.ShapeDtypeStruct((D_ATT // LANES, ntok, LANES), F32)
    out_shape = (jax.ShapeDtypeStruct(x3.shape, F32), att_shape, att_shape, att_shape,
                 jax.ShapeDtypeStruct((SUPER, ntok // CHUNK, CW), F32))
    out_specs = (pl.BlockSpec((kb, r, D_MODEL), tok), att_spec, att_spec, att_spec,
                 pl.BlockSpec((SUPER, n // CHUNK, CW), lambda i: (0, i, 0)))
    keep_tiles = 0
    if not per_seq:
        assert keep % r == 0
        keep_tiles = keep // r
        first = grid[0] - keep_tiles
        tail_spec = pl.BlockSpec((D_ATT, r), lambda i: (0, jnp.maximum(i - first, 0)))
        out_shape += (jax.ShapeDtypeStruct((D_ATT, keep), F32),) * 2
        out_specs += (tail_spec, tail_spec)
    return pl.pallas_call(
        functools.partial(_stage_a_kernel, not per_seq, keep_tiles),
        out_shape=out_shape,
        grid=grid,
        in_specs=[pl.BlockSpec((kb, r, D_MODEL), tok), ada_spec, base_spec,
                  _const_spec(off.shape), _const_spec(lanes.shape),
                  _const_spec(gf.shape), _const_spec(gm.shape), _const_spec(gq.shape),
                  _const_spec(gk.shape), _const_spec(seg.shape), _const_spec(wg.shape),
                  _const_spec(wu.shape), _const_spec(wd.shape), _const_spec(win.shape)],
        out_specs=out_specs,
        scratch_shapes=[pltpu.VMEM((SUPER, n, LANES), F32)],
        compiler_params=_params(("arbitrary",)),
        name="stage_a",
    )(x3, ada3, base, off, lanes, gf, gm, gq, gk, seg, wg, wu, wd, win)


UNITS_PER_ITER = 4


def _rows(start, size, stride):
    return pl.ds(start, size) if stride == 1 else pl.ds(start, size, stride=stride)


DIL_PARTS = 4
DIL_SHARE = len(DILATIONS) * DILATIONS[-1] // (UNITS_PER_ITER * DIL_PARTS)


def _dil_attn_part(i, part, side_work, q_ref, k_ref, v_ref, o_ref, kbuf, vbuf, acc, mrow, lrow):
    sb = q_ref.shape[1]
    tq = WIN_STEPS

    @pl.when(jnp.logical_and(part == 0, i == 0))
    def _():
        kbuf[0:sb, :] = jnp.zeros((sb, LANES), F32)
        vbuf[0:sb, :] = jnp.zeros((sb, LANES), F32)

    @pl.when(jnp.logical_and(part == 0, i > 0))
    def _():
        kbuf[0:sb, :] = kbuf[sb:2 * sb, :]
        vbuf[0:sb, :] = vbuf[sb:2 * sb, :]

    @pl.when(part == 0)
    def _():
        kbuf[sb:2 * sb, :] = k_ref[0]
        vbuf[sb:2 * sb, :] = v_ref[0]

    row = lax.broadcasted_iota(jnp.int32, (tq, 2 * tq), 0)
    col = lax.broadcasted_iota(jnp.int32, (tq, 2 * tq), 1)
    band = jnp.where(col >= row, 0.0, NEG)
    band = jnp.where(col <= row + tq, band, NEG)
    band0 = jnp.where(col >= tq, band, NEG)
    band_first = jnp.where(i == 0, band0, band)
    low = lax.broadcasted_iota(jnp.int32, (tq, LANES), 1) < HEAD_DIM
    high = jnp.logical_not(low)
    klow = lax.broadcasted_iota(jnp.int32, (2 * tq, LANES), 1) < HEAD_DIM
    ones_sel = jnp.concatenate([jnp.where(klow, 1.0, 0.0), jnp.where(klow, 0.0, 1.0)], axis=0).astype(BF16)

    def unit(d, qs, bias, mode):
        qp = (q_ref[0, _rows(qs, tq, d), :] * (HEAD_DIM ** -0.5)).astype(BF16)
        kp = kbuf[_rows(sb + qs - d * tq, 2 * tq, d), :].astype(BF16)
        vp = vbuf[_rows(sb + qs - d * tq, 2 * tq, d), :].astype(BF16)
        ps, mxs = [], []
        for sel in (low, high):
            qe = jnp.where(sel, qp, jnp.zeros_like(qp))
            s = lax.dot_general(qe, kp, _NT, preferred_element_type=F32) + bias
            mx = jnp.max(s, axis=1, keepdims=True)
            ps.append(jnp.exp(s - mx).astype(BF16))
            mxs.append(mx)
        zero = jnp.zeros_like(vp)
        vsel = jnp.concatenate([jnp.where(klow, vp, zero), jnp.where(klow, zero, vp)], axis=0)
        ol = _dot(jnp.concatenate(ps, axis=1), jnp.concatenate([vsel, ones_sel], axis=1))
        o, l = ol[:, 0:LANES], ol[:, LANES:]
        mx = jnp.where(low, mxs[0], mxs[1])
        rows = _rows(qs, tq, d)
        if mode != "init":
            m_old = mrow[rows, :]
            m_new = jnp.maximum(m_old, mx)
            a_old = jnp.exp(m_old - m_new)
            a_new = jnp.exp(mx - m_new)
            o = acc[rows, :] * a_old + o * a_new
            l = lrow[rows, :] * a_old + l * a_new
            mx = m_new
        if mode == "final":
            o_ref[0, rows, :] = o * (1.0 / l)
        else:
            acc[rows, :] = o
            mrow[rows, :] = mx
            lrow[rows, :] = l

    upi = UNITS_PER_ITER
    d16, d4 = DILATIONS[2], DILATIONS[1]
    nblk = sb // tq
    assert d16 == nblk and d4 % upi == 0

    def body16(g):
        for u in range(upi):
            unit(d16, g * upi + u, band_first, "init")

    def body4(g):
        mb = (g * upi) // d4
        bias = jnp.where(mb == 0, band_first, band)
        for u in range(upi):
            unit(d4, (g * upi) % d4 + u + d4 * tq * mb, bias, "merge")

    def body1(g):
        for u in range(upi):
            bias = jnp.where(g == 0, band_first, band) if u == 0 else band
            unit(1, (g * upi + u) * tq, bias, "final")

    counts = (nblk // upi,) * 3
    assert DIL_SHARE * DIL_PARTS == sum(counts)
    first = 0
    for body, cnt in zip((body16, body4, body1), counts):
        lo = jnp.clip(part * DIL_SHARE - first, 0, cnt)
        hi = jnp.clip((part + 1) * DIL_SHARE - first, 0, cnt)
        off = first - part * DIL_SHARE

        def wrapped(g, c, body=body, off=off):
            side_work(g + off)
            body(g)
            return c

        lax.fori_loop(lo, hi, wrapped, 0)
        first += cnt


def _sample_bias():
    w = WIN_MAX
    out = []
    i = (np.arange(N_HEADS * CHUNK) % CHUNK)[:, None]
    for d in DILATIONS:
        span = WIN_STEPS * d
        c = np.arange(w - span, w)[None, :]
        dist = w + i - c
        ok_buf = (dist % d == 0) & (dist <= span)
        cn = np.arange(LANES)[None, :] - (LANES - CHUNK)
        dn = i - cn
        ok_new = (cn >= 0) & (dn >= 0) & (dn % d == 0) & (dn <= span)
        ok = np.concatenate([ok_buf, ok_new], axis=1)
        out.append(jnp.asarray(np.where(ok, 0.0, NEG), F32))
    return out


SHIFT_HEAD_ROWS = 200
SHIFT_ROWS = -(-(D_ATT - SHIFT_HEAD_ROWS) // (8 * DIL_SHARE)) * 8


def _shift_rows(start, nrows, kc_ref, vc_ref, ko_ref, vo_ref, knt, vnt):
    w = kc_ref.shape[1]
    nq = CHUNK
    rows = pl.ds(start, nrows)
    lane = lax.broadcasted_iota(jnp.int32, (nrows, LANES), 1)
    for buf_ref, out_ref, new_ref in ((kc_ref, ko_ref, knt), (vc_ref, vo_ref, vnt)):
        rolled = pltpu.roll(buf_ref[rows, :], w - nq, 1)
        out_ref[0, rows, 0:w - LANES] = rolled[:, 0:w - LANES]
        out_ref[0, rows, w - LANES:] = jnp.where(lane >= LANES - nq, new_ref[rows, :],
                                                 rolled[:, w - LANES:])


def _sample_scores(q_ref, kn_ref, vn_ref, kc_ref, vc_ref, b1_ref, b2_ref, b3_ref, o_ref, knt, vnt):
    w = kc_ref.shape[1]
    nq = q_ref.shape[1]
    kc = kc_ref[...]
    vc = vc_ref[...]
    pad = jnp.zeros((LANES - nq, D_ATT), F32)
    knt[...] = jnp.concatenate([pad, kn_ref[0]], axis=0).T
    vnt[...] = jnp.concatenate([pad, vn_ref[0]], axis=0).T
    knb = knt[...].astype(BF16)
    vnb = vnt[...].astype(BF16)

    rows = N_HEADS * nq
    q = q_ref[0] * (HEAD_DIM ** -0.5)
    qt = jnp.concatenate([q] * N_HEADS, axis=0)
    rhead = lax.broadcasted_iota(jnp.int32, (rows, D_ATT), 0) // nq
    lhead = lax.broadcasted_iota(jnp.int32, (rows, D_ATT), 1) // HEAD_DIM
    qe = jnp.where(rhead == lhead, qt, 0.0).astype(BF16)
    kcb = kc.astype(BF16)
    vcb = vc.astype(BF16)
    s_buf = _dot(qe, kcb)
    s_new = _dot(qe, knb)

    outs, lses = [], []
    for d, b_ref in zip(DILATIONS, (b1_ref, b2_ref, b3_ref)):
        span = WIN_STEPS * d
        s = jnp.concatenate([s_buf[:, w - span:], s_new], axis=1) + b_ref[...]
        mx = jnp.max(s, axis=1, keepdims=True)
        p = jnp.exp(s - mx)
        l = jnp.sum(p, axis=1, keepdims=True)
        vcat = jnp.concatenate([vcb[:, w - span:], vnb], axis=1)
        o = lax.dot_general(p.astype(BF16), vcat, _NT, preferred_element_type=F32)
        outs.append(o * (1.0 / l))
        lses.append(mx + jnp.log(l))
    lmax = jnp.maximum(jnp.maximum(lses[0], lses[1]), lses[2])
    es = [jnp.exp(l - lmax) for l in lses]
    o = (es[0] * outs[0] + es[1] * outs[1] + es[2] * outs[2]) * (1.0 / (es[0] + es[1] + es[2]))
    o = jnp.where(rhead == lhead, o, 0.0)
    acc = o[0:nq]
    for h in range(1, N_HEADS):
        acc = acc + o[h * nq:(h + 1) * nq]
    o_ref[0] = acc


RING = 3


def _attention_kernel(nseq, nsteps, nsb, q_ref, kn_ref, vn_ref, kc_hbm, vc_hbm, b1_ref, b2_ref, b3_ref,
                      qp_ref, kp_ref, vp_ref, o_ref, ko_ref, vo_ref, op_ref,
                      kbuf, vbuf, acc, mrow, lrow, knt, vnt, kring, vring, sems):
    j = pl.program_id(0)

    def fetch(step):
        seq = jnp.minimum(step, nseq - 1)
        slot = step % RING
        return (pltpu.make_async_copy(kc_hbm.at[seq], kring.at[slot], sems.at[0, slot]),
                pltpu.make_async_copy(vc_hbm.at[seq], vring.at[slot], sems.at[1, slot]))

    @pl.when(j == 0)
    def _():
        for step in range(min(RING - 1, nsteps)):
            for cp in fetch(step):
                cp.start()

    @pl.when(j + RING - 1 < nsteps)
    def _():
        for cp in fetch(j + RING - 1):
            cp.start()

    for cp in fetch(j):
        cp.wait()
    kc_ref, vc_ref = kring.at[j % RING], vring.at[j % RING]
    _sample_scores(q_ref, kn_ref, vn_ref, kc_ref, vc_ref, b1_ref, b2_ref, b3_ref, o_ref, knt, vnt)
    shift = functools.partial(_shift_rows, kc_ref=kc_ref, vc_ref=vc_ref, ko_ref=ko_ref,
                              vo_ref=vo_ref, knt=knt, vnt=vnt)
    shift(0, SHIFT_HEAD_ROWS)

    def side_work(it):
        start = jnp.minimum(SHIFT_HEAD_ROWS + it * SHIFT_ROWS, D_ATT - SHIFT_ROWS)
        shift(pl.multiple_of(start, 8), SHIFT_ROWS)

    _dil_attn_part((j // DIL_PARTS) % nsb, j % DIL_PARTS, side_work, qp_ref, kp_ref, vp_ref, op_ref,
                   kbuf, vbuf, acc, mrow, lrow)


def _attention(q, kn, vn, kc, vc, qp, kp, vp):
    b, nq, _ = q.shape
    w = kc.shape[2]
    npair, s, _ = qp.shape
    sb = WIN_STEPS * DILATIONS[-1]
    assert w == WIN_MAX and nq == CHUNK and s % sb == 0 and DILATIONS[0] == 1
    nsb = s // sb
    ndil = npair * nsb * DIL_PARTS
    assert b <= ndil, "every sample sequence needs a grid step of the prompt attention"
    b1, b2, b3 = _sample_bias()
    seq = lambda j: (jnp.minimum(j, b - 1), 0, 0)
    new = pl.BlockSpec((1, nq, D_ATT), seq)
    buf = pl.BlockSpec((1, D_ATT, w), seq)
    blk = pl.BlockSpec((1, sb, LANES), lambda j: (j // (DIL_PARTS * nsb), (j // DIL_PARTS) % nsb, 0))
    hbm = pl.BlockSpec(memory_space=pl.ANY)
    return pl.pallas_call(
        functools.partial(_attention_kernel, b, ndil, nsb),
        out_shape=(jax.ShapeDtypeStruct(q.shape, F32), jax.ShapeDtypeStruct(kc.shape, F32),
                   jax.ShapeDtypeStruct(vc.shape, F32), jax.ShapeDtypeStruct(qp.shape, F32)),
        grid=(ndil,),
        in_specs=[new, new, new, hbm, hbm, _const_spec(b1.shape), _const_spec(b2.shape),
                  _const_spec(b3.shape), blk, blk, blk],
        out_specs=(new, buf, buf, blk),
        scratch_shapes=[pltpu.VMEM((2 * sb, LANES), F32), pltpu.VMEM((2 * sb, LANES), F32),
                        pltpu.VMEM((sb, LANES), F32), pltpu.VMEM((sb, LANES), F32),
                        pltpu.VMEM((sb, LANES), F32), pltpu.VMEM((D_ATT, LANES), F32),
                        pltpu.VMEM((D_ATT, LANES), F32), pltpu.VMEM((RING, D_ATT, w), F32),
                        pltpu.VMEM((RING, D_ATT, w), F32), pltpu.SemaphoreType.DMA((2, RING))],
        compiler_params=_params(("arbitrary",), RING_VMEM_LIMIT),
        name="attention",
    )(q, kn, vn, kc, vc, b1, b2, b3, qp, kp, vp)


def _split(a):
    hi = a.astype(BF16)
    return hi, (a - hi.astype(F32)).astype(BF16)


def _s5_prep_kernel(vec_ref, mat_ref, p_ref, qt_ref, t_ref, a_ref, pf_scr):
    ar, ai = vec_ref[0, 0:1, :], vec_ref[0, 1:2, :]
    dt = jnp.exp(vec_ref[0, 2:3, :])
    bre_ref, bim_ref, cre_ref, cim_ref = (mat_ref.at[m] for m in range(4))
    mag = jnp.exp(ar * dt)
    lr, li = mag * jnp.cos(ai * dt), mag * jnp.sin(ai * dt)
    den = 1.0 / (ar * ar + ai * ai)
    kr = ((lr - 1.0) * ar + li * ai) * den
    ki = (li * ar - (lr - 1.0) * ai) * den
    bre, bim = bre_ref[0], bim_ref[0]
    bbr, bbi = kr * bre - ki * bim, kr * bim + ki * bre
    cre, cim = cre_ref[0], cim_ref[0]
    pr, pi = [jnp.ones_like(lr)], [jnp.zeros_like(lr)]
    for _ in range(CHUNK):
        pr.append(pr[-1] * lr - pi[-1] * li)
        pi.append(pr[-2] * li + pi[-1] * lr)
    for s in range(CHUNK):
        n = CHUNK - 1 - s
        rows = slice(s * LANES, (s + 1) * LANES)
        pf_scr[rows, 0:SG_STATE] = bbr * pr[n] - bbi * pi[n]
        pf_scr[rows, SG_STATE:] = bbr * pi[n] + bbi * pr[n]
        n = s + 1
        re, im = cre * pr[n] - cim * pi[n], cre * pi[n] + cim * pr[n]
        qt_ref[0, rows, 0:SG_STATE] = re.astype(BF16)
        qt_ref[0, rows, SG_STATE:] = (-im).astype(BF16)
    pf = pf_scr[...]
    p_ref[0] = pf.astype(BF16)
    a_ref[0] = jnp.concatenate([pr[CHUNK], pi[CHUNK]], axis=-1)
    ph, plo = _split(pf)
    ch, clo = _split(jnp.concatenate([cre, -cim], axis=-1))
    lags = (lax.dot_general(ph, ch, _NT, preferred_element_type=F32)
            + lax.dot_general(ph, clo, _NT, preferred_element_type=F32)
            + lax.dot_general(plo, ch, _NT, preferred_element_type=F32)).astype(BF16)
    for tau in range(CHUNK):
        cols = slice(tau * LANES, (tau + 1) * LANES)
        nr = (tau + 1) * LANES
        t_ref[0, 0:nr, cols] = lags[CW - nr:, :]
        if nr < CW:
            t_ref[0, nr:, cols] = jnp.zeros((CW - nr, LANES), BF16)


def _block_diag(a):
    g = LANES // SSM_GROUP
    m, _, r, c = a.shape
    a = a.reshape(m, SUPER, g, r, c)
    rows = [jnp.pad(a[:, :, k], ((0, 0), (0, 0), (0, 0), (k * c, (g - 1 - k) * c))) for k in range(g)]
    return jnp.concatenate(rows, axis=2)


def _s5_prep(a_re, a_im, log_dt, b_re, b_im, c_re, c_im):
    ldt = jnp.broadcast_to(log_dt[:, None], (N_SSM_GROUPS, SSM_STATE))
    vecs = jnp.stack([a.reshape(SUPER, SG_STATE) for a in (a_re, a_im, ldt)], axis=1)
    mats = _block_diag(jnp.stack([jnp.swapaxes(b_re, 1, 2), jnp.swapaxes(b_im, 1, 2), c_re, c_im]))
    vec = pl.BlockSpec((1, 3, SG_STATE), lambda g: (g, 0, 0))
    mat = pl.BlockSpec((4, 1, LANES, SG_STATE), lambda g: (0, g, 0, 0))
    big = pl.BlockSpec((1, CW, CW), lambda g: (g, 0, 0))
    big_shape = jax.ShapeDtypeStruct((SUPER, CW, CW), BF16)
    return pl.pallas_call(
        _s5_prep_kernel,
        out_shape=(big_shape, big_shape, big_shape, jax.ShapeDtypeStruct((SUPER, 1, CW), F32)),
        grid=(SUPER,),
        in_specs=[vec, mat],
        out_specs=(big, big, big, pl.BlockSpec((1, 1, CW), lambda g: (g, 0, 0))),
        scratch_shapes=[pltpu.VMEM((CW, CW), F32)],
        compiler_params=_params(("arbitrary",)),
        name="s5_prep",
    )(vecs, mats)


def _gelu_tanh(y):
    return 0.5 * y * (1.0 + jnp.tanh(math.sqrt(2.0 / math.pi) * (y + 0.044715 * (y * y * y))))


def _s5_out(u, ub, hin, qt, t, d):
    y = _dot(ub, t) + lax.dot_general(hin.astype(BF16), qt, _NT, preferred_element_type=F32) + d * u
    return _gelu_tanh(y)


def _s5_scan_kernel(u_ref, p_ref, qt_ref, t_ref, a_ref, d_ref, y_ref, hout_ref,
                    h_scr, gs_scr, hin_scr):
    tc = u_ref.shape[1]

    @pl.when(pl.program_id(0) == 0)
    def _():
        h_scr[...] = jnp.zeros_like(h_scr)

    ubs = [u_ref[sg].astype(BF16) for sg in range(SUPER)]
    for sg in range(SUPER):
        gs_scr[sg] = _dot(ubs[sg], p_ref[sg])
    ar = [a_ref[sg][:, 0:SG_STATE] for sg in range(SUPER)]
    ai = [a_ref[sg][:, SG_STATE:] for sg in range(SUPER)]
    hr = [h_scr[sg][:, 0:SG_STATE] for sg in range(SUPER)]
    hi = [h_scr[sg][:, SG_STATE:] for sg in range(SUPER)]
    for k in range(tc):
        for sg in range(SUPER):
            hin_scr[sg, k:k + 1, 0:SG_STATE] = hr[sg]
            hin_scr[sg, k:k + 1, SG_STATE:] = hi[sg]
            g = gs_scr[sg, k:k + 1, :]
            hr[sg], hi[sg] = (ar[sg] * hr[sg] - ai[sg] * hi[sg] + g[:, 0:SG_STATE],
                              ar[sg] * hi[sg] + ai[sg] * hr[sg] + g[:, SG_STATE:])
    for sg in range(SUPER):
        h_new = jnp.concatenate([hr[sg], hi[sg]], axis=-1)
        h_scr[sg] = h_new
        hout_ref[sg] = h_new
        y_ref[sg] = _s5_out(u_ref[sg], ubs[sg], hin_scr[sg], qt_ref[sg], t_ref[sg], d_ref[sg])


def _s5_scan(u, mats, dvec, tc):
    p, qt, t, a = mats
    nch = u.shape[1]
    tc = min(tc, nch)
    tile = pl.BlockSpec((SUPER, tc, CW), lambda i: (0, i, 0))
    return pl.pallas_call(
        _s5_scan_kernel,
        out_shape=(jax.ShapeDtypeStruct(u.shape, F32), jax.ShapeDtypeStruct((SUPER, 1, CW), F32)),
        grid=(nch // tc,),
        in_specs=[tile, _const_spec(p.shape), _const_spec(qt.shape), _const_spec(t.shape),
                  _const_spec(a.shape), _const_spec(dvec.shape)],
        out_specs=(tile, _const_spec((SUPER, 1, CW))),
        scratch_shapes=[pltpu.VMEM((SUPER, 1, CW), F32), pltpu.VMEM((SUPER, tc, CW), F32),
                        pltpu.VMEM((SUPER, tc, CW), F32)],
        compiler_params=_params(("arbitrary",)),
        name="s5_scan",
    )(u, p, qt, t, a, dvec)


def _s5_step_kernel(u_ref, hr_ref, hi_ref, p_ref, qt_ref, t_ref, a_ref, d_ref,
                    y_ref, hro_ref, hio_ref):
    u = u_ref[0]
    ub = u.astype(BF16)
    hr, hi = hr_ref[...].T, hi_ref[...].T
    a = a_ref[0]
    ar, ai = a[:, 0:SG_STATE], a[:, SG_STATE:]
    gs = _dot(ub, p_ref[0])
    hro_ref[...] = (ar * hr - ai * hi + gs[:, 0:SG_STATE]).T
    hio_ref[...] = (ar * hi + ai * hr + gs[:, SG_STATE:]).T
    y_ref[0] = _s5_out(u, ub, jnp.concatenate([hr, hi], axis=-1), qt_ref[0], t_ref[0], d_ref[0])


def _s5_step(u, h_re, h_im, mats, dvec):
    p, qt, t, a = mats
    b = u.shape[1]
    tile = pl.BlockSpec((1, b, CW), lambda g: (g, 0, 0))
    st = pl.BlockSpec((SG_STATE, b), lambda g: (g, 0))
    big = pl.BlockSpec((1, CW, CW), lambda g: (g, 0, 0))
    vec = pl.BlockSpec((1, 1, CW), lambda g: (g, 0, 0))
    st_shape = jax.ShapeDtypeStruct(h_re.shape, F32)
    return pl.pallas_call(
        _s5_step_kernel,
        out_shape=(jax.ShapeDtypeStruct(u.shape, F32), st_shape, st_shape),
        grid=(SUPER,),
        in_specs=[tile, st, st, big, big, big, vec, vec],
        out_specs=(tile, st, st),
        compiler_params=_params(("arbitrary",)),
        name="s5_step",
    )(u, h_re, h_im, p, qt, t, a, dvec)


def _stage_b_kernel(pair_major, x_ref, ada_ref, att_ref, gs_ref, g2_ref, wglu_ref, wout_ref,
                    wg_ref, wu_ref, wd_ref, y_ref, g_scr):
    kb, r, _ = x_ref.shape
    n = kb * r
    x = x_ref[...]
    ada = ada_ref[...]
    gt2 = ada[:, 5:6]
    sh3, sc3, gt3 = ada[:, 6:7], ada[:, 7:8], ada[:, 8:9]

    if pair_major:
        o_att = jnp.concatenate([att_ref[hp] for hp in range(D_ATT // LANES)], axis=-1)
    else:
        o_att = att_ref[...].reshape(n, D_ATT)

    for sg in range(SUPER):
        for s in range(CHUNK):
            g_scr[sg, pl.ds(s, n // CHUNK, stride=CHUNK), :] = gs_ref[sg, :, s * LANES:(s + 1) * LANES]
    g = jnp.concatenate([g_scr[sg] for sg in range(SUPER)], axis=-1)
    gl = _dot(g.astype(BF16), wglu_ref[...].astype(BF16))
    o_ssm = gl[:, 0:D_SSM] * (1.0 / (1.0 + jnp.exp(-gl[:, D_SSM:])))
    mix = _dot(jnp.concatenate([o_att, o_ssm], axis=-1).astype(BF16), wout_ref[...].astype(BF16))
    x2 = x + gt2 * mix.reshape(kb, r, D_MODEL)

    h = _rms(x2, g2_ref[...]) * (1.0 + sc3) + sh3
    f = _swiglu(h.reshape(n, D_MODEL).astype(BF16), wg_ref, wu_ref, wd_ref)
    y_ref[...] = x2 + 0.5 * gt3 * f.reshape(kb, r, D_MODEL)


def _stage_b(x3, ada3, att, gs, per_seq, wts, tm):
    b3, r3, _ = x3.shape
    if per_seq:
        kb, r = min(tm // r3, b3), r3
        grid = (b3 // kb,)
        tok = lambda i: (i, 0, 0)
        ada_spec = pl.BlockSpec((kb, N_ADA, D_MODEL), lambda i: (i, 0, 0))
    else:
        kb, r = 1, min(tm, r3)
        grid = (r3 // r,)
        tok = lambda i: (0, i, 0)
        ada_spec = pl.BlockSpec((1, N_ADA, D_MODEL), lambda i: (0, 0, 0))
    n = kb * r
    g2, wglu, wout, wg, wu, wd = wts
    if per_seq:
        att_spec = pl.BlockSpec((kb, r, D_ATT), tok)
    else:
        att_spec = pl.BlockSpec((D_ATT // LANES, n, LANES), lambda i: (0, i, 0))
    return pl.pallas_call(
        functools.partial(_stage_b_kernel, not per_seq),
        out_shape=jax.ShapeDtypeStruct(x3.shape, F32),
        grid=grid,
        in_specs=[pl.BlockSpec((kb, r, D_MODEL), tok), ada_spec, att_spec,
                  pl.BlockSpec((SUPER, n // CHUNK, CW), lambda i: (0, i, 0)),
                  _const_spec(g2.shape), _const_spec(wglu.shape), _const_spec(wout.shape),
                  _const_spec(wg.shape), _const_spec(wu.shape), _const_spec(wd.shape)],
        out_specs=pl.BlockSpec((kb, r, D_MODEL), tok),
        scratch_shapes=[pltpu.VMEM((SUPER, n, LANES), F32)],
        compiler_params=_params(("arbitrary",)),
        name="stage_b",
    )(x3, ada3, att, gs, g2, wglu, wout, wg, wu, wd)


def _layer(x_p, x_s, c_p, c_s, cache_k, cache_v, st_re, st_im, p):
    (w_ada, b_ada, g_ffn1, w1_gate, w1_up, w1_down, g_mix, w_in, g_q, g_k,
     a_re, a_im, log_dt, b_re, b_im, c_re, c_im, d_skip, w_glu, w_out,
     g_ffn2, w2_gate, w2_up, w2_down) = p
    bp, s, _ = x_p.shape
    bs, ns, _ = x_s.shape
    assert bp == 1 and ns == CHUNK and s % (WIN_STEPS * DILATIONS[-1]) == 0
    w_buf = cache_k.shape[2]
    tm = 512

    rows = bp + bs
    pad = (-rows) % 8
    c_all = jnp.concatenate([c_p, c_s, jnp.zeros((pad, D_MODEL), F32)], axis=0)
    ada = _ada(c_all, w_ada, b_ada)
    ada_p = ada[0:bp].reshape(bp, N_ADA, D_MODEL)
    ada_s = ada[bp:rows].reshape(bs, N_ADA, D_MODEL)

    vec = lambda g: g.reshape(1, 1, -1)
    head_gain = lambda g: jnp.tile(g, N_HEADS).reshape(1, D_ATT)
    seg = jnp.asarray(np.kron(np.eye(2 * LANES // HEAD_DIM),
                              np.full((HEAD_DIM, HEAD_DIM), 1.0 / HEAD_DIM)), BF16)
    w1g, w1u, w1d, win, w2g, w2u, w2d = _cast_bf16(
        [w1_gate, w1_up, w1_down, w_in, w2_gate, w2_up, w2_down])
    wts_a = (vec(g_ffn1), vec(g_mix), head_gain(g_q), head_gain(g_k), seg, w1g, w1u, w1d, win)
    wts_b = (vec(g_ffn2), w_glu, w_out, w2g, w2u, w2d)

    keep = min(WIN_MAX, s)
    x1_p, q_p, k_p, v_p, u_p, kt_p, vt_p = _stage_a(x_p, ada_p, 0, False, wts_a, tm, keep)
    x1_s, q_s, k_s, v_s, u_s = _stage_a(x_s, ada_s, PAST_LEN, True, wts_a, tm)

    o_s, kwin_s, vwin_s, o_p = _attention(q_s, k_s, v_s, cache_k, cache_v, q_p, k_p, v_p)

    mats = _s5_prep(a_re, a_im, log_dt, b_re, b_im, c_re, c_im)
    dvec = jnp.tile(d_skip.reshape(SUPER, 1, LANES), (1, 1, CHUNK))
    gs_p, hfin_p = _s5_scan(u_p, mats, dvec, 256)
    gs_s, hre_s, him_s = _s5_step(u_s, st_re, st_im, mats, dvec)

    y_p = _stage_b(x1_p, ada_p, o_p, gs_p, False, wts_b, tm)
    y_s = _stage_b(x1_s, ada_s, o_s, gs_s, True, wts_b, tm)

    unflip = lambda a: jnp.transpose(a.reshape(-1, N_HEADS, HEAD_DIM, a.shape[-1]), (0, 3, 1, 2))
    hfin_p = hfin_p.reshape(SUPER, 2, SG_STATE)
    states = lambda h: h.reshape(-1, N_SSM_GROUPS, SSM_STATE)
    states_t = lambda h: jnp.transpose(h.reshape(N_SSM_GROUPS, SSM_STATE, bs), (2, 0, 1))
    return (y_p, y_s, unflip(kt_p[None]), unflip(vt_p[None]), states(hfin_p[:, 0]), states(hfin_p[:, 1]),
            unflip(kwin_s), unflip(vwin_s), states_t(hre_s), states_t(him_s))


def kernel(x_prompt, x_sample, c_prompt, c_sample, cache_k_win, cache_v_win, state_ssm_re, state_ssm_im, w_ada, b_ada, g_ffn1, w1_gate, w1_up, w1_down, g_mix, w_in, g_q, g_k, ssm_a_re, ssm_a_im, ssm_log_dt, ssm_b_re, ssm_b_im, ssm_c_re, ssm_c_im, ssm_d, w_glu, w_out, g_ffn2, w2_gate, w2_up, w2_down):
    depth = w_ada.shape[0]
    assert depth == 1
    bs = x_sample.shape[0]
    w_buf = cache_k_win.shape[2]
    p = tuple(a[0] for a in (w_ada, b_ada, g_ffn1, w1_gate, w1_up, w1_down, g_mix, w_in, g_q, g_k,
                             ssm_a_re, ssm_a_im, ssm_log_dt, ssm_b_re, ssm_b_im, ssm_c_re, ssm_c_im,
                             ssm_d, w_glu, w_out, g_ffn2, w2_gate, w2_up, w2_down))
    flip = lambda a: jnp.transpose(a[0], (0, 2, 3, 1)).reshape(bs, D_ATT, w_buf)
    state_t = lambda a: jnp.transpose(a[0], (1, 2, 0)).reshape(N_SSM_GROUPS * SSM_STATE, bs)
    outs = _layer(x_prompt, x_sample, c_prompt, c_sample, flip(cache_k_win), flip(cache_v_win),
                  state_t(state_ssm_re), state_t(state_ssm_im), p)
    return tuple(o[None] if i >= 2 else o for i, o in enumerate(outs))
```

```python
import functools
import math

import numpy as np
import jax
import jax.numpy as jnp
from jax import lax
from jax.experimental import pallas as pl
from jax.experimental.pallas import tpu as pltpu

F32 = jnp.float32
BF16 = jnp.bfloat16

D_MODEL = 1024
D_ATT = 512
D_SSM = 512
HEAD_DIM = 64
N_HEADS = 8
ROT_DIM = 16
ROPE_THETA = 500000.0
DILATIONS = (1, 4, 16)
WIN_STEPS = 128
WIN_MAX = 2048
PAST_LEN = 8192
SSM_GROUP = 16
N_SSM_GROUPS = 32
SSM_STATE = 64
D_FF = 2816
N_ADA = 9
EPS = 1e-6

LANES = 128
CHUNK = 8
SUPER = D_SSM // LANES
SG_STATE = (LANES // SSM_GROUP) * SSM_STATE
CW = CHUNK * LANES
NEG = -1e30
VMEM_LIMIT = 56 * 1024 * 1024
RING_VMEM_LIMIT = 60 * 1024 * 1024

_NT = (((1,), (1,)), ((), ()))


def _params(sem, vmem=VMEM_LIMIT):
    return pltpu.CompilerParams(dimension_semantics=sem, vmem_limit_bytes=vmem)


def _const_spec(shape):
    nd = len(shape)
    return pl.BlockSpec(shape, lambda *_: (0,) * nd, pipeline_mode=pl.Buffered(1))


def _dot(a, b):
    return jnp.dot(a, b, preferred_element_type=F32)


def _rms(x, g):
    ms = jnp.mean(x * x, axis=-1, keepdims=True)
    return x * lax.rsqrt(ms + EPS) * g


def _swiglu(h, wg_ref, wu_ref, wd_ref):
    a = _dot(h, wg_ref[...])
    b = _dot(h, wu_ref[...])
    t = (a * (1.0 / (1.0 + jnp.exp(-a))) * b).astype(BF16)
    return _dot(t, wd_ref[...])


def _cast_kernel(*refs):
    n = len(refs) // 2
    for src, dst in zip(refs[:n], refs[n:]):
        dst[...] = src[...].astype(BF16)


def _cast_bf16(ws, steps=8):
    specs = [pl.BlockSpec((w.shape[0] // steps, w.shape[1]), lambda i: (i, 0)) for w in ws]
    assert all(w.shape[0] % (16 * steps) == 0 for w in ws)
    return pl.pallas_call(
        _cast_kernel,
        out_shape=[jax.ShapeDtypeStruct(w.shape, BF16) for w in ws],
        grid=(steps,),
        in_specs=specs,
        out_specs=specs,
        compiler_params=_params(("arbitrary",)),
        name="cast_weights",
    )(*ws)


def _ada_kernel(c_ref, w_ref, b_ref, o_ref):
    c = c_ref[...]
    s = (c * (1.0 / (1.0 + jnp.exp(-c)))).astype(BF16)
    o_ref[...] = _dot(s, w_ref[...].astype(BF16)) + b_ref[...]


def _ada(c, w_ada, b_ada):
    m = c.shape[0]
    n = w_ada.shape[1]
    tn = n // 3
    return pl.pallas_call(
        _ada_kernel,
        out_shape=jax.ShapeDtypeStruct((m, n), F32),
        grid=(n // tn,),
        in_specs=[pl.BlockSpec((m, D_MODEL), lambda j: (0, 0)),
                  pl.BlockSpec((D_MODEL, tn), lambda j: (0, j)),
                  pl.BlockSpec((1, tn), lambda j: (0, j))],
        out_specs=pl.BlockSpec((m, tn), lambda j: (0, j)),
        compiler_params=_params(("arbitrary",)),
        name="ada",
    )(c, w_ada, b_ada.reshape(1, n))


def _stage_a_kernel(pair_major, keep_tiles, x_ref, ada_ref, base_ref, off_ref, lanes_ref, gf_ref,
                    gm_ref, gq_ref, gk_ref, seg_ref, wg_ref, wu_ref, wd_ref, win_ref,
                    x1_ref, q_ref, k_ref, v_ref, u_ref, *rest):
    kt_ref, vt_ref, u_scr = rest if pair_major else (None, None) + rest
    kb, r, _ = x_ref.shape
    n = kb * r
    x = x_ref[...]
    ada = ada_ref[...]
    sh1, sc1, gt1 = ada[:, 0:1], ada[:, 1:2], ada[:, 2:3]
    sh2, sc2 = ada[:, 3:4], ada[:, 4:5]

    h = _rms(x, gf_ref[...]) * (1.0 + sc1) + sh1
    f = _swiglu(h.reshape(n, D_MODEL).astype(BF16), wg_ref, wu_ref, wd_ref)
    x1 = x + 0.5 * gt1 * f.reshape(kb, r, D_MODEL)
    x1_ref[...] = x1

    h = _rms(x1, gm_ref[...]) * (1.0 + sc2) + sh2
    proj = _dot(h.reshape(n, D_MODEL).astype(BF16), win_ref[...])

    ca, sa = base_ref[0, 0:1, :], base_ref[0, 1:2, :]
    cb, sb = off_ref[0], off_ref[1]
    c, s = ca * cb - sa * sb, sa * cb + ca * sb
    rot, neg_lo, pos_hi = lanes_ref[0], lanes_ref[1], lanes_ref[2]
    tile4 = lambda t: jnp.concatenate([t] * (D_ATT // LANES), axis=-1)[None]
    cos, sina, sinb = tile4(c * rot + (1.0 - rot)), tile4(s * neg_lo), tile4(s * pos_hi)

    def head_norm_rope(z, g):
        z2 = (z * z).astype(BF16)
        sw = seg_ref.shape[0]
        ms = jnp.concatenate([_dot(z2[:, c:c + sw], seg_ref[...]) for c in range(0, D_ATT, sw)], axis=-1)
        zn = z * lax.rsqrt(ms + EPS) * g
        up = pltpu.roll(zn, D_ATT - ROT_DIM // 2, 1).reshape(kb, r, D_ATT)
        dn = pltpu.roll(zn, ROT_DIM // 2, 1).reshape(kb, r, D_ATT)
        return zn.reshape(kb, r, D_ATT) * cos + up * sina + dn * sinb

    q = head_norm_rope(proj[:, 0:D_ATT], gq_ref[...])
    k = head_norm_rope(proj[:, D_ATT:2 * D_ATT], gk_ref[...])
    v = proj[:, 2 * D_ATT:3 * D_ATT]
    if pair_major:
        k2 = k.reshape(n, D_ATT)
        for z, z_ref in ((q.reshape(n, D_ATT), q_ref), (k2, k_ref), (v, v_ref)):
            for hp in range(D_ATT // LANES):
                z_ref[hp] = z[:, hp * LANES:(hp + 1) * LANES]

        @pl.when(pl.program_id(0) >= pl.num_programs(0) - keep_tiles)
        def _():
            kt_ref[...] = k2.T
            vt_ref[...] = v.T
    else:
        q_ref[...] = q
        k_ref[...] = k
        v_ref[...] = v.reshape(kb, r, D_ATT)
    u = proj[:, 3 * D_ATT:]
    for sg in range(SUPER):
        u_scr[sg] = u[:, sg * LANES:(sg + 1) * LANES]
    for sg in range(SUPER):
        for s in range(CHUNK):
            u_ref[sg, :, s * LANES:(s + 1) * LANES] = u_scr[sg, pl.ds(s, n // CHUNK, stride=CHUNK), :]


def _rope_inputs(start, n_tiles, tile):
    half = ROT_DIM // 2
    j = np.arange(LANES) % HEAD_DIM
    inv = ROPE_THETA ** (-(j % half).astype(np.float64) / half)
    a = (start + tile * np.arange(n_tiles))[:, None] * inv[None, :]
    b = np.arange(tile)[:, None] * inv[None, :]
    lo = (j < half).astype(np.float32)
    hi = ((j >= half) & (j < ROT_DIM)).astype(np.float32)
    lanes = np.stack([lo + hi, -lo, hi])[:, None, :]
    base = np.stack([np.cos(a), np.sin(a)], axis=1).astype(np.float32)
    off = np.stack([np.cos(b), np.sin(b)]).astype(np.float32)
    return jnp.asarray(base), jnp.asarray(off), jnp.asarray(lanes)


def _stage_a(x3, ada3, pos0, per_seq, wts, tm, keep=0):
    b3, r3, _ = x3.shape
    if per_seq:
        kb, r = min(tm // r3, b3), r3
        grid = (b3 // kb,)
        tok = lambda i: (i, 0, 0)
        tab = lambda i: (0, 0, 0)
        ada_spec = pl.BlockSpec((kb, N_ADA, D_MODEL), lambda i: (i, 0, 0))
        base, off, lanes = _rope_inputs(pos0, 1, r)
    else:
        kb, r = 1, min(tm, r3)
        grid = (r3 // r,)
        tok = lambda i: (0, i, 0)
        tab = lambda i: (i, 0, 0)
        ada_spec = pl.BlockSpec((1, N_ADA, D_MODEL), lambda i: (0, 0, 0))
        base, off, lanes = _rope_inputs(pos0, r3 // r, r)
    n = kb * r
    ntok = b3 * r3
    gf, gm, gq, gk, seg, wg, wu, wd, win = wts
    base_spec = pl.BlockSpec((1, 2, LANES), tab)
    if per_seq:
        att_spec = pl.BlockSpec((kb, r, D_ATT), tok)
        att_shape = jax.ShapeDtypeStruct((b3, r3, D_ATT), F32)
    else:
        att_spec = pl.BlockSpec((D_ATT // LANES, n, LANES), lambda i: (0, i, 0))
        att_shape = jax.ShapeDtypeStruct((D_ATT // LANES, ntok, LANES), F32)
    out_shape = (jax.ShapeDtypeStruct(x3.shape, F32), att_shape, att_shape, att_shape,
                 jax.ShapeDtypeStruct((SUPER, ntok // CHUNK, CW), F32))
    out_specs = (pl.BlockSpec((kb, r, D_MODEL), tok), att_spec, att_spec, att_spec,
                 pl.BlockSpec((SUPER, n // CHUNK, CW), lambda i: (0, i, 0)))
    keep_tiles = 0
    if not per_seq:
        assert keep % r == 0
        keep_tiles = keep // r
        first = grid[0] - keep_tiles
        tail_spec = pl.BlockSpec((D_ATT, r), lambda i: (0, jnp.maximum(i - first, 0)))
        out_shape += (jax.ShapeDtypeStruct((D_ATT, keep), F32),) * 2
        out_specs += (tail_spec, tail_spec)
    return pl.pallas_call(
        functools.partial(_stage_a_kernel, not per_seq, keep_tiles),
        out_shape=out_shape,
        grid=grid,
        in_specs=[pl.BlockSpec((kb, r, D_MODEL), tok), ada_spec, base_spec,
                  _const_spec(off.shape), _const_spec(lanes.shape),
                  _const_spec(gf.shape), _const_spec(gm.shape), _const_spec(gq.shape),
                  _const_spec(gk.shape), _const_spec(seg.shape), _const_spec(wg.shape),
                  _const_spec(wu.shape), _const_spec(wd.shape), _const_spec(win.shape)],
        out_specs=out_specs,
        scratch_shapes=[pltpu.VMEM((SUPER, n, LANES), F32)],
        compiler_params=_params(("arbitrary",)),
        name="stage_a",
    )(x3, ada3, base, off, lanes, gf, gm, gq, gk, seg, wg, wu, wd, win)


UNITS_PER_ITER = 4


def _rows(start, size, stride):
    return pl.ds(start, size) if stride == 1 else pl.ds(start, size, stride=stride)


DIL_PARTS = 4
DIL_SHARE = len(DILATIONS) * DILATIONS[-1] // (UNITS_PER_ITER * DIL_PARTS)


def _dil_attn_part(i, part, side_work, q_ref, k_ref, v_ref, o_ref, kbuf, vbuf, acc, mrow, lrow):
    sb = q_ref.shape[1]
    tq = WIN_STEPS

    @pl.when(jnp.logical_and(part == 0, i == 0))
    def _():
        kbuf[0:sb, :] = jnp.zeros((sb, LANES), F32)
        vbuf[0:sb, :] = jnp.zeros((sb, LANES), F32)

    @pl.when(jnp.logical_and(part == 0, i > 0))
    def _():
        kbuf[0:sb, :] = kbuf[sb:2 * sb, :]
        vbuf[0:sb, :] = vbuf[sb:2 * sb, :]

    @pl.when(part == 0)
    def _():
        kbuf[sb:2 * sb, :] = k_ref[0]
        vbuf[sb:2 * sb, :] = v_ref[0]

    row = lax.broadcasted_iota(jnp.int32, (tq, 2 * tq), 0)
    col = lax.broadcasted_iota(jnp.int32, (tq, 2 * tq), 1)
    band = jnp.where(col >= row, 0.0, NEG)
    band = jnp.where(col <= row + tq, band, NEG)
    band0 = jnp.where(col >= tq, band, NEG)
    band_first = jnp.where(i == 0, band0, band)
    low = lax.broadcasted_iota(jnp.int32, (tq, LANES), 1) < HEAD_DIM
    high = jnp.logical_not(low)
    klow = lax.broadcasted_iota(jnp.int32, (2 * tq, LANES), 1) < HEAD_DIM
    ones_sel = jnp.concatenate([jnp.where(klow, 1.0, 0.0), jnp.where(klow, 0.0, 1.0)], axis=0).astype(BF16)

    def unit(d, qs, bias, mode):
        qp = (q_ref[0, _rows(qs, tq, d), :] * (HEAD_DIM ** -0.5)).astype(BF16)
        kp = kbuf[_rows(sb + qs - d * tq, 2 * tq, d), :].astype(BF16)
        vp = vbuf[_rows(sb + qs - d * tq, 2 * tq, d), :].astype(BF16)
        ps, mxs = [], []
        for sel in (low, high):
            qe = jnp.where(sel, qp, jnp.zeros_like(qp))
            s = lax.dot_general(qe, kp, _NT, preferred_element_type=F32) + bias
            mx = jnp.max(s, axis=1, keepdims=True)
            ps.append(jnp.exp(s - mx).astype(BF16))
            mxs.append(mx)
        zero = jnp.zeros_like(vp)
        vsel = jnp.concatenate([jnp.where(klow, vp, zero), jnp.where(klow, zero, vp)], axis=0)
        ol = _dot(jnp.concatenate(ps, axis=1), jnp.concatenate([vsel, ones_sel], axis=1))
        o, l = ol[:, 0:LANES], ol[:, LANES:]
        mx = jnp.where(low, mxs[0], mxs[1])
        rows = _rows(qs, tq, d)
        if mode != "init":
            m_old = mrow[rows, :]
            m_new = jnp.maximum(m_old, mx)
            a_old = jnp.exp(m_old - m_new)
            a_new = jnp.exp(mx - m_new)
            o = acc[rows, :] * a_old + o * a_new
            l = lrow[rows, :] * a_old + l * a_new
            mx = m_new
        if mode == "final":
            o_ref[0, rows, :] = o * (1.0 / l)
        else:
            acc[rows, :] = o
            mrow[rows, :] = mx
            lrow[rows, :] = l

    upi = UNITS_PER_ITER
    d16, d4 = DILATIONS[2], DILATIONS[1]
    nblk = sb // tq
    assert d16 == nblk and d4 % upi == 0

    def body16(g):
        for u in range(upi):
            unit(d16, g * upi + u, band_first, "init")

    def body4(g):
        mb = (g * upi) // d4
        bias = jnp.where(mb == 0, band_first, band)
        for u in range(upi):
            unit(d4, (g * upi) % d4 + u + d4 * tq * mb, bias, "merge")

    def body1(g):
        for u in range(upi):
            bias = jnp.where(g == 0, band_first, band) if u == 0 else band
            unit(1, (g * upi + u) * tq, bias, "final")

    counts = (nblk // upi,) * 3
    assert DIL_SHARE * DIL_PARTS == sum(counts)
    first = 0
    for body, cnt in zip((body16, body4, body1), counts):
        lo = jnp.clip(part * DIL_SHARE - first, 0, cnt)
        hi = jnp.clip((part + 1) * DIL_SHARE - first, 0, cnt)
        off = first - part * DIL_SHARE

        def wrapped(g, c, body=body, off=off):
            side_work(g + off)
            body(g)
            return c

        lax.fori_loop(lo, hi, wrapped, 0)
        first += cnt


def _sample_bias():
    w = WIN_MAX
    out = []
    i = (np.arange(N_HEADS * CHUNK) % CHUNK)[:, None]
    for d in DILATIONS:
        span = WIN_STEPS * d
        c = np.arange(w - span, w)[None, :]
        dist = w + i - c
        ok_buf = (dist % d == 0) & (dist <= span)
        cn = np.arange(LANES)[None, :] - (LANES - CHUNK)
        dn = i - cn
        ok_new = (cn >= 0) & (dn >= 0) & (dn % d == 0) & (dn <= span)
        ok = np.concatenate([ok_buf, ok_new], axis=1)
        out.append(jnp.asarray(np.where(ok, 0.0, NEG), F32))
    return out


SHIFT_HEAD_ROWS = 200
SHIFT_ROWS = -(-(D_ATT - SHIFT_HEAD_ROWS) // (8 * DIL_SHARE)) * 8


def _shift_rows(start, nrows, kc_ref, vc_ref, ko_ref, vo_ref, knt, vnt):
    w = kc_ref.shape[1]
    nq = CHUNK
    rows = pl.ds(start, nrows)
    lane = lax.broadcasted_iota(jnp.int32, (nrows, LANES), 1)
    for buf_ref, out_ref, new_ref in ((kc_ref, ko_ref, knt), (vc_ref, vo_ref, vnt)):
        rolled = pltpu.roll(buf_ref[rows, :], w - nq, 1)
        out_ref[0, rows, 0:w - LANES] = rolled[:, 0:w - LANES]
        out_ref[0, rows, w - LANES:] = jnp.where(lane >= LANES - nq, new_ref[rows, :],
                                                 rolled[:, w - LANES:])


def _sample_scores(q_ref, kn_ref, vn_ref, kc_ref, vc_ref, b1_ref, b2_ref, b3_ref, o_ref, knt, vnt):
    w = kc_ref.shape[1]
    nq = q_ref.shape[1]
    kc = kc_ref[...]
    vc = vc_ref[...]
    pad = jnp.zeros((LANES - nq, D_ATT), F32)
    knt[...] = jnp.concatenate([pad, kn_ref[0]], axis=0).T
    vnt[...] = jnp.concatenate([pad, vn_ref[0]], axis=0).T
    knb = knt[...].astype(BF16)
    vnb = vnt[...].astype(BF16)

    rows = N_HEADS * nq
    q = q_ref[0] * (HEAD_DIM ** -0.5)
    qt = jnp.concatenate([q] * N_HEADS, axis=0)
    rhead = lax.broadcasted_iota(jnp.int32, (rows, D_ATT), 0) // nq
    lhead = lax.broadcasted_iota(jnp.int32, (rows, D_ATT), 1) // HEAD_DIM
    qe = jnp.where(rhead == lhead, qt, 0.0).astype(BF16)
    kcb = kc.astype(BF16)
    vcb = vc.astype(BF16)
    s_buf = _dot(qe, kcb)
    s_new = _dot(qe, knb)

    outs, lses = [], []
    for d, b_ref in zip(DILATIONS, (b1_ref, b2_ref, b3_ref)):
        span = WIN_STEPS * d
        s = jnp.concatenate([s_buf[:, w - span:], s_new], axis=1) + b_ref[...]
        mx = jnp.max(s, axis=1, keepdims=True)
        p = jnp.exp(s - mx)
        l = jnp.sum(p, axis=1, keepdims=True)
        vcat = jnp.concatenate([vcb[:, w - span:], vnb], axis=1)
        o = lax.dot_general(p.astype(BF16), vcat, _NT, preferred_element_type=F32)
        outs.append(o * (1.0 / l))
        lses.append(mx + jnp.log(l))
    lmax = jnp.maximum(jnp.maximum(lses[0], lses[1]), lses[2])
    es = [jnp.exp(l - lmax) for l in lses]
    o = (es[0] * outs[0] + es[1] * outs[1] + es[2] * outs[2]) * (1.0 / (es[0] + es[1] + es[2]))
    o = jnp.where(rhead == lhead, o, 0.0)
    acc = o[0:nq]
    for h in range(1, N_HEADS):
        acc = acc + o[h * nq:(h + 1) * nq]
    o_ref[0] = acc


RING = 3


def _attention_kernel(nseq, nsteps, nsb, q_ref, kn_ref, vn_ref, kc_hbm, vc_hbm, b1_ref, b2_ref, b3_ref,
                      qp_ref, kp_ref, vp_ref, o_ref, ko_ref, vo_ref, op_ref,
                      kbuf, vbuf, acc, mrow, lrow, knt, vnt, kring, vring, sems):
    j = pl.program_id(0)

    def fetch(step):
        seq = jnp.minimum(step, nseq - 1)
        slot = step % RING
        return (pltpu.make_async_copy(kc_hbm.at[seq], kring.at[slot], sems.at[0, slot]),
                pltpu.make_async_copy(vc_hbm.at[seq], vring.at[slot], sems.at[1, slot]))

    @pl.when(j == 0)
    def _():
        for step in range(min(RING - 1, nsteps)):
            for thread, cp in enumerate(fetch(step)):
                cp.start(priority=thread)

    @pl.when(j + RING - 1 < nsteps)
    def _():
        for thread, cp in enumerate(fetch(j + RING - 1)):
            cp.start(priority=thread)

    for cp in fetch(j):
        cp.wait()
    kc_ref, vc_ref = kring.at[j % RING], vring.at[j % RING]
    _sample_scores(q_ref, kn_ref, vn_ref, kc_ref, vc_ref, b1_ref, b2_ref, b3_ref, o_ref, knt, vnt)
    shift = functools.partial(_shift_rows, kc_ref=kc_ref, vc_ref=vc_ref, ko_ref=ko_ref,
                              vo_ref=vo_ref, knt=knt, vnt=vnt)
    shift(0, SHIFT_HEAD_ROWS)

    def side_work(it):
        start = jnp.minimum(SHIFT_HEAD_ROWS + it * SHIFT_ROWS, D_ATT - SHIFT_ROWS)
        shift(pl.multiple_of(start, 8), SHIFT_ROWS)

    _dil_attn_part((j // DIL_PARTS) % nsb, j % DIL_PARTS, side_work, qp_ref, kp_ref, vp_ref, op_ref,
                   kbuf, vbuf, acc, mrow, lrow)


def _attention(q, kn, vn, kc, vc, qp, kp, vp):
    b, nq, _ = q.shape
    w = kc.shape[2]
    npair, s, _ = qp.shape
    sb = WIN_STEPS * DILATIONS[-1]
    assert w == WIN_MAX and nq == CHUNK and s % sb == 0 and DILATIONS[0] == 1
    nsb = s // sb
    ndil = npair * nsb * DIL_PARTS
    assert b <= ndil, "every sample sequence needs a grid step of the prompt attention"
    b1, b2, b3 = _sample_bias()
    seq = lambda j: (jnp.minimum(j, b - 1), 0, 0)
    new = pl.BlockSpec((1, nq, D_ATT), seq)
    buf = pl.BlockSpec((1, D_ATT, w), seq)
    blk = pl.BlockSpec((1, sb, LANES), lambda j: (j // (DIL_PARTS * nsb), (j // DIL_PARTS) % nsb, 0))
    hbm = pl.BlockSpec(memory_space=pl.ANY)
    return pl.pallas_call(
        functools.partial(_attention_kernel, b, ndil, nsb),
        out_shape=(jax.ShapeDtypeStruct(q.shape, F32), jax.ShapeDtypeStruct(kc.shape, F32),
                   jax.ShapeDtypeStruct(vc.shape, F32), jax.ShapeDtypeStruct(qp.shape, F32)),
        grid=(ndil,),
        in_specs=[new, new, new, hbm, hbm, _const_spec(b1.shape), _const_spec(b2.shape),
                  _const_spec(b3.shape), blk, blk, blk],
        out_specs=(new, buf, buf, blk),
        scratch_shapes=[pltpu.VMEM((2 * sb, LANES), F32), pltpu.VMEM((2 * sb, LANES), F32),
                        pltpu.VMEM((sb, LANES), F32), pltpu.VMEM((sb, LANES), F32),
                        pltpu.VMEM((sb, LANES), F32), pltpu.VMEM((D_ATT, LANES), F32),
                        pltpu.VMEM((D_ATT, LANES), F32), pltpu.VMEM((RING, D_ATT, w), F32),
                        pltpu.VMEM((RING, D_ATT, w), F32), pltpu.SemaphoreType.DMA((2, RING))],
        compiler_params=_params(("arbitrary",), RING_VMEM_LIMIT),
        name="attention",
    )(q, kn, vn, kc, vc, b1, b2, b3, qp, kp, vp)


def _split(a):
    hi = a.astype(BF16)
    return hi, (a - hi.astype(F32)).astype(BF16)


def _s5_prep_kernel(vec_ref, mat_ref, p_ref, qt_ref, t_ref, a_ref, pf_scr):
    ar, ai = vec_ref[0, 0:1, :], vec_ref[0, 1:2, :]
    dt = jnp.exp(vec_ref[0, 2:3, :])
    bre_ref, bim_ref, cre_ref, cim_ref = (mat_ref.at[m] for m in range(4))
    mag = jnp.exp(ar * dt)
    lr, li = mag * jnp.cos(ai * dt), mag * jnp.sin(ai * dt)
    den = 1.0 / (ar * ar + ai * ai)
    kr = ((lr - 1.0) * ar + li * ai) * den
    ki = (li * ar - (lr - 1.0) * ai) * den
    bre, bim = bre_ref[0], bim_ref[0]
    bbr, bbi = kr * bre - ki * bim, kr * bim + ki * bre
    cre, cim = cre_ref[0], cim_ref[0]
    pr, pi = [jnp.ones_like(lr)], [jnp.zeros_like(lr)]
    for _ in range(CHUNK):
        pr.append(pr[-1] * lr - pi[-1] * li)
        pi.append(pr[-2] * li + pi[-1] * lr)
    for s in range(CHUNK):
        n = CHUNK - 1 - s
        rows = slice(s * LANES, (s + 1) * LANES)
        pf_scr[rows, 0:SG_STATE] = bbr * pr[n] - bbi * pi[n]
        pf_scr[rows, SG_STATE:] = bbr * pi[n] + bbi * pr[n]
        n = s + 1
        re, im = cre * pr[n] - cim * pi[n], cre * pi[n] + cim * pr[n]
        qt_ref[0, rows, 0:SG_STATE] = re.astype(BF16)
        qt_ref[0, rows, SG_STATE:] = (-im).astype(BF16)
    pf = pf_scr[...]
    p_ref[0] = pf.astype(BF16)
    a_ref[0] = jnp.concatenate([pr[CHUNK], pi[CHUNK]], axis=-1)
    ph, plo = _split(pf)
    ch, clo = _split(jnp.concatenate([cre, -cim], axis=-1))
    lags = (lax.dot_general(ph, ch, _NT, preferred_element_type=F32)
            + lax.dot_general(ph, clo, _NT, preferred_element_type=F32)
            + lax.dot_general(plo, ch, _NT, preferred_element_type=F32)).astype(BF16)
    for tau in range(CHUNK):
        cols = slice(tau * LANES, (tau + 1) * LANES)
        nr = (tau + 1) * LANES
        t_ref[0, 0:nr, cols] = lags[CW - nr:, :]
        if nr < CW:
            t_ref[0, nr:, cols] = jnp.zeros((CW - nr, LANES), BF16)


def _block_diag(a):
    g = LANES // SSM_GROUP
    m, _, r, c = a.shape
    a = a.reshape(m, SUPER, g, r, c)
    rows = [jnp.pad(a[:, :, k], ((0, 0), (0, 0), (0, 0), (k * c, (g - 1 - k) * c))) for k in range(g)]
    return jnp.concatenate(rows, axis=2)


def _s5_prep(a_re, a_im, log_dt, b_re, b_im, c_re, c_im):
    ldt = jnp.broadcast_to(log_dt[:, None], (N_SSM_GROUPS, SSM_STATE))
    vecs = jnp.stack([a.reshape(SUPER, SG_STATE) for a in (a_re, a_im, ldt)], axis=1)
    mats = _block_diag(jnp.stack([jnp.swapaxes(b_re, 1, 2), jnp.swapaxes(b_im, 1, 2), c_re, c_im]))
    vec = pl.BlockSpec((1, 3, SG_STATE), lambda g: (g, 0, 0))
    mat = pl.BlockSpec((4, 1, LANES, SG_STATE), lambda g: (0, g, 0, 0))
    big = pl.BlockSpec((1, CW, CW), lambda g: (g, 0, 0))
    big_shape = jax.ShapeDtypeStruct((SUPER, CW, CW), BF16)
    return pl.pallas_call(
        _s5_prep_kernel,
        out_shape=(big_shape, big_shape, big_shape, jax.ShapeDtypeStruct((SUPER, 1, CW), F32)),
        grid=(SUPER,),
        in_specs=[vec, mat],
        out_specs=(big, big, big, pl.BlockSpec((1, 1, CW), lambda g: (g, 0, 0))),
        scratch_shapes=[pltpu.VMEM((CW, CW), F32)],
        compiler_params=_params(("arbitrary",)),
        name="s5_prep",
    )(vecs, mats)


def _gelu_tanh(y):
    return 0.5 * y * (1.0 + jnp.tanh(math.sqrt(2.0 / math.pi) * (y + 0.044715 * (y * y * y))))


def _s5_out(u, ub, hin, qt, t, d):
    y = _dot(ub, t) + lax.dot_general(hin.astype(BF16), qt, _NT, preferred_element_type=F32) + d * u
    return _gelu_tanh(y)


def _s5_scan_kernel(u_ref, p_ref, qt_ref, t_ref, a_ref, d_ref, y_ref, hout_ref,
                    h_scr, gs_scr, hin_scr):
    tc = u_ref.shape[1]

    @pl.when(pl.program_id(0) == 0)
    def _():
        h_scr[...] = jnp.zeros_like(h_scr)

    ubs = [u_ref[sg].astype(BF16) for sg in range(SUPER)]
    for sg in range(SUPER):
        gs_scr[sg] = _dot(ubs[sg], p_ref[sg])
    ar = [a_ref[sg][:, 0:SG_STATE] for sg in range(SUPER)]
    ai = [a_ref[sg][:, SG_STATE:] for sg in range(SUPER)]
    hr = [h_scr[sg][:, 0:SG_STATE] for sg in range(SUPER)]
    hi = [h_scr[sg][:, SG_STATE:] for sg in range(SUPER)]
    for k in range(tc):
        for sg in range(SUPER):
            hin_scr[sg, k:k + 1, 0:SG_STATE] = hr[sg]
            hin_scr[sg, k:k + 1, SG_STATE:] = hi[sg]
            g = gs_scr[sg, k:k + 1, :]
            hr[sg], hi[sg] = (ar[sg] * hr[sg] - ai[sg] * hi[sg] + g[:, 0:SG_STATE],
                              ar[sg] * hi[sg] + ai[sg] * hr[sg] + g[:, SG_STATE:])
    for sg in range(SUPER):
        h_new = jnp.concatenate([hr[sg], hi[sg]], axis=-1)
        h_scr[sg] = h_new
        hout_ref[sg] = h_new
        y_ref[sg] = _s5_out(u_ref[sg], ubs[sg], hin_scr[sg], qt_ref[sg], t_ref[sg], d_ref[sg])


def _s5_scan(u, mats, dvec, tc):
    p, qt, t, a = mats
    nch = u.shape[1]
    tc = min(tc, nch)
    tile = pl.BlockSpec((SUPER, tc, CW), lambda i: (0, i, 0))
    return pl.pallas_call(
        _s5_scan_kernel,
        out_shape=(jax.ShapeDtypeStruct(u.shape, F32), jax.ShapeDtypeStruct((SUPER, 1, CW), F32)),
        grid=(nch // tc,),
        in_specs=[tile, _const_spec(p.shape), _const_spec(qt.shape), _const_spec(t.shape),
                  _const_spec(a.shape), _const_spec(dvec.shape)],
        out_specs=(tile, _const_spec((SUPER, 1, CW))),
        scratch_shapes=[pltpu.VMEM((SUPER, 1, CW), F32), pltpu.VMEM((SUPER, tc, CW), F32),
                        pltpu.VMEM((SUPER, tc, CW), F32)],
        compiler_params=_params(("arbitrary",)),
        name="s5_scan",
    )(u, p, qt, t, a, dvec)


def _s5_step_kernel(u_ref, hr_ref, hi_ref, p_ref, qt_ref, t_ref, a_ref, d_ref,
                    y_ref, hro_ref, hio_ref):
    u = u_ref[0]
    ub = u.astype(BF16)
    hr, hi = hr_ref[...].T, hi_ref[...].T
    a = a_ref[0]
    ar, ai = a[:, 0:SG_STATE], a[:, SG_STATE:]
    gs = _dot(ub, p_ref[0])
    hro_ref[...] = (ar * hr - ai * hi + gs[:, 0:SG_STATE]).T
    hio_ref[...] = (ar * hi + ai * hr + gs[:, SG_STATE:]).T
    y_ref[0] = _s5_out(u, ub, jnp.concatenate([hr, hi], axis=-1), qt_ref[0], t_ref[0], d_ref[0])


def _s5_step(u, h_re, h_im, mats, dvec):
    p, qt, t, a = mats
    b = u.shape[1]
    tile = pl.BlockSpec((1, b, CW), lambda g: (g, 0, 0))
    st = pl.BlockSpec((SG_STATE, b), lambda g: (g, 0))
    big = pl.BlockSpec((1, CW, CW), lambda g: (g, 0, 0))
    vec = pl.BlockSpec((1, 1, CW), lambda g: (g, 0, 0))
    st_shape = jax.ShapeDtypeStruct(h_re.shape, F32)
    return pl.pallas_call(
        _s5_step_kernel,
        out_shape=(jax.ShapeDtypeStruct(u.shape, F32), st_shape, st_shape),
        grid=(SUPER,),
        in_specs=[tile, st, st, big, big, big, vec, vec],
        out_specs=(tile, st, st),
        compiler_params=_params(("arbitrary",)),
        name="s5_step",
    )(u, h_re, h_im, p, qt, t, a, dvec)


def _stage_b_kernel(pair_major, x_ref, ada_ref, att_ref, gs_ref, g2_ref, wglu_ref, wout_ref,
                    wg_ref, wu_ref, wd_ref, y_ref, g_scr):
    kb, r, _ = x_ref.shape
    n = kb * r
    x = x_ref[...]
    ada = ada_ref[...]
    gt2 = ada[:, 5:6]
    sh3, sc3, gt3 = ada[:, 6:7], ada[:, 7:8], ada[:, 8:9]

    if pair_major:
        o_att = jnp.concatenate([att_ref[hp] for hp in range(D_ATT // LANES)], axis=-1)
    else:
        o_att = att_ref[...].reshape(n, D_ATT)

    for sg in range(SUPER):
        for s in range(CHUNK):
            g_scr[sg, pl.ds(s, n // CHUNK, stride=CHUNK), :] = gs_ref[sg, :, s * LANES:(s + 1) * LANES]
    g = jnp.concatenate([g_scr[sg] for sg in range(SUPER)], axis=-1)
    gl = _dot(g.astype(BF16), wglu_ref[...].astype(BF16))
    o_ssm = gl[:, 0:D_SSM] * (1.0 / (1.0 + jnp.exp(-gl[:, D_SSM:])))
    mix = _dot(jnp.concatenate([o_att, o_ssm], axis=-1).astype(BF16), wout_ref[...].astype(BF16))
    x2 = x + gt2 * mix.reshape(kb, r, D_MODEL)

    h = _rms(x2, g2_ref[...]) * (1.0 + sc3) + sh3
    f = _swiglu(h.reshape(n, D_MODEL).astype(BF16), wg_ref, wu_ref, wd_ref)
    y_ref[...] = x2 + 0.5 * gt3 * f.reshape(kb, r, D_MODEL)


def _stage_b(x3, ada3, att, gs, per_seq, wts, tm):
    b3, r3, _ = x3.shape
    if per_seq:
        kb, r = min(tm // r3, b3), r3
        grid = (b3 // kb,)
        tok = lambda i: (i, 0, 0)
        ada_spec = pl.BlockSpec((kb, N_ADA, D_MODEL), lambda i: (i, 0, 0))
    else:
        kb, r = 1, min(tm, r3)
        grid = (r3 // r,)
        tok = lambda i: (0, i, 0)
        ada_spec = pl.BlockSpec((1, N_ADA, D_MODEL), lambda i: (0, 0, 0))
    n = kb * r
    g2, wglu, wout, wg, wu, wd = wts
    if per_seq:
        att_spec = pl.BlockSpec((kb, r, D_ATT), tok)
    else:
        att_spec = pl.BlockSpec((D_ATT // LANES, n, LANES), lambda i: (0, i, 0))
    return pl.pallas_call(
        functools.partial(_stage_b_kernel, not per_seq),
        out_shape=jax.ShapeDtypeStruct(x3.shape, F32),
        grid=grid,
        in_specs=[pl.BlockSpec((kb, r, D_MODEL), tok), ada_spec, att_spec,
                  pl.BlockSpec((SUPER, n // CHUNK, CW), lambda i: (0, i, 0)),
                  _const_spec(g2.shape), _const_spec(wglu.shape), _const_spec(wout.shape),
                  _const_spec(wg.shape), _const_spec(wu.shape), _const_spec(wd.shape)],
        out_specs=pl.BlockSpec((kb, r, D_MODEL), tok),
        scratch_shapes=[pltpu.VMEM((SUPER, n, LANES), F32)],
        compiler_params=_params(("arbitrary",)),
        name="stage_b",
    )(x3, ada3, att, gs, g2, wglu, wout, wg, wu, wd)


def _layer(x_p, x_s, c_p, c_s, cache_k, cache_v, st_re, st_im, p):
    (w_ada, b_ada, g_ffn1, w1_gate, w1_up, w1_down, g_mix, w_in, g_q, g_k,
     a_re, a_im, log_dt, b_re, b_im, c_re, c_im, d_skip, w_glu, w_out,
     g_ffn2, w2_gate, w2_up, w2_down) = p
    bp, s, _ = x_p.shape
    bs, ns, _ = x_s.shape
    assert bp == 1 and ns == CHUNK and s % (WIN_STEPS * DILATIONS[-1]) == 0
    w_buf = cache_k.shape[2]
    tm = 512

    rows = bp + bs
    pad = (-rows) % 8
    c_all = jnp.concatenate([c_p, c_s, jnp.zeros((pad, D_MODEL), F32)], axis=0)
    ada = _ada(c_all, w_ada, b_ada)
    ada_p = ada[0:bp].reshape(bp, N_ADA, D_MODEL)
    ada_s = ada[bp:rows].reshape(bs, N_ADA, D_MODEL)

    vec = lambda g: g.reshape(1, 1, -1)
    head_gain = lambda g: jnp.tile(g, N_HEADS).reshape(1, D_ATT)
    seg = jnp.asarray(np.kron(np.eye(2 * LANES // HEAD_DIM),
                              np.full((HEAD_DIM, HEAD_DIM), 1.0 / HEAD_DIM)), BF16)
    w1g, w1u, w1d, win, w2g, w2u, w2d = _cast_bf16(
        [w1_gate, w1_up, w1_down, w_in, w2_gate, w2_up, w2_down])
    wts_a = (vec(g_ffn1), vec(g_mix), head_gain(g_q), head_gain(g_k), seg, w1g, w1u, w1d, win)
    wts_b = (vec(g_ffn2), w_glu, w_out, w2g, w2u, w2d)

    keep = min(WIN_MAX, s)
    x1_p, q_p, k_p, v_p, u_p, kt_p, vt_p = _stage_a(x_p, ada_p, 0, False, wts_a, tm, keep)
    x1_s, q_s, k_s, v_s, u_s = _stage_a(x_s, ada_s, PAST_LEN, True, wts_a, tm)

    o_s, kwin_s, vwin_s, o_p = _attention(q_s, k_s, v_s, cache_k, cache_v, q_p, k_p, v_p)

    mats = _s5_prep(a_re, a_im, log_dt, b_re, b_im, c_re, c_im)
    dvec = jnp.tile(d_skip.reshape(SUPER, 1, LANES), (1, 1, CHUNK))
    gs_p, hfin_p = _s5_scan(u_p, mats, dvec, 256)
    gs_s, hre_s, him_s = _s5_step(u_s, st_re, st_im, mats, dvec)

    y_p = _stage_b(x1_p, ada_p, o_p, gs_p, False, wts_b, tm)
    y_s = _stage_b(x1_s, ada_s, o_s, gs_s, True, wts_b, tm)

    unflip = lambda a: jnp.transpose(a.reshape(-1, N_HEADS, HEAD_DIM, a.shape[-1]), (0, 3, 1, 2))
    hfin_p = hfin_p.reshape(SUPER, 2, SG_STATE)
    states = lambda h: h.reshape(-1, N_SSM_GROUPS, SSM_STATE)
    states_t = lambda h: jnp.transpose(h.reshape(N_SSM_GROUPS, SSM_STATE, bs), (2, 0, 1))
    return (y_p, y_s, unflip(kt_p[None]), unflip(vt_p[None]), states(hfin_p[:, 0]), states(hfin_p[:, 1]),
            unflip(kwin_s), unflip(vwin_s), states_t(hre_s), states_t(him_s))


def kernel(x_prompt, x_sample, c_prompt, c_sample, cache_k_win, cache_v_win, state_ssm_re, state_ssm_im, w_ada, b_ada, g_ffn1, w1_gate, w1_up, w1_down, g_mix, w_in, g_q, g_k, ssm_a_re, ssm_a_im, ssm_log_dt, ssm_b_re, ssm_b_im, ssm_c_re, ssm_c_im, ssm_d, w_glu, w_out, g_ffn2, w2_gate, w2_up, w2_down):
    depth = w_ada.shape[0]
    assert depth == 1
    bs = x_sample.shape[0]
    w_buf = cache_k_win.shape[2]
    p = tuple(a[0] for a in (w_ada, b_ada, g_ffn1, w1_gate, w1_up, w1_down, g_mix, w_in, g_q, g_k,
                             ssm_a_re, ssm_a_im, ssm_log_dt, ssm_b_re, ssm_b_im, ssm_c_re, ssm_c_im,
                             ssm_d, w_glu, w_out, g_ffn2, w2_gate, w2_up, w2_down))
    flip = lambda a: jnp.transpose(a[0], (0, 2, 3, 1)).reshape(bs, D_ATT, w_buf)
    state_t = lambda a: jnp.transpose(a[0], (1, 2, 0)).reshape(N_SSM_GROUPS * SSM_STATE, bs)
    outs = _layer(x_prompt, x_sample, c_prompt, c_sample, flip(cache_k_win), flip(cache_v_win),
                  state_t(state_ssm_re), state_t(state_ssm_im), p)
    return tuple(o[None] if i >= 2 else o for i, o in enumerate(outs))
```
